```python
import jax, jax.numpy as jnp
from jax import lax
import numpy as np

D_MODEL = 1024
BATCH = 8
SEQ = 4096
DEPTH = 1

D_MIX = D_MODEL
D_HGRN = D_MIX // 2
HGRN_HEAD_DIM = 128
HGRN_HEADS = D_HGRN // HGRN_HEAD_DIM
HGRN_CHUNK = 64
D_ATTN = D_MIX - D_HGRN
ATTN_HEAD_DIM = 64
ATTN_HEADS = D_ATTN // ATTN_HEAD_DIM
DILATED_PATTERNS = ((128, 1), (512, 4), (2048, 16))
D_IN_PROJ = 4 * D_HGRN + 3 * D_ATTN
N_EXPERTS = 32
TOP_K = 4
D_EXPERT = D_MODEL
SWIGLU_ALPHA = 1.702
SWIGLU_LIMIT = 7.0
MOE_BLOCK = 128
NORM_EPS = 1e-6

kernel_name = 'hymba_hgrn2_dilated_moe_block'


def _rmsnorm(x, g):
    xf = x.astype(jnp.float32)
    r = lax.rsqrt(jnp.mean(xf * xf, axis=-1, keepdims=True) + NORM_EPS)
    return (xf * r).astype(x.dtype) * g


def _head_rmsnorm(o, g, n_heads):
    B, S, W = o.shape
    of = o.astype(jnp.float32).reshape(B, S, n_heads, W // n_heads)
    of = of * lax.rsqrt(jnp.mean(of * of, axis=-1, keepdims=True) + NORM_EPS)
    return of.reshape(B, S, W).astype(o.dtype) * g


def _hgrn2(q, f_logit, v, lb):
    B, S, H, K = q.shape
    C = HGRN_CHUNK
    NC = S // C
    f = lb + (1.0 - lb) * jax.nn.sigmoid(f_logit)
    log_f = jnp.log(f)
    k = 1.0 - f

    def to_chunks(t):
        return t.reshape(B, NC, C, H, t.shape[-1]).transpose(1, 0, 3, 2, 4)

    causal = jnp.tril(jnp.ones((C, C), dtype=bool))

    def step(state, inp):
        qc, kc, vc, gc = inp
        b = jnp.cumsum(gc, axis=2)
        o_inter = jnp.einsum('bhtk,bhkv->bhtv', qc * jnp.exp(b), state)
        diff = b[:, :, :, None, :] - b[:, :, None, :, :]
        decay = jnp.exp(jnp.where(causal[:, :, None], diff, -jnp.inf))
        scores = jnp.einsum('bhtk,bhsk,bhtsk->bhts', qc, kc, decay)
        o_intra = jnp.einsum('bhts,bhsv->bhtv', scores, vc)
        b_last = b[:, :, -1:, :]
        new_state = (jnp.exp(b_last[:, :, 0, :])[..., None] * state
                     + jnp.einsum('bhsk,bhsv->bhkv', kc * jnp.exp(b_last - b), vc))
        return new_state, o_inter + o_intra

    V = v.shape[-1]
    s0 = jnp.zeros((B, H, K, V), jnp.float32)
    _, o = lax.scan(step, s0, (to_chunks(q), to_chunks(k), to_chunks(v), to_chunks(log_f)))
    return o.transpose(1, 0, 3, 2, 4).reshape(B, S, H * V)


def _dilated_pattern(q, k, v, window, dil):
    B, S, H, Dh = q.shape
    blk = window // dil
    span = dil * blk
    L = -(-S // span) * span
    M = L // dil
    nb = M // blk

    def to_classes(t):
        t = jnp.pad(t, ((0, 0), (0, L - S), (0, 0), (0, 0)))
        return t.reshape(B, M, dil, H, Dh).transpose(0, 2, 1, 3, 4).reshape(B, dil, nb, blk, H, Dh)

    def with_prev(t):
        prev = jnp.pad(t, ((0, 0), (0, 0), (1, 0), (0, 0), (0, 0), (0, 0)))[:, :, :-1]
        return jnp.concatenate([prev, t], axis=3)

    qc = to_classes(q)
    kw = with_prev(to_classes(k))
    vw = with_prev(to_classes(v))
    s = jnp.einsum('brnqhd,brnkhd->brnhqk', qc, kw) * (Dh ** -0.5)
    qi = jnp.arange(blk)[:, None]
    ki = jnp.arange(2 * blk)[None, :]
    dist = qi + blk - ki
    band = (dist >= 0) & (dist <= blk)
    has_prev = (jnp.arange(nb) > 0)[:, None, None] | (ki >= blk)[None]
    mask = band[None] & has_prev
    s = jnp.where(mask[None, None, :, None], s, -jnp.inf)
    m = jnp.max(s, axis=-1, keepdims=True)
    p = jnp.exp(s - m)
    den = jnp.sum(p, axis=-1)
    den_q = jnp.transpose(den, (0, 1, 2, 4, 3))
    o = jnp.einsum('brnhqk,brnkhd->brnqhd', p, vw) / den_q[..., None]
    lse = jnp.transpose(m[..., 0], (0, 1, 2, 4, 3)) + jnp.log(den_q)

    def from_classes(t):
        t = t.reshape((B, dil, M) + t.shape[4:])
        t = jnp.moveaxis(t, 1, 2)
        return t.reshape((B, L) + t.shape[3:])[:, :S]

    return from_classes(o), from_classes(lse)


def _dilated_mixture(q, k, v):
    outs = []
    lses = []
    for window, dil in DILATED_PATTERNS:
        o_p, lse_p = _dilated_pattern(q, k, v, window, dil)
        outs.append(o_p)
        lses.append(lse_p)
    wts = jax.nn.softmax(jnp.stack(lses, axis=0), axis=0)
    return jnp.sum(jnp.stack(outs, axis=0) * wts[..., None], axis=0)


def _clamped_swiglu(u):
    glu = jnp.minimum(u[..., ::2], SWIGLU_LIMIT)
    lin = jnp.clip(u[..., 1::2], -SWIGLU_LIMIT, SWIGLU_LIMIT)
    return glu * jax.nn.sigmoid(SWIGLU_ALPHA * glu) * (lin + 1.0)


def _moe(h, router_w, router_b, w_up, b_up, w_down, b_down):
    N, D = h.shape
    NK = N * TOP_K
    logits = (h @ router_w + router_b).astype(jnp.float32)
    top_vals, top_idx = lax.top_k(logits, TOP_K)
    gates = jax.nn.softmax(top_vals, axis=-1)
    flat_e = top_idx.reshape(-1)
    flat_tok = jnp.arange(NK, dtype=jnp.int32) // TOP_K
    order = jnp.argsort(flat_e)
    sorted_e = flat_e[order]
    sorted_tok = flat_tok[order]
    counts = jnp.bincount(flat_e, length=N_EXPERTS)
    padded = (counts + MOE_BLOCK - 1) // MOE_BLOCK * MOE_BLOCK
    pad_end = jnp.cumsum(padded)
    pad_start = pad_end - padded
    start = jnp.cumsum(counts) - counts
    dest = pad_start[sorted_e] + jnp.arange(NK, dtype=jnp.int32) - start[sorted_e]
    R = NK + N_EXPERTS * MOE_BLOCK
    NB = R // MOE_BLOCK
    xbuf = jnp.zeros((R, D), h.dtype).at[dest].set(h[sorted_tok])
    block_e = jnp.clip(jnp.searchsorted(pad_end, jnp.arange(NB) * MOE_BLOCK, side='right'),
                       0, N_EXPERTS - 1)

    def expert_block(args):
        xb, e = args
        u = xb @ w_up[e] + b_up[e]
        return _clamped_swiglu(u) @ w_down[e] + b_down[e]

    ybuf = lax.map(expert_block, (xbuf.reshape(NB, MOE_BLOCK, D), block_e)).reshape(R, D)
    y = ybuf[dest] * gates.reshape(-1)[order][:, None].astype(ybuf.dtype)
    return jnp.zeros((N, D), h.dtype).at[sorted_tok].add(y)


def setup_inputs(seed: int = 0) -> dict:
    key = jax.random.key(seed)
    ks = jax.random.split(key, 16)
    f32 = jnp.float32
    nrm = lambda k, shape, scale: jax.random.normal(k, shape, f32) * scale
    return {
        'x': jax.random.normal(ks[0], (BATCH, SEQ, D_MODEL), f32),
        'norm1_g': 1.0 + nrm(ks[1], (DEPTH, D_MODEL), 0.02),
        'w_in': nrm(ks[2], (DEPTH, D_MODEL, D_IN_PROJ), D_MODEL ** -0.5),
        'hgrn_lb_logits': nrm(ks[3], (DEPTH + 1, D_HGRN), 0.5),
        'hgrn_norm_g': 1.0 + nrm(ks[4], (DEPTH, D_HGRN), 0.02),
        'attn_norm_g': 1.0 + nrm(ks[5], (DEPTH, D_ATTN), 0.02),
        'w_out': nrm(ks[6], (DEPTH, D_MIX, D_MODEL), D_MIX ** -0.5),
        'norm2_g': 1.0 + nrm(ks[7], (DEPTH, D_MODEL), 0.02),
        'router_w': nrm(ks[8], (DEPTH, D_MODEL, N_EXPERTS), D_MODEL ** -0.5),
        'router_b': nrm(ks[9], (DEPTH, N_EXPERTS), 0.01),
        'w_up': nrm(ks[10], (DEPTH, N_EXPERTS, D_MODEL, 2 * D_EXPERT), D_MODEL ** -0.5),
        'b_up': nrm(ks[11], (DEPTH, N_EXPERTS, 2 * D_EXPERT), 0.01),
        'w_down': nrm(ks[12], (DEPTH, N_EXPERTS, D_EXPERT, D_MODEL), D_EXPERT ** -0.5),
        'b_down': nrm(ks[13], (DEPTH, N_EXPERTS, D_MODEL), 0.01),
        'final_norm_g': 1.0 + nrm(ks[14], (D_MODEL,), 0.02),
    }


def reference(x, norm1_g, w_in, hgrn_lb_logits, hgrn_norm_g, attn_norm_g, w_out, norm2_g,
              router_w, router_b, w_up, b_up, w_down, b_down, final_norm_g):
    B, S, D = x.shape
    f32 = jnp.float32
    lb_all = jnp.cumsum(jax.nn.softmax(hgrn_lb_logits.astype(f32), axis=0), axis=0)
    splits = [D_HGRN, 2 * D_HGRN, 3 * D_HGRN, 4 * D_HGRN, 4 * D_HGRN + D_ATTN, 4 * D_HGRN + 2 * D_ATTN]
    for l in range(DEPTH):
        h = _rmsnorm(x, norm1_g[l])
        proj = h @ w_in[l]
        hq, hf, hi, hg, aq, ak, av = jnp.split(proj, splits, axis=-1)
        lb = lb_all[l].reshape(HGRN_HEADS, HGRN_HEAD_DIM)
        to_h = lambda t: t.astype(f32).reshape(B, S, HGRN_HEADS, HGRN_HEAD_DIM)
        o_a = _hgrn2(to_h(hq), to_h(hf), to_h(hi), lb).astype(x.dtype)
        o_a = _head_rmsnorm(o_a, hgrn_norm_g[l], HGRN_HEADS) * jax.nn.silu(hg)
        to_a = lambda t: t.astype(f32).reshape(B, S, ATTN_HEADS, ATTN_HEAD_DIM)
        o_b = _dilated_mixture(to_a(aq), to_a(ak), to_a(av)).reshape(B, S, D_ATTN).astype(x.dtype)
        o_b = _head_rmsnorm(o_b, attn_norm_g[l], ATTN_HEADS)
        x = x + jnp.concatenate([o_a, o_b], axis=-1) @ w_out[l]
        h2 = _rmsnorm(x, norm2_g[l]).reshape(B * S, D)
        x = x + _moe(h2, router_w[l], router_b[l], w_up[l], b_up[l], w_down[l], b_down[l]).reshape(B, S, D)
    return _rmsnorm(x, final_norm_g)
```

```python
import functools

import jax
import jax.numpy as jnp
from jax import lax
from jax.experimental import pallas as pl
from jax.experimental.pallas import tpu as pltpu

F32 = jnp.float32
BF16 = jnp.bfloat16

NORM_EPS = 1e-6
HGRN_HEAD_DIM = 128
ATTN_HEAD_DIM = 64
DILATED_PATTERNS = ((128, 1), (512, 4), (2048, 16))
TOP_K = 4
SWIGLU_ALPHA = 1.702
SWIGLU_LIMIT = 7.0

LANES = 128
SUBLANES = 8
HGRN_CHUNK = 64
ATTN_BLOCK = 128
MOE_ROWS = 512
VMEM_LIMIT = 56 * 1024 * 1024

_NT = (((1,), (1,)), ((), ()))
_TN = (((0,), (0,)), ((), ()))


def _cparams(sem):
    return pltpu.CompilerParams(dimension_semantics=sem, vmem_limit_bytes=VMEM_LIMIT)


def _in_proj_kernel(x_ref, g_ref, w_ref, *out_refs):
    x = x_ref[...]
    r = lax.rsqrt(jnp.mean(x * x, axis=-1, keepdims=True) + NORM_EPS)
    h = ((x * r) * g_ref[...]).astype(BF16)
    width = out_refs[0].shape[-1]
    for j, o_ref in enumerate(out_refs):
        o_ref[...] = jnp.dot(h, w_ref[:, j * width:(j + 1) * width],
                             preferred_element_type=F32).astype(o_ref.dtype)


def _in_proj(x2, gain, w_bf16, out_dtypes, tile_m):
    n, d = x2.shape
    width = w_bf16.shape[1] // len(out_dtypes)
    return pl.pallas_call(
        _in_proj_kernel,
        grid=(n // tile_m,),
        in_specs=[pl.BlockSpec((tile_m, d), lambda i: (i, 0)),
                  pl.BlockSpec((1, d), lambda i: (0, 0)),
                  pl.BlockSpec(w_bf16.shape, lambda i: (0, 0))],
        out_specs=[pl.BlockSpec((tile_m, width), lambda i: (i, 0)) for _ in out_dtypes],
        out_shape=[jax.ShapeDtypeStruct((n, width), dt) for dt in out_dtypes],
        compiler_params=_cparams(("arbitrary",)),
        name="in_proj",
    )(x2, gain, w_bf16)


def _hgrn_kernel(q_ref, f_ref, i_ref, g_ref, lb_ref, gain_ref, o_ref, state_ref, *, n_heads):
    hd = HGRN_HEAD_DIM
    width = n_heads * hd
    nblk = HGRN_CHUNK // SUBLANES

    @pl.when(pl.program_id(1) == 0)
    def _():
        state_ref[...] = jnp.zeros_like(state_ref)

    lb = lb_ref[...]
    gain = gain_ref[...]
    row8 = lax.broadcasted_iota(jnp.int32, (SUBLANES, width), 0)
    rowc = lax.broadcasted_iota(jnp.int32, (SUBLANES, HGRN_CHUNK), 0)
    colc = lax.broadcasted_iota(jnp.int32, (SUBLANES, HGRN_CHUNK), 1)
    r64 = lax.broadcasted_iota(jnp.int32, (HGRN_CHUNK, HGRN_CHUNK), 0)
    c64 = lax.broadcasted_iota(jnp.int32, (HGRN_CHUNK, HGRN_CHUNK), 1)
    same32 = (r64 // 32) == (c64 // 32)
    same16 = (r64 // 16) == (c64 // 16)

    def scan8(x):
        for s in (1, 2, 4):
            x = x + jnp.where(row8 >= s, pltpu.roll(x, s, axis=0), 0.0)
        return x

    def cat(blocks):
        return jnp.concatenate(blocks, axis=0)

    def chunk(ci, carry):
        r0 = pl.multiple_of(ci * HGRN_CHUNK, HGRN_CHUNK)
        rows = pl.ds(r0, HGRN_CHUNK)
        q = q_ref[0, rows, :].astype(F32)
        v = i_ref[0, rows, :]
        f = lb + (1.0 - lb) * jax.nn.sigmoid(f_ref[0, rows, :])
        logf = jnp.log(f)
        kk = 1.0 - f

        qb = [q[SUBLANES * i:SUBLANES * (i + 1)] for i in range(nblk)]
        kb = [kk[SUBLANES * i:SUBLANES * (i + 1)] for i in range(nblk)]
        b8 = [scan8(logf[SUBLANES * i:SUBLANES * (i + 1)]) for i in range(nblk)]
        t8 = [jnp.broadcast_to(x[SUBLANES - 1:SUBLANES], x.shape) for x in b8]
        b16 = [b8[i] + t8[i - 1] if i % 2 else b8[i] for i in range(nblk)]
        t16 = [t8[i - i % 2] + t8[i - i % 2 + 1] for i in range(nblk)]
        b32 = [b16[i] + t16[i - 2] if (i // 2) % 2 else b16[i] for i in range(nblk)]
        t32 = [t16[i - i % 4] + t16[i - i % 4 + 2] for i in range(nblk)]
        b64 = [b32[i] + t32[0] if i >= 4 else b32[i] for i in range(nblk)]
        t64 = t32[0] + t32[4]

        zero = jnp.zeros_like(qb[0])

        def q_side(bl, span):
            return cat([qb[i] * jnp.exp(bl[i]) if (i * SUBLANES // span) % 2 else zero for i in range(nblk)])

        def k_side(bl, tl, span):
            return cat([zero if (i * SUBLANES // span) % 2 else kb[i] * jnp.exp(tl[i] - bl[i]) for i in range(nblk)])

        q64 = cat([qb[i] * jnp.exp(b64[i]) for i in range(nblk)]).astype(BF16)
        k64 = cat([kb[i] * jnp.exp(t64 - b64[i]) for i in range(nblk)]).astype(BF16)
        q32, k32 = q_side(b32, 32).astype(BF16), k_side(b32, t32, 32).astype(BF16)
        q16, k16 = q_side(b16, 16).astype(BF16), k_side(b16, t16, 16).astype(BF16)
        q8, k8 = q_side(b8, 8).astype(BF16), k_side(b8, t8, 8).astype(BF16)
        decay = jnp.exp(t64[0:1])

        diag = [[jnp.zeros((SUBLANES, HGRN_CHUNK), F32) for _ in range(nblk)] for _ in range(n_heads)]
        for j in range(SUBLANES):
            for i in range(nblk):
                if j == 0:
                    p = qb[i] * kb[i]
                else:
                    arg = jnp.where(row8 >= j, b8[i] - pltpu.roll(b8[i], j, axis=0), -jnp.inf)
                    p = qb[i] * pltpu.roll(kb[i], j, axis=0) * jnp.exp(arg)
                hit = colc == rowc + (SUBLANES * i - j)
                for h in range(n_heads):
                    w = jnp.sum(p[:, h * hd:(h + 1) * hd], axis=-1, keepdims=True)
                    diag[h][i] = diag[h][i] + jnp.where(hit, w, 0.0)

        outs = []
        for h in range(n_heads):
            sl = slice(h * hd, (h + 1) * hd)
            a = cat(diag[h])
            a = a + lax.dot_general(q32[:, sl], k32[:, sl], _NT, preferred_element_type=F32)
            a = a + jnp.where(same32, lax.dot_general(q16[:, sl], k16[:, sl], _NT, preferred_element_type=F32), 0.0)
            a = a + jnp.where(same16, lax.dot_general(q8[:, sl], k8[:, sl], _NT, preferred_element_type=F32), 0.0)
            st = state_ref[h]
            o = lax.dot_general(q64[:, sl], st.astype(BF16), _NT, preferred_element_type=F32)
            o = o + jnp.dot(a.astype(BF16), v[:, sl], preferred_element_type=F32)
            state_ref[h] = st * decay[:, sl] + lax.dot_general(v[:, sl], k64[:, sl], _TN,
                                                              preferred_element_type=F32)
            o = o * lax.rsqrt(jnp.mean(o * o, axis=-1, keepdims=True) + NORM_EPS)
            outs.append(o)
        o = jnp.concatenate(outs, axis=-1) * gain
        o_ref[0, rows, :] = (o * jax.nn.silu(g_ref[0, rows, :].astype(F32))).astype(o_ref.dtype)
        return carry

    lax.fori_loop(0, q_ref.shape[1] // HGRN_CHUNK, chunk, 0)


def _hgrn2(hq, hf, hi, hg, lb, gain, tile_s):
    b, s, width = hq.shape
    n_heads = width // HGRN_HEAD_DIM
    blk = pl.BlockSpec((1, tile_s, width), lambda bi, si: (bi, si, 0))
    vec = pl.BlockSpec((1, width), lambda bi, si: (0, 0))
    return pl.pallas_call(
        functools.partial(_hgrn_kernel, n_heads=n_heads),
        grid=(b, s // tile_s),
        in_specs=[blk, blk, blk, blk, vec, vec],
        out_specs=blk,
        out_shape=jax.ShapeDtypeStruct((b, s, width), BF16),
        scratch_shapes=[pltpu.VMEM((n_heads, HGRN_HEAD_DIM, HGRN_HEAD_DIM), F32)],
        compiler_params=_cparams(("arbitrary", "arbitrary")),
        name="hgrn2",
    )(hq, hf, hi, hg, lb, gain)


def _attn_kernel(qc_ref, kc_ref, kp_ref, vc_ref, vp_ref, o_ref, lse_ref, *, group, scale):
    blk = ATTN_BLOCK
    n_pairs = qc_ref.shape[-1] // LANES
    lane = lax.broadcasted_iota(jnp.int32, (1, LANES), 1)
    low = lane < ATTN_HEAD_DIM
    qi = lax.broadcasted_iota(jnp.int32, (blk, 2 * blk), 0)
    ki = lax.broadcasted_iota(jnp.int32, (blk, 2 * blk), 1)
    band = ((ki >= blk) & (ki - blk <= qi)) | ((ki < blk) & (ki >= qi))
    first_key = jnp.where(pl.program_id(2) == 0, blk, 0)
    band_first = band & (ki >= first_key)

    for g in range(group):
        rows = slice(g * blk, (g + 1) * blk)
        prev = slice((g - 1) * blk, g * blk)
        mask = band_first if g == 0 else band
        for pr in range(n_pairs):
            sl = slice(pr * LANES, (pr + 1) * LANES)
            q = qc_ref[0, rows, sl]
            if g == 0:
                k2 = jnp.concatenate([kp_ref[0, :, sl], kc_ref[0, rows, sl]], axis=0)
                v2 = jnp.concatenate([vp_ref[0, :, sl], vc_ref[0, rows, sl]], axis=0)
            else:
                k2 = jnp.concatenate([kc_ref[0, prev, sl], kc_ref[0, rows, sl]], axis=0)
                v2 = jnp.concatenate([vc_ref[0, prev, sl], vc_ref[0, rows, sl]], axis=0)
            acc = None
            lse = None
            for half in (low, jnp.logical_not(low)):
                qh = jnp.where(half, q, jnp.zeros_like(q))
                vh = jnp.where(half, v2, jnp.zeros_like(v2))
                s = lax.dot_general(qh, k2, _NT, preferred_element_type=F32) * scale
                s = jnp.where(mask, s, -jnp.inf)
                m = jnp.max(s, axis=-1, keepdims=True)
                p = jnp.exp(s - m)
                den = jnp.sum(p, axis=-1, keepdims=True)
                oh = jnp.dot(p.astype(BF16), vh, preferred_element_type=F32) / den
                lh = m + jnp.log(den)
                acc = oh if acc is None else acc + oh
                lse = jnp.broadcast_to(lh, (blk, LANES)) if lse is None else jnp.where(low, lse, lh)
            o_ref[0, rows, sl] = acc.astype(o_ref.dtype)
            lse_ref[0, rows, sl] = lse


def _dilated_attention(aq, ak, av, dil, group):
    b, s, width = aq.shape
    m = s // dil
    nb = m // ATTN_BLOCK
    group = min(group, nb)
    tq = group * ATTN_BLOCK
    view = lambda t: t.reshape(b, m, dil * width)
    cur = pl.BlockSpec((1, tq, width), lambda bi, r, n: (bi, n, r))
    prv = pl.BlockSpec((1, ATTN_BLOCK, width), lambda bi, r, n: (bi, jnp.maximum(n * group - 1, 0), r))
    o, lse = pl.pallas_call(
        functools.partial(_attn_kernel, group=group, scale=ATTN_HEAD_DIM ** -0.5),
        grid=(b, dil, nb // group),
        in_specs=[cur, cur, prv, cur, prv],
        out_specs=[cur, cur],
        out_shape=[jax.ShapeDtypeStruct((b, m, dil * width), BF16),
                   jax.ShapeDtypeStruct((b, m, dil * width), F32)],
        compiler_params=_cparams(("arbitrary", "arbitrary", "arbitrary")),
        name=f"dilated_attn_d{dil}",
    )(view(aq), view(ak), view(ak), view(av), view(av))
    return o.reshape(b, s, width), lse.reshape(b, s, width)


def _out_proj_kernel(oa_ref, o1_ref, o2_ref, o3_ref, l1_ref, l2_ref, l3_ref, x_ref, gb_ref, hm_ref,
                     wa_ref, wb_ref, g2_ref, rw_ref, rb_ref, x1_ref, h2_ref, lg_ref):
    l1, l2, l3 = l1_ref[...], l2_ref[...], l3_ref[...]
    mx = jnp.maximum(jnp.maximum(l1, l2), l3)
    e1, e2, e3 = jnp.exp(l1 - mx), jnp.exp(l2 - mx), jnp.exp(l3 - mx)
    ob = (e1 * o1_ref[...].astype(F32) + e2 * o2_ref[...].astype(F32) + e3 * o3_ref[...].astype(F32)) / (e1 + e2 + e3)
    ms = jnp.dot(ob * ob, hm_ref[...], preferred_element_type=F32, precision=lax.Precision.HIGHEST)
    obn = (ob * lax.rsqrt(ms + NORM_EPS) * gb_ref[...]).astype(BF16)
    y = jnp.dot(oa_ref[...], wa_ref[...], preferred_element_type=F32)
    y = y + jnp.dot(obn, wb_ref[...], preferred_element_type=F32)
    x1 = x_ref[...] + y
    x1_ref[...] = x1
    h2 = (x1 * lax.rsqrt(jnp.mean(x1 * x1, axis=-1, keepdims=True) + NORM_EPS)) * g2_ref[...]
    h2_ref[...] = h2
    lg_ref[...] = jnp.dot(h2, rw_ref[...], preferred_element_type=F32,
                          precision=lax.Precision.HIGHEST) + rb_ref[...]


def _out_proj(oa, obs, lses, x2, gain_b, head_mean, w_a, w_b, gain2, rw, rb, tile_m):
    n, d = x2.shape
    wm = oa.shape[1]
    row = lambda w: pl.BlockSpec((tile_m, w), lambda i: (i, 0))
    full = lambda a: pl.BlockSpec(a.shape, lambda i: (0, 0))
    return pl.pallas_call(
        _out_proj_kernel,
        grid=(n // tile_m,),
        in_specs=[row(wm)] * 7 + [row(d), full(gain_b), full(head_mean), full(w_a), full(w_b),
                                   full(gain2), full(rw), full(rb)],
        out_specs=[row(d), row(d), row(LANES)],
        out_shape=[jax.ShapeDtypeStruct((n, d), F32), jax.ShapeDtypeStruct((n, d), F32),
                   jax.ShapeDtypeStruct((n, LANES), F32)],
        compiler_params=_cparams(("arbitrary",)),
        name="out_proj",
    )(oa, *obs, *lses, x2, gain_b, head_mean, w_a, w_b, gain2, rw, rb)


def _route_kernel(lg_ref, idx_ref, gate_ref, rank_ref, cnt_ref, carry_ref):
    tr = lg_ref.shape[0]

    @pl.when(pl.program_id(0) == 0)
    def _():
        carry_ref[...] = jnp.zeros_like(carry_ref)

    logits = lg_ref[...]
    lane = lax.broadcasted_iota(jnp.int32, (tr, LANES), 1).astype(F32)
    vals, idxs = [], []
    for _ in range(TOP_K):
        m = jnp.max(logits, axis=-1, keepdims=True)
        ix = jnp.min(jnp.where(logits == m, lane, float(LANES)), axis=-1, keepdims=True)
        vals.append(m)
        idxs.append(ix)
        logits = jnp.where(lane == ix, -jnp.inf, logits)
    exps = [jnp.exp(v - vals[0]) for v in vals]
    den = exps[0] + exps[1] + exps[2] + exps[3]
    chosen = jnp.zeros((tr, LANES), F32)
    for ix in idxs:
        chosen = chosen + jnp.where(lane == ix, 1.0, 0.0)
    r = lax.broadcasted_iota(jnp.int32, (tr, tr), 0)
    c = lax.broadcasted_iota(jnp.int32, (tr, tr), 1)
    before = jnp.dot(jnp.where(r > c, 1.0, 0.0).astype(BF16), chosen.astype(BF16), preferred_element_type=F32)
    base = before + carry_ref[...]
    idx_o = jnp.zeros((tr, LANES), F32)
    gate_o = jnp.zeros((tr, LANES), F32)
    rank_o = jnp.zeros((tr, LANES), F32)
    for k in range(TOP_K):
        rk = jnp.sum(jnp.where(lane == idxs[k], base, 0.0), axis=-1, keepdims=True)
        idx_o = jnp.where(lane == float(k), idxs[k], idx_o)
        gate_o = jnp.where(lane == float(k), exps[k] / den, gate_o)
        rank_o = jnp.where(lane == float(k), rk, rank_o)
    idx_ref[...] = idx_o.astype(jnp.int32)
    gate_ref[...] = gate_o
    rank_ref[...] = rank_o.astype(jnp.int32)
    carry_ref[...] = carry_ref[...] + jnp.sum(chosen, axis=0, keepdims=True)
    cnt_ref[...] = carry_ref[...].astype(jnp.int32)


def _route(logits, tile_r):
    n = logits.shape[0]
    row = pl.BlockSpec((tile_r, LANES), lambda i: (i, 0))
    return pl.pallas_call(
        _route_kernel,
        grid=(n // tile_r,),
        in_specs=[row],
        out_specs=[row, row, row, pl.BlockSpec((1, LANES), lambda i: (0, 0))],
        out_shape=[jax.ShapeDtypeStruct((n, LANES), jnp.int32), jax.ShapeDtypeStruct((n, LANES), F32),
                   jax.ShapeDtypeStruct((n, LANES), jnp.int32), jax.ShapeDtypeStruct((1, LANES), jnp.int32)],
        scratch_shapes=[pltpu.VMEM((1, LANES), F32)],
        compiler_params=_cparams(("arbitrary",)),
        name="route",
    )(logits)


def _scatter_kernel(dest_ref, h_ref, xin_ref, xout_ref, sem):
    del xin_ref
    tt = h_ref.shape[0]

    def row_copy(t, k):
        d = dest_ref[0, 0, t * TOP_K + k]
        return pltpu.make_async_copy(h_ref.at[pl.ds(t, 1)], xout_ref.at[pl.ds(d, 1)], sem)

    def start(t, c):
        for k in range(TOP_K):
            row_copy(t, k).start()
        return c

    def wait(t, c):
        for k in range(TOP_K):
            row_copy(t, k).wait()
        return c

    lax.fori_loop(0, tt, start, 0)
    lax.fori_loop(0, tt, wait, 0)


def _moe_scatter(dest3, h2, n_rows, tile_t):
    n, d = h2.shape
    zeros = jnp.zeros((n_rows, d), h2.dtype)
    return pl.pallas_call(
        _scatter_kernel,
        grid=(n // tile_t,),
        in_specs=[pl.BlockSpec((1, 1, tile_t * TOP_K), lambda i: (i, 0, 0), memory_space=pltpu.SMEM),
                  pl.BlockSpec((tile_t, d), lambda i: (i, 0)),
                  pl.BlockSpec(memory_space=pl.ANY)],
        out_specs=pl.BlockSpec(memory_space=pl.ANY),
        out_shape=jax.ShapeDtypeStruct((n_rows, d), h2.dtype),
        scratch_shapes=[pltpu.SemaphoreType.DMA],
        input_output_aliases={2: 0},
        compiler_params=_cparams(("arbitrary",)),
        name="moe_scatter",
    )(dest3, h2, zeros)


def _expert_kernel(be_ref, nu_ref, x_ref, wg_ref, wl_ref, bg_ref, bl_ref, wd_ref, bd_ref, y_ref):
    del be_ref

    @pl.when(pl.program_id(0) < nu_ref[0])
    def _():
        x = x_ref[...].astype(BF16)
        glu = jnp.dot(x, wg_ref[0], preferred_element_type=F32) + bg_ref[0]
        lin = jnp.dot(x, wl_ref[0], preferred_element_type=F32) + bl_ref[0]
        glu = jnp.minimum(glu, SWIGLU_LIMIT)
        lin = jnp.clip(lin, -SWIGLU_LIMIT, SWIGLU_LIMIT)
        h = glu * jax.nn.sigmoid(SWIGLU_ALPHA * glu) * (lin + 1.0)
        y_ref[...] = jnp.dot(h.astype(BF16), wd_ref[0], preferred_element_type=F32) + bd_ref[0]

    @pl.when(pl.program_id(0) >= nu_ref[0])
    def _():
        y_ref[...] = jnp.zeros_like(y_ref)


def _moe_experts(block_e, n_used, xbuf, wg, wl, bg, bl, wd, bd):
    n_rows, d = xbuf.shape
    de = wg.shape[-1]
    nblk = n_rows // MOE_ROWS
    rows = pl.BlockSpec((MOE_ROWS, d), lambda i, be, nu: (i, 0))
    wspec = lambda a: pl.BlockSpec((1,) + a.shape[1:], lambda i, be, nu: (be[i], 0, 0))
    return pl.pallas_call(
        _expert_kernel,
        grid_spec=pltpu.PrefetchScalarGridSpec(
            num_scalar_prefetch=2,
            grid=(nblk,),
            in_specs=[rows, wspec(wg), wspec(wl), wspec(bg), wspec(bl), wspec(wd), wspec(bd)],
            out_specs=rows),
        out_shape=jax.ShapeDtypeStruct((n_rows, d), F32),
        compiler_params=_cparams(("arbitrary",)),
        name="moe_experts",
    )(block_e, n_used, xbuf, wg, wl, bg, bl, wd, bd)


def _combine_kernel(dest_ref, gate_ref, x1_ref, g_ref, y_ref, o_ref, buf_ref, sem):
    tt = x1_ref.shape[0]

    def row_copy(t, k):
        d = dest_ref[0, 0, t * TOP_K + k]
        return pltpu.make_async_copy(y_ref.at[pl.ds(d, 1)], buf_ref.at[k, pl.ds(t, 1)], sem)

    def start(t, c):
        for k in range(TOP_K):
            row_copy(t, k).start()
        return c

    def wait(t, c):
        for k in range(TOP_K):
            row_copy(t, k).wait()
        return c

    lax.fori_loop(0, tt, start, 0)
    lax.fori_loop(0, tt, wait, 0)
    gates = gate_ref[...]
    acc = x1_ref[...]
    for k in range(TOP_K):
        acc = acc + gates[:, k:k + 1] * buf_ref[k]
    o_ref[...] = (acc * lax.rsqrt(jnp.mean(acc * acc, axis=-1, keepdims=True) + NORM_EPS)) * g_ref[...]


def _moe_combine(dest3, gates, x1, gain, ybuf, tile_t):
    n, d = x1.shape
    return pl.pallas_call(
        _combine_kernel,
        grid=(n // tile_t,),
        in_specs=[pl.BlockSpec((1, 1, tile_t * TOP_K), lambda i: (i, 0, 0), memory_space=pltpu.SMEM),
                  pl.BlockSpec((tile_t, LANES), lambda i: (i, 0)),
                  pl.BlockSpec((tile_t, d), lambda i: (i, 0)),
                  pl.BlockSpec((1, d), lambda i: (0, 0)),
                  pl.BlockSpec(memory_space=pl.ANY)],
        out_specs=pl.BlockSpec((tile_t, d), lambda i: (i, 0)),
        out_shape=jax.ShapeDtypeStruct((n, d), F32),
        scratch_shapes=[pltpu.VMEM((TOP_K, tile_t, d), F32), pltpu.SemaphoreType.DMA],
        compiler_params=_cparams(("arbitrary",)),
        name="moe_combine",
    )(dest3, gates, x1, gain, ybuf)


def kernel(x, norm1_g, w_in, hgrn_lb_logits, hgrn_norm_g, attn_norm_g, w_out, norm2_g, router_w, router_b,
           w_up, b_up, w_down, b_down, final_norm_g):
    b, s, d = x.shape
    n = b * s
    depth = w_in.shape[0]
    n_experts = router_w.shape[-1]
    d_mix = w_out.shape[1]
    d_hgrn = d_mix // 2
    d_attn = d_mix - d_hgrn
    assert w_in.shape[-1] == 4 * d_hgrn + 3 * d_attn and d_hgrn == d_attn
    assert all(s % win == 0 and win // dil == ATTN_BLOCK for win, dil in DILATED_PATTERNS)
    assert s % 512 == 0 and n % 512 == 0 and n_experts <= LANES
    assert depth == 1, "the final rmsnorm is fused into the single layer's MoE combine"

    lb_all = jnp.cumsum(jax.nn.softmax(hgrn_lb_logits.astype(F32), axis=0), axis=0)
    lane = jnp.arange(d_attn)
    head_mean = jnp.where((lane[:, None] // ATTN_HEAD_DIM) == (lane[None, :] // ATTN_HEAD_DIM),
                          1.0 / ATTN_HEAD_DIM, 0.0).astype(F32)
    x2 = x.reshape(n, d)
    l = 0
    hq, hf, hi, hg, aq, ak, av = _in_proj(
        x2, norm1_g[l].reshape(1, d), w_in[l].astype(BF16),
        (BF16, F32, BF16, BF16, BF16, BF16, BF16), tile_m=512)
    to3 = lambda t: t.reshape(b, s, -1)
    o_a = _hgrn2(to3(hq), to3(hf), to3(hi), to3(hg), lb_all[l].reshape(1, d_hgrn),
                 hgrn_norm_g[l].reshape(1, d_hgrn), tile_s=512)
    obs, lses = [], []
    for _, dil in DILATED_PATTERNS:
        o_p, lse_p = _dilated_attention(to3(aq), to3(ak), to3(av), dil, group=2)
        obs.append(o_p.reshape(n, d_attn))
        lses.append(lse_p.reshape(n, d_attn))
    rw = jnp.zeros((d, LANES), F32).at[:, :n_experts].set(router_w[l])
    rb = jnp.full((1, LANES), -jnp.inf, F32).at[0, :n_experts].set(router_b[l])
    w_o = w_out[l].astype(BF16)
    x1, h2, logits = _out_proj(o_a.reshape(n, d_hgrn), obs, lses, x2, attn_norm_g[l].reshape(1, d_attn),
                               head_mean, w_o[:d_hgrn], w_o[d_hgrn:], norm2_g[l].reshape(1, d), rw, rb,
                               tile_m=256)
    idx, gates, rank, counts = _route(logits, tile_r=512)

    counts = counts[0, :n_experts]
    padded = (counts + MOE_ROWS - 1) // MOE_ROWS * MOE_ROWS
    pad_end = jnp.cumsum(padded)
    pad_start = pad_end - padded
    n_rows = n * TOP_K + n_experts * MOE_ROWS
    nblk = n_rows // MOE_ROWS
    dest = pad_start[idx[:, :TOP_K]] + rank[:, :TOP_K]
    block_row = jnp.arange(nblk, dtype=jnp.int32) * MOE_ROWS
    block_e = jnp.minimum(jnp.sum(pad_end[None, :] <= block_row[:, None], axis=1), n_experts - 1).astype(jnp.int32)
    n_used = (pad_end[-1:] // MOE_ROWS).astype(jnp.int32)
    tile_t = 256
    dest3 = dest.astype(jnp.int32).reshape(n // tile_t, 1, tile_t * TOP_K)

    xbuf = _moe_scatter(dest3, h2, n_rows, tile_t)
    wu = w_up[l]
    ybuf = _moe_experts(block_e, n_used, xbuf,
                        wu[:, :, 0::2].astype(BF16), wu[:, :, 1::2].astype(BF16),
                        b_up[l][:, None, 0::2], b_up[l][:, None, 1::2],
                        w_down[l].astype(BF16), b_down[l][:, None, :])
    out = _moe_combine(dest3, gates, x1, final_norm_g.reshape(1, d), ybuf, tile_t)
    return out.reshape(b, s, d)
```

```python
import functools

import jax
import jax.numpy as jnp
from jax import lax
from jax.experimental import pallas as pl
from jax.experimental.pallas import tpu as pltpu

F32 = jnp.float32
BF16 = jnp.bfloat16

NORM_EPS = 1e-6
HGRN_HEAD_DIM = 128
ATTN_HEAD_DIM = 64
DILATED_PATTERNS = ((128, 1), (512, 4), (2048, 16))
TOP_K = 4
SWIGLU_ALPHA = 1.702
SWIGLU_LIMIT = 7.0

LANES = 128
SUBLANES = 8
HGRN_CHUNK = 64
ATTN_BLOCK = 128
MOE_ROWS = 512
MOE_CHUNK = SUBLANES
VMEM_LIMIT = 56 * 1024 * 1024

_NT = (((1,), (1,)), ((), ()))
_TN = (((0,), (0,)), ((), ()))


def _cparams(sem):
    return pltpu.CompilerParams(dimension_semantics=sem, vmem_limit_bytes=VMEM_LIMIT)


def _to_classes(res, o_ref, slab_ref, dil):
    tm, width = res.shape
    for sb in range(width // LANES):
        slab_ref[sb] = res[:, sb * LANES:(sb + 1) * LANES]
    for r in range(dil):
        for sb in range(width // LANES):
            col = r * width + sb * LANES
            o_ref[:, col:col + LANES] = slab_ref[sb, pl.ds(r, tm // dil, stride=dil), :].astype(o_ref.dtype)


def _from_classes(ref, slab_ref, dil):
    if dil == 1:
        return ref[...].astype(F32)
    tm_d = ref.shape[0]
    width = ref.shape[1] // dil
    for r in range(dil):
        for sb in range(width // LANES):
            col = r * width + sb * LANES
            slab_ref[sb, pl.ds(r, tm_d, stride=dil), :] = ref[:, col:col + LANES].astype(F32)
    return jnp.concatenate([slab_ref[sb] for sb in range(width // LANES)], axis=-1)


def _in_proj_kernel(x_ref, g_ref, w_ref, *refs, plain_dtypes, n_attn, dils):
    out_refs, slab_ref = refs[:-1], refs[-1]
    x = x_ref[...]
    r = lax.rsqrt(jnp.mean(x * x, axis=-1, keepdims=True) + NORM_EPS)
    h = ((x * r) * g_ref[...]).astype(BF16)
    width = out_refs[0].shape[-1]
    n_plain = len(plain_dtypes)
    for j in range(n_plain + n_attn):
        res = jnp.dot(h, w_ref[:, j * width:(j + 1) * width], preferred_element_type=F32)
        if j < n_plain:
            out_refs[j][...] = res.astype(out_refs[j].dtype)
            continue
        for p, dil in enumerate(dils):
            o_ref = out_refs[n_plain + (j - n_plain) * len(dils) + p]
            if dil == 1:
                o_ref[...] = res.astype(o_ref.dtype)
            else:
                _to_classes(res, o_ref, slab_ref, dil)


def _in_proj(x2, gain, w_bf16, plain_dtypes, n_attn, dils, tile_m):
    n, d = x2.shape
    width = w_bf16.shape[1] // (len(plain_dtypes) + n_attn)
    out_specs = [pl.BlockSpec((tile_m, width), lambda i: (i, 0)) for _ in plain_dtypes]
    out_shape = [jax.ShapeDtypeStruct((n, width), dt) for dt in plain_dtypes]
    for _ in range(n_attn):
        for dil in dils:
            out_specs.append(pl.BlockSpec((tile_m // dil, dil * width), lambda i: (i, 0)))
            out_shape.append(jax.ShapeDtypeStruct((n // dil, dil * width), BF16))
    return pl.pallas_call(
        functools.partial(_in_proj_kernel, plain_dtypes=plain_dtypes, n_attn=n_attn, dils=dils),
        grid=(n // tile_m,),
        in_specs=[pl.BlockSpec((tile_m, d), lambda i: (i, 0)),
                  pl.BlockSpec((1, d), lambda i: (0, 0)),
                  pl.BlockSpec(w_bf16.shape, lambda i: (0, 0))],
        out_specs=out_specs,
        out_shape=out_shape,
        scratch_shapes=[pltpu.VMEM((width // LANES, tile_m, LANES), F32)],
        compiler_params=_cparams(("arbitrary",)),
        name="in_proj",
    )(x2, gain, w_bf16)


def _hgrn_kernel(q_ref, f_ref, i_ref, g_ref, lb_ref, gain_ref, o_ref, state_ref, *, n_heads):
    hd = HGRN_HEAD_DIM
    width = n_heads * hd
    nblk = HGRN_CHUNK // SUBLANES

    @pl.when(pl.program_id(1) == 0)
    def _():
        state_ref[...] = jnp.zeros_like(state_ref)

    lb = lb_ref[...]
    gain = gain_ref[...]
    row8 = lax.broadcasted_iota(jnp.int32, (SUBLANES, width), 0)
    rowc = lax.broadcasted_iota(jnp.int32, (SUBLANES, HGRN_CHUNK), 0)
    colc = lax.broadcasted_iota(jnp.int32, (SUBLANES, HGRN_CHUNK), 1)
    r64 = lax.broadcasted_iota(jnp.int32, (HGRN_CHUNK, HGRN_CHUNK), 0)
    c64 = lax.broadcasted_iota(jnp.int32, (HGRN_CHUNK, HGRN_CHUNK), 1)
    same32 = (r64 // 32) == (c64 // 32)
    same16 = (r64 // 16) == (c64 // 16)

    def scan8(x):
        for s in (1, 2, 4):
            x = x + jnp.where(row8 >= s, pltpu.roll(x, s, axis=0), 0.0)
        return x

    def cat(blocks):
        return jnp.concatenate(blocks, axis=0)

    def chunk(ci, carry):
        r0 = pl.multiple_of(ci * HGRN_CHUNK, HGRN_CHUNK)
        rows = pl.ds(r0, HGRN_CHUNK)
        q = q_ref[0, rows, :].astype(F32)
        v = i_ref[0, rows, :]
        f = lb + (1.0 - lb) * jax.nn.sigmoid(f_ref[0, rows, :])
        logf = jnp.log(f)
        kk = 1.0 - f

        qb = [q[SUBLANES * i:SUBLANES * (i + 1)] for i in range(nblk)]
        kb = [kk[SUBLANES * i:SUBLANES * (i + 1)] for i in range(nblk)]
        b8 = [scan8(logf[SUBLANES * i:SUBLANES * (i + 1)]) for i in range(nblk)]
        t8 = [jnp.broadcast_to(x[SUBLANES - 1:SUBLANES], x.shape) for x in b8]
        b16 = [b8[i] + t8[i - 1] if i % 2 else b8[i] for i in range(nblk)]
        t16 = [t8[i - i % 2] + t8[i - i % 2 + 1] for i in range(nblk)]
        b32 = [b16[i] + t16[i - 2] if (i // 2) % 2 else b16[i] for i in range(nblk)]
        t32 = [t16[i - i % 4] + t16[i - i % 4 + 2] for i in range(nblk)]
        b64 = [b32[i] + t32[0] if i >= 4 else b32[i] for i in range(nblk)]
        t64 = t32[0] + t32[4]

        zero = jnp.zeros_like(qb[0])

        def q_side(bl, span):
            return cat([qb[i] * jnp.exp(bl[i]) if (i * SUBLANES // span) % 2 else zero for i in range(nblk)])

        def k_side(bl, tl, span):
            return cat([zero if (i * SUBLANES // span) % 2 else kb[i] * jnp.exp(tl[i] - bl[i]) for i in range(nblk)])

        q64 = cat([qb[i] * jnp.exp(b64[i]) for i in range(nblk)]).astype(BF16)
        k64 = cat([kb[i] * jnp.exp(t64 - b64[i]) for i in range(nblk)]).astype(BF16)
        q32, k32 = q_side(b32, 32).astype(BF16), k_side(b32, t32, 32).astype(BF16)
        q16, k16 = q_side(b16, 16).astype(BF16), k_side(b16, t16, 16).astype(BF16)
        q8, k8 = q_side(b8, 8).astype(BF16), k_side(b8, t8, 8).astype(BF16)
        decay = jnp.exp(t64[0:1])

        diag = [[jnp.zeros((SUBLANES, HGRN_CHUNK), F32) for _ in range(nblk)] for _ in range(n_heads)]
        for j in range(SUBLANES):
            for i in range(nblk):
                if j == 0:
                    p = qb[i] * kb[i]
                else:
                    arg = jnp.where(row8 >= j, b8[i] - pltpu.roll(b8[i], j, axis=0), -jnp.inf)
                    p = qb[i] * pltpu.roll(kb[i], j, axis=0) * jnp.exp(arg)
                hit = colc == rowc + (SUBLANES * i - j)
                for h in range(n_heads):
                    w = jnp.sum(p[:, h * hd:(h + 1) * hd], axis=-1, keepdims=True)
                    diag[h][i] = diag[h][i] + jnp.where(hit, w, 0.0)

        outs = []
        for h in range(n_heads):
            sl = slice(h * hd, (h + 1) * hd)
            a = cat(diag[h])
            a = a + lax.dot_general(q32[:, sl], k32[:, sl], _NT, preferred_element_type=F32)
            a = a + jnp.where(same32, lax.dot_general(q16[:, sl], k16[:, sl], _NT, preferred_element_type=F32), 0.0)
            a = a + jnp.where(same16, lax.dot_general(q8[:, sl], k8[:, sl], _NT, preferred_element_type=F32), 0.0)
            st = state_ref[h]
            o = lax.dot_general(q64[:, sl], st.astype(BF16), _NT, preferred_element_type=F32)
            o = o + jnp.dot(a.astype(BF16), v[:, sl], preferred_element_type=F32)
            state_ref[h] = st * decay[:, sl] + lax.dot_general(v[:, sl], k64[:, sl], _TN,
                                                              preferred_element_type=F32)
            o = o * lax.rsqrt(jnp.mean(o * o, axis=-1, keepdims=True) + NORM_EPS)
            outs.append(o)
        o = jnp.concatenate(outs, axis=-1) * gain
        o_ref[0, rows, :] = (o * jax.nn.silu(g_ref[0, rows, :].astype(F32))).astype(o_ref.dtype)
        return carry

    lax.fori_loop(0, q_ref.shape[1] // HGRN_CHUNK, chunk, 0)


def _hgrn2(hq, hf, hi, hg, lb, gain, tile_s):
    b, s, width = hq.shape
    n_heads = width // HGRN_HEAD_DIM
    blk = pl.BlockSpec((1, tile_s, width), lambda bi, si: (bi, si, 0))
    vec = pl.BlockSpec((1, width), lambda bi, si: (0, 0))
    return pl.pallas_call(
        functools.partial(_hgrn_kernel, n_heads=n_heads),
        grid=(b, s // tile_s),
        in_specs=[blk, blk, blk, blk, vec, vec],
        out_specs=blk,
        out_shape=jax.ShapeDtypeStruct((b, s, width), BF16),
        scratch_shapes=[pltpu.VMEM((n_heads, HGRN_HEAD_DIM, HGRN_HEAD_DIM), F32)],
        compiler_params=_cparams(("arbitrary", "arbitrary")),
        name="hgrn2",
    )(hq, hf, hi, hg, lb, gain)


def _attn_kernel(qc_ref, kc_ref, kp_ref, vc_ref, vp_ref, o_ref, lse_ref, *, group, scale):
    blk = ATTN_BLOCK
    n_pairs = qc_ref.shape[-1] // LANES
    lane = lax.broadcasted_iota(jnp.int32, (1, LANES), 1)
    low = lane < ATTN_HEAD_DIM
    qi = lax.broadcasted_iota(jnp.int32, (blk, 2 * blk), 0)
    ki = lax.broadcasted_iota(jnp.int32, (blk, 2 * blk), 1)
    band = ((ki >= blk) & (ki - blk <= qi)) | ((ki < blk) & (ki >= qi))
    first_key = jnp.where(pl.program_id(2) == 0, blk, 0)
    band_first = band & (ki >= first_key)

    for g in range(group):
        rows = slice(g * blk, (g + 1) * blk)
        prev = slice((g - 1) * blk, g * blk)
        mask = band_first if g == 0 else band
        for pr in range(n_pairs):
            sl = slice(pr * LANES, (pr + 1) * LANES)
            q = qc_ref[0, rows, sl]
            if g == 0:
                k2 = jnp.concatenate([kp_ref[0, :, sl], kc_ref[0, rows, sl]], axis=0)
                v2 = jnp.concatenate([vp_ref[0, :, sl], vc_ref[0, rows, sl]], axis=0)
            else:
                k2 = jnp.concatenate([kc_ref[0, prev, sl], kc_ref[0, rows, sl]], axis=0)
                v2 = jnp.concatenate([vc_ref[0, prev, sl], vc_ref[0, rows, sl]], axis=0)
            acc = None
            lse = None
            for half in (low, jnp.logical_not(low)):
                qh = jnp.where(half, q, jnp.zeros_like(q))
                vh = jnp.where(half, v2, jnp.zeros_like(v2))
                s = lax.dot_general(qh, k2, _NT, preferred_element_type=F32) * scale
                s = jnp.where(mask, s, -jnp.inf)
                m = jnp.max(s, axis=-1, keepdims=True)
                p = jnp.exp(s - m)
                den = jnp.sum(p, axis=-1, keepdims=True)
                oh = jnp.dot(p.astype(BF16), vh, preferred_element_type=F32) / den
                lh = m + jnp.log(den)
                acc = oh if acc is None else acc + oh
                lse = jnp.broadcast_to(lh, (blk, LANES)) if lse is None else jnp.where(low, lse, lh)
            o_ref[0, rows, sl] = acc.astype(o_ref.dtype)
            lse_ref[0, rows, sl] = lse


def _dilated_attention(aq, ak, av, dil, group):
    b, m, width = aq.shape
    width //= dil
    nb = m // ATTN_BLOCK
    group = min(group, nb)
    tq = group * ATTN_BLOCK
    cur = pl.BlockSpec((1, tq, width), lambda bi, r, n: (bi, n, r))
    prv = pl.BlockSpec((1, ATTN_BLOCK, width), lambda bi, r, n: (bi, jnp.maximum(n * group - 1, 0), r))
    o, lse = pl.pallas_call(
        functools.partial(_attn_kernel, group=group, scale=ATTN_HEAD_DIM ** -0.5),
        grid=(b, dil, nb // group),
        in_specs=[cur, cur, prv, cur, prv],
        out_specs=[cur, cur],
        out_shape=[jax.ShapeDtypeStruct((b, m, dil * width), BF16),
                   jax.ShapeDtypeStruct((b, m, dil * width), F32)],
        compiler_params=_cparams(("arbitrary", "arbitrary", "arbitrary")),
        name=f"dilated_attn_d{dil}",
    )(aq, ak, ak, av, av)
    return o, lse


def _split_bf16(v):
    hi = v.astype(BF16)
    return hi, (v - hi.astype(F32)).astype(BF16)


def _out_proj_kernel(oa_ref, *refs, dils):
    n_pat = len(dils)
    o_refs, l_refs = refs[:n_pat], refs[n_pat:2 * n_pat]
    (x_ref, gb_ref, hm_ref, wa_ref, wb_ref, g2_ref, rwh_ref, rwl_ref, rb_ref,
     x1_ref, h2_ref, lg_ref) = refs[2 * n_pat:2 * n_pat + 12]
    slabs = refs[2 * n_pat + 12:]
    outs = [_from_classes(o_refs[p], slabs[2 * p], dils[p]) for p in range(n_pat)]
    lses = [_from_classes(l_refs[p], slabs[2 * p + 1], dils[p]) for p in range(n_pat)]
    mx = functools.reduce(jnp.maximum, lses)
    es = [jnp.exp(l - mx) for l in lses]
    ob = sum(e * o for e, o in zip(es, outs)) / sum(es)
    sq_hi, sq_lo = _split_bf16(ob * ob)
    ms = (jnp.dot(sq_hi, hm_ref[...], preferred_element_type=F32)
          + jnp.dot(sq_lo, hm_ref[...], preferred_element_type=F32))
    obn = (ob * lax.rsqrt(ms + NORM_EPS) * gb_ref[...]).astype(BF16)
    y = jnp.dot(oa_ref[...], wa_ref[...], preferred_element_type=F32)
    y = y + jnp.dot(obn, wb_ref[...], preferred_element_type=F32)
    x1 = x_ref[...] + y
    x1_ref[...] = x1
    h2 = (x1 * lax.rsqrt(jnp.mean(x1 * x1, axis=-1, keepdims=True) + NORM_EPS)) * g2_ref[...]
    h2_ref[...] = h2
    h_hi, h_lo = _split_bf16(h2)
    lg_ref[...] = (jnp.dot(h_hi, rwh_ref[...], preferred_element_type=F32)
                   + jnp.dot(h_hi, rwl_ref[...], preferred_element_type=F32)
                   + jnp.dot(h_lo, rwh_ref[...], preferred_element_type=F32)) + rb_ref[...]


def _out_proj(oa, obs, lses, dils, x2, gain_b, head_mean, w_a, w_b, gain2, rw_hi, rw_lo, rb, tile_m):
    n, d = x2.shape
    wm = oa.shape[1]
    row = lambda w: pl.BlockSpec((tile_m, w), lambda i: (i, 0))
    cls = [pl.BlockSpec((tile_m // dil, dil * wm), lambda i: (i, 0)) for dil in dils]
    full = lambda a: pl.BlockSpec(a.shape, lambda i: (0, 0))
    consts = (gain_b, head_mean, w_a, w_b, gain2, rw_hi, rw_lo, rb)
    return pl.pallas_call(
        functools.partial(_out_proj_kernel, dils=dils),
        grid=(n // tile_m,),
        in_specs=[row(wm)] + cls + cls + [row(d)] + [full(a) for a in consts],
        out_specs=[row(d), row(d), row(LANES)],
        out_shape=[jax.ShapeDtypeStruct((n, d), F32), jax.ShapeDtypeStruct((n, d), F32),
                   jax.ShapeDtypeStruct((n, LANES), F32)],
        scratch_shapes=[pltpu.VMEM((wm // LANES, tile_m, LANES), F32) for _ in range(2 * len(dils))],
        compiler_params=_cparams(("arbitrary",)),
        name="out_proj",
    )(oa, *obs, *lses, x2, *consts)


def _route_kernel(lg_ref, gate_ref, pos_ref, post_ref, start_ref, nch_ref, nfull_ref, off_ref, cnt_ref,
                  carry_ref):
    tr = lg_ref.shape[0]

    @pl.when(pl.program_id(0) == 0)
    def _():
        carry_ref[...] = jnp.zeros_like(carry_ref)

    logits = lg_ref[...]
    lane = lax.broadcasted_iota(jnp.int32, (tr, LANES), 1).astype(F32)
    vals, idxs = [], []
    for _ in range(TOP_K):
        m = jnp.max(logits, axis=-1, keepdims=True)
        ix = jnp.min(jnp.where(logits == m, lane, float(LANES)), axis=-1, keepdims=True)
        vals.append(m)
        idxs.append(ix)
        logits = jnp.where(lane == ix, -jnp.inf, logits)
    exps = [jnp.exp(v - vals[0]) for v in vals]
    den = exps[0] + exps[1] + exps[2] + exps[3]
    chosen = jnp.zeros((tr, LANES), F32)
    for ix in idxs:
        chosen = chosen + jnp.where(lane == ix, 1.0, 0.0)
    r = lax.broadcasted_iota(jnp.int32, (tr, tr), 0)
    c = lax.broadcasted_iota(jnp.int32, (tr, tr), 1)
    before = jnp.dot(jnp.where(r > c, 1.0, 0.0).astype(BF16), chosen.astype(BF16), preferred_element_type=F32)
    count = jnp.sum(chosen, axis=0, keepdims=True)
    carry = carry_ref[...]
    head = carry - MOE_CHUNK * jnp.floor(carry * (1.0 / MOE_CHUNK))
    nfull = jnp.floor((head + count) * (1.0 / MOE_CHUNK))
    nch = jnp.floor((head + count + (MOE_CHUNK - 1)) * (1.0 / MOE_CHUNK))
    el = lax.broadcasted_iota(jnp.int32, (LANES, LANES), 0)
    ec = lax.broadcasted_iota(jnp.int32, (LANES, LANES), 1)
    off = MOE_CHUNK * jnp.dot(jnp.broadcast_to(nch, (SUBLANES, LANES)).astype(BF16),
                              jnp.where(el < ec, 1.0, 0.0).astype(BF16), preferred_element_type=F32)[0:1]
    slot = before + off + head
    gate_o = jnp.zeros((tr, LANES), F32)
    pos_o = jnp.zeros((tr, LANES), F32)
    for k in range(TOP_K):
        pk = jnp.sum(jnp.where(lane == idxs[k], slot, 0.0), axis=-1, keepdims=True)
        gate_o = jnp.where(lane == float(k), exps[k] / den, gate_o)
        pos_o = jnp.where(lane == float(k), pk, pos_o)
    gate_ref[...] = gate_o
    pos_ref[...] = pos_o.astype(jnp.int32)
    post_ref[...] = pos_o.T[0:SUBLANES].astype(jnp.int32)
    start_ref[0] = (carry - head).astype(jnp.int32)
    nch_ref[0] = nch.astype(jnp.int32)
    nfull_ref[0] = nfull.astype(jnp.int32)
    off_ref[0] = off.astype(jnp.int32)
    carry_ref[...] = carry + count
    cnt_ref[...] = (carry + count).astype(jnp.int32)


def _route(logits, tile_t):
    n = logits.shape[0]
    nt = n // tile_t
    row = pl.BlockSpec((tile_t, LANES), lambda i: (i, 0))
    tab = pl.BlockSpec((1, 1, LANES), lambda i: (i, 0, 0))
    tab_shape = jax.ShapeDtypeStruct((nt, 1, LANES), jnp.int32)
    return pl.pallas_call(
        _route_kernel,
        grid=(nt,),
        in_specs=[row],
        out_specs=[row, row, pl.BlockSpec((SUBLANES, tile_t), lambda i: (i, 0)), tab, tab, tab, tab,
                   pl.BlockSpec((1, LANES), lambda i: (0, 0))],
        out_shape=[jax.ShapeDtypeStruct((n, LANES), F32), jax.ShapeDtypeStruct((n, LANES), jnp.int32),
                   jax.ShapeDtypeStruct((nt * SUBLANES, tile_t), jnp.int32), tab_shape, tab_shape, tab_shape,
                   tab_shape, jax.ShapeDtypeStruct((1, LANES), jnp.int32)],
        scratch_shapes=[pltpu.VMEM((1, LANES), F32)],
        compiler_params=_cparams(("arbitrary",)),
        name="route",
    )(logits)


def _for_each_chunk(seg_ref, cnt_ref, off_ref, n_experts, fn):
    base = pl.program_id(0) * n_experts

    def expert(e, carry):
        seg = pl.multiple_of(seg_ref[base + e], MOE_CHUNK)
        off = pl.multiple_of(off_ref[base + e], MOE_CHUNK)

        def chunk(c, carry):
            fn(pl.multiple_of(off + c * MOE_CHUNK, MOE_CHUNK), pl.multiple_of(seg + c * MOE_CHUNK, MOE_CHUNK))
            return carry

        return lax.fori_loop(0, cnt_ref[base + e], chunk, carry)

    lax.fori_loop(0, n_experts, expert, 0)


def _scatter_kernel(seg_ref, nch_ref, nwr_ref, off_ref, post_ref, h_ref, xin_ref, xout_ref, stage_ref, open_ref,
                    sem, *, n_experts):
    del xin_ref
    rs, tt = stage_ref.shape[0], h_ref.shape[0]
    base = pl.program_id(0) * n_experts

    @pl.when(pl.program_id(0) == 0)
    def _():
        open_ref[...] = jnp.zeros_like(open_ref)

    srow = lax.broadcasted_iota(jnp.int32, (rs, tt), 0)
    post = post_ref[...]
    hit = srow == post[0:1]
    for k in range(1, TOP_K):
        hit = hit | (srow == post[k:k + 1])
    sel = jnp.where(hit, 1.0, 0.0).astype(BF16)
    stage_ref[...] = jnp.dot(sel, h_ref[...].astype(BF16), preferred_element_type=F32)

    def add_open(e, carry):
        rows = pl.ds(pl.multiple_of(off_ref[base + e], MOE_CHUNK), MOE_CHUNK)
        stage_ref[rows, :] = stage_ref[rows, :] + open_ref[e]
        return carry

    def save_open(e, carry):
        nwr = nwr_ref[base + e]
        rows = pl.ds(pl.multiple_of(off_ref[base + e] + nwr * MOE_CHUNK, MOE_CHUNK), MOE_CHUNK)
        still_open = nch_ref[base + e] > nwr
        open_ref[e] = jnp.where(still_open, stage_ref[rows, :], 0.0)
        return carry

    def chunk_copy(srow0, xrow0):
        return pltpu.make_async_copy(stage_ref.at[pl.ds(srow0, MOE_CHUNK)], xout_ref.at[pl.ds(xrow0, MOE_CHUNK)], sem)

    lax.fori_loop(0, n_experts, add_open, 0)
    _for_each_chunk(seg_ref, nwr_ref, off_ref, n_experts, lambda s0, x0: chunk_copy(s0, x0).start())
    lax.fori_loop(0, n_experts, save_open, 0)
    _for_each_chunk(seg_ref, nwr_ref, off_ref, n_experts, lambda s0, x0: chunk_copy(s0, x0).wait())


def _stage_rows(tile_t, n_experts):
    return tile_t * TOP_K + (2 * n_experts + 1) * MOE_CHUNK


def _moe_scatter(seg, nch, nwr, off, post, h2, n_rows, tile_t, n_experts):
    n, d = h2.shape
    zeros = jnp.zeros((n_rows, d), F32)
    return pl.pallas_call(
        functools.partial(_scatter_kernel, n_experts=n_experts),
        grid_spec=pltpu.PrefetchScalarGridSpec(
            num_scalar_prefetch=4,
            grid=(n // tile_t,),
            in_specs=[pl.BlockSpec((SUBLANES, tile_t), lambda i, *_: (i, 0)),
                      pl.BlockSpec((tile_t, d), lambda i, *_: (i, 0)),
                      pl.BlockSpec(memory_space=pl.ANY)],
            out_specs=pl.BlockSpec(memory_space=pl.ANY),
            scratch_shapes=[pltpu.VMEM((_stage_rows(tile_t, n_experts), d), F32),
                            pltpu.VMEM((n_experts, MOE_CHUNK, d), F32), pltpu.SemaphoreType.DMA]),
        out_shape=jax.ShapeDtypeStruct((n_rows, d), F32),
        input_output_aliases={6: 0},
        compiler_params=_cparams(("arbitrary",)),
        name="moe_scatter",
    )(seg, nch, nwr, off, post, h2, zeros)


def _expert_kernel(be_ref, nu_ref, x_ref, wu_ref, bg_ref, bl_ref, wd_ref, bd_ref, y_ref, wg_s, wl_s, wd_s):
    i = pl.program_id(0)
    used = i < nu_ref[0]
    new_expert = (i == 0) | (be_ref[i] != be_ref[jnp.maximum(i - 1, 0)])

    @pl.when(used & new_expert)
    def _():
        r = lax.broadcasted_iota(jnp.int32, (2 * LANES, LANES), 0)
        c = lax.broadcasted_iota(jnp.int32, (2 * LANES, LANES), 1)
        even = jnp.where(r == 2 * c, 1.0, 0.0).astype(BF16)
        odd = jnp.where(r == 2 * c + 1, 1.0, 0.0).astype(BF16)
        for cb in range(wg_s.shape[1] // LANES):
            grp = wu_ref[0, :, cb * 2 * LANES:(cb + 1) * 2 * LANES].astype(BF16)
            out = slice(cb * LANES, (cb + 1) * LANES)
            wg_s[:, out] = jnp.dot(grp, even, preferred_element_type=F32).astype(BF16)
            wl_s[:, out] = jnp.dot(grp, odd, preferred_element_type=F32).astype(BF16)
        wd_s[...] = wd_ref[0].astype(BF16)

    @pl.when(used)
    def _():
        x = x_ref[...].astype(BF16)
        glu = jnp.dot(x, wg_s[...], preferred_element_type=F32) + bg_ref[0]
        lin = jnp.dot(x, wl_s[...], preferred_element_type=F32) + bl_ref[0]
        glu = jnp.minimum(glu, SWIGLU_LIMIT)
        lin = jnp.clip(lin, -SWIGLU_LIMIT, SWIGLU_LIMIT)
        h = glu * jax.nn.sigmoid(SWIGLU_ALPHA * glu) * (lin + 1.0)
        y_ref[...] = jnp.dot(h.astype(BF16), wd_s[...], preferred_element_type=F32) + bd_ref[0]

    @pl.when(jnp.logical_not(used))
    def _():
        y_ref[...] = jnp.zeros_like(y_ref)


def _moe_experts(block_e, n_used, xbuf, w_up, bg, bl, w_down, bd):
    n_rows, d = xbuf.shape
    de = w_down.shape[1]
    nblk = n_rows // MOE_ROWS
    rows = pl.BlockSpec((MOE_ROWS, d), lambda i, be, nu: (i, 0))
    wspec = lambda a: pl.BlockSpec((1,) + a.shape[1:], lambda i, be, nu: (be[i], 0, 0))
    return pl.pallas_call(
        _expert_kernel,
        grid_spec=pltpu.PrefetchScalarGridSpec(
            num_scalar_prefetch=2,
            grid=(nblk,),
            in_specs=[rows, wspec(w_up), wspec(bg), wspec(bl), wspec(w_down), wspec(bd)],
            out_specs=rows,
            scratch_shapes=[pltpu.VMEM((d, de), BF16), pltpu.VMEM((d, de), BF16), pltpu.VMEM((de, d), BF16)]),
        out_shape=jax.ShapeDtypeStruct((n_rows, d), F32),
        compiler_params=_cparams(("arbitrary",)),
        name="moe_experts",
    )(block_e, n_used, xbuf, w_up, bg, bl, w_down, bd)


def _combine_kernel(seg_ref, nch_ref, off_ref, pos_ref, gate_ref, x1_ref, g_ref, y_ref, o_ref, stage_ref, sem,
                    *, n_experts):
    rs, tt = stage_ref.shape[0], x1_ref.shape[0]

    @pl.when(pl.program_id(0) == 0)
    def _():
        stage_ref[...] = jnp.zeros_like(stage_ref)

    def chunk_copy(srow0, yrow0):
        return pltpu.make_async_copy(y_ref.at[pl.ds(yrow0, MOE_CHUNK)], stage_ref.at[pl.ds(srow0, MOE_CHUNK)], sem)

    _for_each_chunk(seg_ref, nch_ref, off_ref, n_experts, lambda s0, y0: chunk_copy(s0, y0).start())
    scol = lax.broadcasted_iota(jnp.int32, (tt, rs), 1)
    pos, gates = pos_ref[...], gate_ref[...]
    weights = jnp.zeros((tt, rs), F32)
    for k in range(TOP_K):
        weights = jnp.where(scol == pos[:, k:k + 1], gates[:, k:k + 1], weights)
    _for_each_chunk(seg_ref, nch_ref, off_ref, n_experts, lambda s0, y0: chunk_copy(s0, y0).wait())
    acc = x1_ref[...] + jnp.dot(weights.astype(BF16), stage_ref[...].astype(BF16), preferred_element_type=F32)
    o_ref[...] = (acc * lax.rsqrt(jnp.mean(acc * acc, axis=-1, keepdims=True) + NORM_EPS)) * g_ref[...]


def _moe_combine(seg, nch, off, pos, gates, x1, gain, ybuf, tile_t, n_experts):
    n, d = x1.shape
    return pl.pallas_call(
        functools.partial(_combine_kernel, n_experts=n_experts),
        grid_spec=pltpu.PrefetchScalarGridSpec(
            num_scalar_prefetch=3,
            grid=(n // tile_t,),
            in_specs=[pl.BlockSpec((tile_t, LANES), lambda i, *_: (i, 0)),
                      pl.BlockSpec((tile_t, LANES), lambda i, *_: (i, 0)),
                      pl.BlockSpec((tile_t, d), lambda i, *_: (i, 0)),
                      pl.BlockSpec((1, d), lambda i, *_: (0, 0)),
                      pl.BlockSpec(memory_space=pl.ANY)],
            out_specs=pl.BlockSpec((tile_t, d), lambda i, *_: (i, 0)),
            scratch_shapes=[pltpu.VMEM((_stage_rows(tile_t, n_experts), d), F32), pltpu.SemaphoreType.DMA]),
        out_shape=jax.ShapeDtypeStruct((n, d), F32),
        compiler_params=_cparams(("arbitrary",)),
        name="moe_combine",
    )(seg, nch, off, pos, gates, x1, gain, ybuf)


def kernel(x, norm1_g, w_in, hgrn_lb_logits, hgrn_norm_g, attn_norm_g, w_out, norm2_g, router_w, router_b,
           w_up, b_up, w_down, b_down, final_norm_g):
    b, s, d = x.shape
    n = b * s
    depth = w_in.shape[0]
    n_experts = router_w.shape[-1]
    d_mix = w_out.shape[1]
    d_hgrn = d_mix // 2
    d_attn = d_mix - d_hgrn
    assert w_in.shape[-1] == 4 * d_hgrn + 3 * d_attn and d_hgrn == d_attn
    assert all(s % win == 0 and win // dil == ATTN_BLOCK for win, dil in DILATED_PATTERNS)
    assert s % 512 == 0 and n % 512 == 0 and n_experts <= LANES
    assert depth == 1, "the final rmsnorm is fused into the single layer's MoE combine"

    lb_all = jnp.cumsum(jax.nn.softmax(hgrn_lb_logits.astype(F32), axis=0), axis=0)
    lane = jnp.arange(d_attn)
    head_mean = jnp.where((lane[:, None] // ATTN_HEAD_DIM) == (lane[None, :] // ATTN_HEAD_DIM),
                          1.0 / ATTN_HEAD_DIM, 0.0).astype(F32)
    x2 = x.reshape(n, d)
    l = 0
    dils = tuple(dil for _, dil in DILATED_PATTERNS)
    n_pat = len(dils)
    hq, hf, hi, hg, *attn = _in_proj(
        x2, norm1_g[l].reshape(1, d), w_in[l].astype(BF16),
        (BF16, F32, BF16, BF16), 3, dils, tile_m=512)
    to3 = lambda t: t.reshape(b, s, -1)
    o_a = _hgrn2(to3(hq), to3(hf), to3(hi), to3(hg), lb_all[l].reshape(1, d_hgrn),
                 hgrn_norm_g[l].reshape(1, d_hgrn), tile_s=512)
    obs, lses = [], []
    for p, dil in enumerate(dils):
        aq, ak, av = (attn[a * n_pat + p].reshape(b, s // dil, dil * d_attn) for a in range(3))
        o_p, lse_p = _dilated_attention(aq, ak, av, dil, group=2)
        obs.append(o_p.reshape(n // dil, dil * d_attn))
        lses.append(lse_p.reshape(n // dil, dil * d_attn))
    rw = jnp.zeros((d, LANES), F32).at[:, :n_experts].set(router_w[l])
    rw_hi = rw.astype(BF16)
    rw_lo = (rw - rw_hi.astype(F32)).astype(BF16)
    rb = jnp.full((1, LANES), -jnp.inf, F32).at[0, :n_experts].set(router_b[l])
    w_o = w_out[l].astype(BF16)
    x1, h2, logits = _out_proj(o_a.reshape(n, d_hgrn), obs, lses, dils, x2, attn_norm_g[l].reshape(1, d_attn),
                               head_mean.astype(BF16), w_o[:d_hgrn], w_o[d_hgrn:], norm2_g[l].reshape(1, d),
                               rw_hi, rw_lo, rb, tile_m=256)
    tile_t = 256
    gates, pos, post, start, nch, nfull, off, counts = _route(logits, tile_t)

    counts = counts[0, :n_experts]
    padded = (counts + MOE_ROWS - 1) // MOE_ROWS * MOE_ROWS
    pad_end = jnp.cumsum(padded)
    pad_start = pad_end - padded
    n_rows = n * TOP_K + n_experts * MOE_ROWS
    nblk = n_rows // MOE_ROWS
    block_row = jnp.arange(nblk, dtype=jnp.int32) * MOE_ROWS
    block_e = jnp.minimum(jnp.sum(pad_end[None, :] <= block_row[:, None], axis=1), n_experts - 1).astype(jnp.int32)
    n_used = (pad_end[-1:] // MOE_ROWS).astype(jnp.int32)
    per_expert = lambda t: t[:, 0, :n_experts]
    seg = (per_expert(start) + pad_start[None, :].astype(jnp.int32)).reshape(-1)
    nch, nfull, off = per_expert(nch), per_expert(nfull), per_expert(off).reshape(-1)
    nwr = nfull.at[-1].set(nch[-1]).reshape(-1)
    nch = nch.reshape(-1)

    xbuf = _moe_scatter(seg, nch, nwr, off, post, h2, n_rows, tile_t, n_experts)
    ybuf = _moe_experts(block_e, n_used, xbuf, w_up[l], b_up[l][:, None, 0::2], b_up[l][:, None, 1::2],
                        w_down[l], b_down[l][:, None, :])
    out = _moe_combine(seg, nch, off, pos, gates, x1, final_norm_g.reshape(1, d), ybuf, tile_t, n_experts)
    return out.reshape(b, s, d)
```

```python
import functools

import jax
import jax.numpy as jnp
from jax import lax
from jax.experimental import pallas as pl
from jax.experimental.pallas import tpu as pltpu

F32 = jnp.float32
BF16 = jnp.bfloat16

NORM_EPS = 1e-6
HGRN_HEAD_DIM = 128
ATTN_HEAD_DIM = 64
DILATED_PATTERNS = ((128, 1), (512, 4), (2048, 16))
TOP_K = 4
SWIGLU_ALPHA = 1.702
SWIGLU_LIMIT = 7.0

LANES = 128
SUBLANES = 8
HGRN_CHUNK = 64
ATTN_BLOCK = 128
MOE_ROWS = 512
MOE_CHUNK = SUBLANES
VMEM_LIMIT = 56 * 1024 * 1024

_NT = (((1,), (1,)), ((), ()))
_TN = (((0,), (0,)), ((), ()))


def _cparams(sem):
    return pltpu.CompilerParams(dimension_semantics=sem, vmem_limit_bytes=VMEM_LIMIT)


def _to_classes(res, o_refs, dils, slab_refs):
    tm, width = res.shape
    n_slab = width // LANES
    for sb in range(n_slab):
        slab_refs[0][sb] = res[:, sb * LANES:(sb + 1) * LANES]
    prev = 1
    for p, (o_ref, dil) in enumerate(zip(o_refs, dils)):
        if dil == 1:
            o_ref[...] = res.astype(o_ref.dtype)
            continue
        ratio, src, last = dil // prev, slab_refs[0] if prev == 1 else slab_refs[1], p == len(dils) - 1
        assert dil % prev == 0 and (prev == 1 or last), "one intermediate slab: at most two strided levels"
        rows = tm // dil
        for rp in range(prev):
            for r2 in range(ratio):
                r = r2 * prev + rp
                for sb in range(n_slab):
                    part = src[sb, pl.ds(rp * (tm // prev) + r2, rows, stride=ratio), :]
                    col = r * width + sb * LANES
                    o_ref[:, col:col + LANES] = part.astype(o_ref.dtype)
                    if not last:
                        slab_refs[1][sb, r * rows:(r + 1) * rows, :] = part
        prev = dil


def _from_classes(ref, slab_ref, mid_ref, dil, ratio):
    if dil == 1:
        return ref[...].astype(F32)
    rows = ref.shape[0]
    tm, width = rows * dil, ref.shape[1] // dil
    n_slab = width // LANES
    assert dil in (ratio, ratio * ratio)
    prev = dil // ratio
    for rp in range(prev):
        for r2 in range(ratio):
            r = r2 * prev + rp
            for sb in range(n_slab):
                col = r * width + sb * LANES
                part = ref[:, col:col + LANES].astype(F32)
                if prev == 1:
                    slab_ref[sb, pl.ds(r2, rows, stride=ratio), :] = part
                else:
                    mid_ref[sb, pl.ds(rp * (tm // prev) + r2, rows, stride=ratio), :] = part
    if prev > 1:
        for rp in range(prev):
            for sb in range(n_slab):
                slab_ref[sb, pl.ds(rp, tm // prev, stride=prev), :] = mid_ref[sb, rp * (tm // prev):(rp + 1) * (tm // prev), :]
    return jnp.concatenate([slab_ref[sb] for sb in range(n_slab)], axis=-1)


def _in_proj_kernel(x_ref, g_ref, w_ref, *refs, plain_dtypes, n_attn, dils):
    out_refs, slab_refs = refs[:-2], refs[-2:]
    x = x_ref[...]
    r = lax.rsqrt(jnp.mean(x * x, axis=-1, keepdims=True) + NORM_EPS)
    h = ((x * r) * g_ref[...]).astype(BF16)
    width = out_refs[0].shape[-1]
    n_plain = len(plain_dtypes)
    for j in list(range(n_plain, n_plain + n_attn)) + list(range(n_plain)):
        res = jnp.dot(h, w_ref[:, j * width:(j + 1) * width], preferred_element_type=F32)
        if j < n_plain:
            out_refs[j][...] = res.astype(out_refs[j].dtype)
        else:
            first = n_plain + (j - n_plain) * len(dils)
            _to_classes(res, out_refs[first:first + len(dils)], dils, slab_refs)


def _in_proj(x2, gain, w_bf16, plain_dtypes, n_attn, dils, tile_m):
    n, d = x2.shape
    width = w_bf16.shape[1] // (len(plain_dtypes) + n_attn)
    out_specs = [pl.BlockSpec((tile_m, width), lambda i: (i, 0)) for _ in plain_dtypes]
    out_shape = [jax.ShapeDtypeStruct((n, width), dt) for dt in plain_dtypes]
    for _ in range(n_attn):
        for dil in dils:
            out_specs.append(pl.BlockSpec((tile_m // dil, dil * width), lambda i: (i, 0)))
            out_shape.append(jax.ShapeDtypeStruct((n // dil, dil * width), BF16))
    return pl.pallas_call(
        functools.partial(_in_proj_kernel, plain_dtypes=plain_dtypes, n_attn=n_attn, dils=dils),
        grid=(n // tile_m,),
        in_specs=[pl.BlockSpec((tile_m, d), lambda i: (i, 0)),
                  pl.BlockSpec((1, d), lambda i: (0, 0)),
                  pl.BlockSpec(w_bf16.shape, lambda i: (0, 0))],
        out_specs=out_specs,
        out_shape=out_shape,
        scratch_shapes=[pltpu.VMEM((width // LANES, tile_m, LANES), F32) for _ in range(2)],
        compiler_params=_cparams(("arbitrary",)),
        name="in_proj",
    )(x2, gain, w_bf16)


def _hgrn_kernel(q_ref, f_ref, i_ref, g_ref, lb_ref, gain_ref, o_ref, state_ref, *, n_heads):
    hd = HGRN_HEAD_DIM
    width = n_heads * hd
    nblk = HGRN_CHUNK // SUBLANES

    @pl.when(pl.program_id(1) == 0)
    def _():
        state_ref[...] = jnp.zeros_like(state_ref)

    lb = lb_ref[...]
    gain = gain_ref[...]
    row8 = lax.broadcasted_iota(jnp.int32, (SUBLANES, width), 0)
    rowc = lax.broadcasted_iota(jnp.int32, (SUBLANES, HGRN_CHUNK), 0)
    colc = lax.broadcasted_iota(jnp.int32, (SUBLANES, HGRN_CHUNK), 1)
    r64 = lax.broadcasted_iota(jnp.int32, (HGRN_CHUNK, HGRN_CHUNK), 0)
    c64 = lax.broadcasted_iota(jnp.int32, (HGRN_CHUNK, HGRN_CHUNK), 1)
    same32 = (r64 // 32) == (c64 // 32)
    same16 = (r64 // 16) == (c64 // 16)

    def scan8(x):
        for s in (1, 2, 4):
            x = x + jnp.where(row8 >= s, pltpu.roll(x, s, axis=0), 0.0)
        return x

    def cat(blocks):
        return jnp.concatenate(blocks, axis=0)

    def chunk(ci, carry):
        r0 = pl.multiple_of(ci * HGRN_CHUNK, HGRN_CHUNK)
        rows = pl.ds(r0, HGRN_CHUNK)
        q = q_ref[0, rows, :].astype(F32)
        v = i_ref[0, rows, :]
        f = lb + (1.0 - lb) * jax.nn.sigmoid(f_ref[0, rows, :])
        logf = jnp.log2(f)
        kk = 1.0 - f

        qb = [q[SUBLANES * i:SUBLANES * (i + 1)] for i in range(nblk)]
        kb = [kk[SUBLANES * i:SUBLANES * (i + 1)] for i in range(nblk)]
        b8 = [scan8(logf[SUBLANES * i:SUBLANES * (i + 1)]) for i in range(nblk)]
        t8 = [jnp.broadcast_to(x[SUBLANES - 1:SUBLANES], x.shape) for x in b8]
        b16 = [b8[i] + t8[i - 1] if i % 2 else b8[i] for i in range(nblk)]
        t16 = [t8[i - i % 2] + t8[i - i % 2 + 1] for i in range(nblk)]
        b32 = [b16[i] + t16[i - 2] if (i // 2) % 2 else b16[i] for i in range(nblk)]
        t32 = [t16[i - i % 4] + t16[i - i % 4 + 2] for i in range(nblk)]
        b64 = [b32[i] + t32[0] if i >= 4 else b32[i] for i in range(nblk)]
        t64 = t32[0] + t32[4]

        zero = jnp.zeros_like(qb[0])

        def q_side(bl, span):
            return cat([qb[i] * jnp.exp2(bl[i]) if (i * SUBLANES // span) % 2 else zero for i in range(nblk)])

        def k_side(bl, tl, span):
            return cat([zero if (i * SUBLANES // span) % 2 else kb[i] * jnp.exp2(tl[i] - bl[i]) for i in range(nblk)])

        q64 = cat([qb[i] * jnp.exp2(b64[i]) for i in range(nblk)]).astype(BF16)
        k64 = cat([kb[i] * jnp.exp2(t64 - b64[i]) for i in range(nblk)]).astype(BF16)
        q32, k32 = q_side(b32, 32).astype(BF16), k_side(b32, t32, 32).astype(BF16)
        q16, k16 = q_side(b16, 16).astype(BF16), k_side(b16, t16, 16).astype(BF16)
        q8, k8 = q_side(b8, 8).astype(BF16), k_side(b8, t8, 8).astype(BF16)
        decay = jnp.exp2(t64[0:1])

        diag = [[jnp.zeros((SUBLANES, HGRN_CHUNK), F32) for _ in range(nblk)] for _ in range(n_heads)]
        for j in range(SUBLANES):
            for i in range(nblk):
                if j == 0:
                    p = qb[i] * kb[i]
                else:
                    p = qb[i] * pltpu.roll(kb[i], j, axis=0) * jnp.exp2(b8[i] - pltpu.roll(b8[i], j, axis=0))
                hit = (colc == rowc + (SUBLANES * i - j)) & (rowc >= j)
                for h in range(n_heads):
                    w = jnp.sum(p[:, h * hd:(h + 1) * hd], axis=-1, keepdims=True)
                    diag[h][i] = diag[h][i] + jnp.where(hit, w, 0.0)

        outs = []
        for h in range(n_heads):
            sl = slice(h * hd, (h + 1) * hd)
            a = cat(diag[h])
            a = a + lax.dot_general(q32[:, sl], k32[:, sl], _NT, preferred_element_type=F32)
            a = a + jnp.where(same32, lax.dot_general(q16[:, sl], k16[:, sl], _NT, preferred_element_type=F32), 0.0)
            a = a + jnp.where(same16, lax.dot_general(q8[:, sl], k8[:, sl], _NT, preferred_element_type=F32), 0.0)
            st = state_ref[h]
            o = lax.dot_general(q64[:, sl], st.astype(BF16), _NT, preferred_element_type=F32)
            o = o + jnp.dot(a.astype(BF16), v[:, sl], preferred_element_type=F32)
            state_ref[h] = st * decay[:, sl] + lax.dot_general(v[:, sl], k64[:, sl], _TN,
                                                              preferred_element_type=F32)
            o = o * lax.rsqrt(jnp.mean(o * o, axis=-1, keepdims=True) + NORM_EPS)
            outs.append(o)
        o = jnp.concatenate(outs, axis=-1) * gain
        o_ref[0, rows, :] = (o * jax.nn.silu(g_ref[0, rows, :].astype(F32))).astype(o_ref.dtype)
        return carry

    lax.fori_loop(0, q_ref.shape[1] // HGRN_CHUNK, chunk, 0, unroll=2)


def _hgrn2(hq, hf, hi, hg, lb, gain, tile_s):
    b, s, width = hq.shape
    n_heads = width // HGRN_HEAD_DIM
    blk = pl.BlockSpec((1, tile_s, width), lambda bi, si: (bi, si, 0))
    vec = pl.BlockSpec((1, width), lambda bi, si: (0, 0))
    return pl.pallas_call(
        functools.partial(_hgrn_kernel, n_heads=n_heads),
        grid=(b, s // tile_s),
        in_specs=[blk, blk, blk, blk, vec, vec],
        out_specs=blk,
        out_shape=jax.ShapeDtypeStruct((b, s, width), BF16),
        scratch_shapes=[pltpu.VMEM((n_heads, HGRN_HEAD_DIM, HGRN_HEAD_DIM), F32)],
        compiler_params=_cparams(("arbitrary", "arbitrary")),
        name="hgrn2",
    )(hq, hf, hi, hg, lb, gain)


def _attn_kernel(qc_ref, kc_ref, kp_ref, vc_ref, vp_ref, o_ref, lse_ref, *, group, scale):
    blk = ATTN_BLOCK
    n_pairs = qc_ref.shape[-1] // LANES
    lane = lax.broadcasted_iota(jnp.int32, (1, LANES), 1)
    low = lane < ATTN_HEAD_DIM
    qi = lax.broadcasted_iota(jnp.int32, (blk, 2 * blk), 0)
    ki = lax.broadcasted_iota(jnp.int32, (blk, 2 * blk), 1)
    band = ((ki >= blk) & (ki - blk <= qi)) | ((ki < blk) & (ki >= qi))
    first_key = jnp.where(pl.program_id(2) == 0, blk, 0)
    band_first = band & (ki >= first_key)

    for g in range(group):
        rows = slice(g * blk, (g + 1) * blk)
        prev = slice((g - 1) * blk, g * blk)
        mask = band_first if g == 0 else band
        for pr in range(n_pairs):
            sl = slice(pr * LANES, (pr + 1) * LANES)
            q = qc_ref[0, rows, sl]
            if g == 0:
                k2 = jnp.concatenate([kp_ref[0, :, sl], kc_ref[0, rows, sl]], axis=0)
                v2 = jnp.concatenate([vp_ref[0, :, sl], vc_ref[0, rows, sl]], axis=0)
            else:
                k2 = jnp.concatenate([kc_ref[0, prev, sl], kc_ref[0, rows, sl]], axis=0)
                v2 = jnp.concatenate([vc_ref[0, prev, sl], vc_ref[0, rows, sl]], axis=0)
            acc = None
            lse = None
            for half in (low, jnp.logical_not(low)):
                qh = jnp.where(half, q, jnp.zeros_like(q))
                vh = jnp.where(half, v2, jnp.zeros_like(v2))
                s = lax.dot_general(qh, k2, _NT, preferred_element_type=F32) * scale
                s = jnp.where(mask, s, -jnp.inf)
                m = jnp.max(s, axis=-1, keepdims=True)
                p = jnp.exp(s - m)
                den = jnp.sum(p, axis=-1, keepdims=True)
                oh = jnp.dot(p.astype(BF16), vh, preferred_element_type=F32) / den
                lh = m + jnp.log(den)
                acc = oh if acc is None else acc + oh
                lse = jnp.broadcast_to(lh, (blk, LANES)) if lse is None else jnp.where(low, lse, lh)
            o_ref[0, rows, sl] = acc.astype(o_ref.dtype)
            lse_ref[0, rows, sl] = lse


def _dilated_attention(aq, ak, av, dil, group):
    b, m, width = aq.shape
    width //= dil
    nb = m // ATTN_BLOCK
    group = min(group, nb)
    tq = group * ATTN_BLOCK
    cur = pl.BlockSpec((1, tq, width), lambda bi, r, n: (bi, n, r))
    prv = pl.BlockSpec((1, ATTN_BLOCK, width), lambda bi, r, n: (bi, jnp.maximum(n * group - 1, 0), r))
    o, lse = pl.pallas_call(
        functools.partial(_attn_kernel, group=group, scale=ATTN_HEAD_DIM ** -0.5),
        grid=(b, dil, nb // group),
        in_specs=[cur, cur, prv, cur, prv],
        out_specs=[cur, cur],
        out_shape=[jax.ShapeDtypeStruct((b, m, dil * width), BF16),
                   jax.ShapeDtypeStruct((b, m, dil * width), F32)],
        compiler_params=_cparams(("arbitrary", "arbitrary", "arbitrary")),
        name=f"dilated_attn_d{dil}",
    )(aq, ak, ak, av, av)
    return o, lse


def _split_bf16(v):
    hi = v.astype(BF16)
    return hi, (v - hi.astype(F32)).astype(BF16)


def _out_proj_kernel(oa_ref, *refs, dils):
    n_pat = len(dils)
    o_refs, l_refs = refs[:n_pat], refs[n_pat:2 * n_pat]
    (x_ref, gb_ref, hm_ref, wa_ref, wb_ref, g2_ref, rwh_ref, rwl_ref, rb_ref,
     x1_ref, h2_ref, lg_ref) = refs[2 * n_pat:2 * n_pat + 12]
    slabs = refs[2 * n_pat + 12:]
    ratio = min(d for d in dils if d > 1)
    outs = [_from_classes(o_refs[p], slabs[4 * p], slabs[4 * p + 1], dils[p], ratio) for p in range(n_pat)]
    lses = [_from_classes(l_refs[p], slabs[4 * p + 2], slabs[4 * p + 3], dils[p], ratio) for p in range(n_pat)]
    mx = functools.reduce(jnp.maximum, lses)
    es = [jnp.exp(l - mx) for l in lses]
    ob = sum(e * o for e, o in zip(es, outs)) / sum(es)
    sq_hi, sq_lo = _split_bf16(ob * ob)
    ms = (jnp.dot(sq_hi, hm_ref[...], preferred_element_type=F32)
          + jnp.dot(sq_lo, hm_ref[...], preferred_element_type=F32))
    obn = (ob * lax.rsqrt(ms + NORM_EPS) * gb_ref[...]).astype(BF16)
    y = jnp.dot(oa_ref[...], wa_ref[...], preferred_element_type=F32)
    y = y + jnp.dot(obn, wb_ref[...], preferred_element_type=F32)
    x1 = x_ref[...] + y
    x1_ref[...] = x1
    h2 = (x1 * lax.rsqrt(jnp.mean(x1 * x1, axis=-1, keepdims=True) + NORM_EPS)) * g2_ref[...]
    h2_ref[...] = h2
    h_hi, h_lo = _split_bf16(h2)
    lg_ref[...] = (jnp.dot(h_hi, rwh_ref[...], preferred_element_type=F32)
                   + jnp.dot(h_hi, rwl_ref[...], preferred_element_type=F32)
                   + jnp.dot(h_lo, rwh_ref[...], preferred_element_type=F32)) + rb_ref[...]


def _out_proj(oa, obs, lses, dils, x2, gain_b, head_mean, w_a, w_b, gain2, rw_hi, rw_lo, rb, tile_m):
    n, d = x2.shape
    wm = oa.shape[1]
    row = lambda w: pl.BlockSpec((tile_m, w), lambda i: (i, 0))
    cls = [pl.BlockSpec((tile_m // dil, dil * wm), lambda i: (i, 0)) for dil in dils]
    full = lambda a: pl.BlockSpec(a.shape, lambda i: (0, 0))
    consts = (gain_b, head_mean, w_a, w_b, gain2, rw_hi, rw_lo, rb)
    return pl.pallas_call(
        functools.partial(_out_proj_kernel, dils=dils),
        grid=(n // tile_m,),
        in_specs=[row(wm)] + cls + cls + [row(d)] + [full(a) for a in consts],
        out_specs=[row(d), row(d), row(LANES)],
        out_shape=[jax.ShapeDtypeStruct((n, d), F32), jax.ShapeDtypeStruct((n, d), F32),
                   jax.ShapeDtypeStruct((n, LANES), F32)],
        scratch_shapes=[pltpu.VMEM((wm // LANES, tile_m, LANES), F32) for _ in range(4 * len(dils))],
        compiler_params=_cparams(("arbitrary",)),
        name="out_proj",
    )(oa, *obs, *lses, x2, *consts)


def _route_kernel(lg_ref, gate_ref, pos_ref, post_ref, start_ref, nch_ref, nfull_ref, off_ref, cnt_ref,
                  carry_ref):
    tr = lg_ref.shape[0]

    @pl.when(pl.program_id(0) == 0)
    def _():
        carry_ref[...] = jnp.zeros_like(carry_ref)

    logits = lg_ref[...]
    lane = lax.broadcasted_iota(jnp.int32, (tr, LANES), 1).astype(F32)
    vals, idxs = [], []
    for _ in range(TOP_K):
        m = jnp.max(logits, axis=-1, keepdims=True)
        ix = jnp.min(jnp.where(logits == m, lane, float(LANES)), axis=-1, keepdims=True)
        vals.append(m)
        idxs.append(ix)
        logits = jnp.where(lane == ix, -jnp.inf, logits)
    exps = [jnp.exp(v - vals[0]) for v in vals]
    den = exps[0] + exps[1] + exps[2] + exps[3]
    chosen = jnp.zeros((tr, LANES), F32)
    for ix in idxs:
        chosen = chosen + jnp.where(lane == ix, 1.0, 0.0)
    r = lax.broadcasted_iota(jnp.int32, (tr, tr), 0)
    c = lax.broadcasted_iota(jnp.int32, (tr, tr), 1)
    before = jnp.dot(jnp.where(r > c, 1.0, 0.0).astype(BF16), chosen.astype(BF16), preferred_element_type=F32)
    count = jnp.sum(chosen, axis=0, keepdims=True)
    carry = carry_ref[...]
    head = carry - MOE_CHUNK * jnp.floor(carry * (1.0 / MOE_CHUNK))
    nfull = jnp.floor((head + count) * (1.0 / MOE_CHUNK))
    nch = jnp.floor((head + count + (MOE_CHUNK - 1)) * (1.0 / MOE_CHUNK))
    el = lax.broadcasted_iota(jnp.int32, (LANES, LANES), 0)
    ec = lax.broadcasted_iota(jnp.int32, (LANES, LANES), 1)
    off = MOE_CHUNK * jnp.dot(jnp.broadcast_to(nch, (SUBLANES, LANES)).astype(BF16),
                              jnp.where(el < ec, 1.0, 0.0).astype(BF16), preferred_element_type=F32)[0:1]
    slot = before + off + head
    gate_o = jnp.zeros((tr, LANES), F32)
    pos_o = jnp.zeros((tr, LANES), F32)
    for k in range(TOP_K):
        pk = jnp.sum(jnp.where(lane == idxs[k], slot, 0.0), axis=-1, keepdims=True)
        gate_o = jnp.where(lane == float(k), exps[k] / den, gate_o)
        pos_o = jnp.where(lane == float(k), pk, pos_o)
    gate_ref[...] = gate_o
    pos_ref[...] = pos_o.astype(jnp.int32)
    post_ref[...] = pos_o.T[0:SUBLANES].astype(jnp.int32)
    start_ref[0] = (carry - head).astype(jnp.int32)
    nch_ref[0] = nch.astype(jnp.int32)
    nfull_ref[0] = nfull.astype(jnp.int32)
    off_ref[0] = off.astype(jnp.int32)
    carry_ref[...] = carry + count
    cnt_ref[...] = (carry + count).astype(jnp.int32)


def _route(logits, tile_t):
    n = logits.shape[0]
    nt = n // tile_t
    row = pl.BlockSpec((tile_t, LANES), lambda i: (i, 0))
    tab = pl.BlockSpec((1, 1, LANES), lambda i: (i, 0, 0))
    tab_shape = jax.ShapeDtypeStruct((nt, 1, LANES), jnp.int32)
    return pl.pallas_call(
        _route_kernel,
        grid=(nt,),
        in_specs=[row],
        out_specs=[row, row, pl.BlockSpec((SUBLANES, tile_t), lambda i: (i, 0)), tab, tab, tab, tab,
                   pl.BlockSpec((1, LANES), lambda i: (0, 0))],
        out_shape=[jax.ShapeDtypeStruct((n, LANES), F32), jax.ShapeDtypeStruct((n, LANES), jnp.int32),
                   jax.ShapeDtypeStruct((nt * SUBLANES, tile_t), jnp.int32), tab_shape, tab_shape, tab_shape,
                   tab_shape, jax.ShapeDtypeStruct((1, LANES), jnp.int32)],
        scratch_shapes=[pltpu.VMEM((1, LANES), F32)],
        compiler_params=_cparams(("arbitrary",)),
        name="route",
    )(logits)


def _for_each_chunk(tile, seg_ref, cnt_ref, off_ref, n_experts, fn):
    base = tile * n_experts

    def expert(e, carry):
        seg = pl.multiple_of(seg_ref[base + e], MOE_CHUNK)
        off = pl.multiple_of(off_ref[base + e], MOE_CHUNK)

        def chunk(c, carry):
            fn(pl.multiple_of(off + c * MOE_CHUNK, MOE_CHUNK), pl.multiple_of(seg + c * MOE_CHUNK, MOE_CHUNK))
            return carry

        return lax.fori_loop(0, cnt_ref[base + e], chunk, carry)

    lax.fori_loop(0, n_experts, expert, 0)


def _pad_sizes():
    sizes, s = [], MOE_ROWS // 2
    while s >= MOE_CHUNK:
        sizes.append(s)
        s //= 2
    return sizes


def _scatter_kernel(seg_ref, nch_ref, nwr_ref, off_ref, padrow_ref, padlen_ref, nused_ref, post_ref, h_ref,
                    xout_ref, stage_ref, open_ref, zero_ref, sems, zsem, *, n_experts):
    rs, tt = stage_ref.shape[1], h_ref.shape[0]
    i, nt = pl.program_id(0), pl.num_programs(0)
    slot = i % 2
    base = i * n_experts
    stage = stage_ref.at[slot]

    @pl.when(i == 0)
    def _():
        open_ref[...] = jnp.zeros_like(open_ref)
        zero_ref[...] = jnp.zeros_like(zero_ref)

    srow = lax.broadcasted_iota(jnp.int32, (rs, tt), 0)
    post = post_ref[...]
    hit = srow == post[0:1]
    for k in range(1, TOP_K):
        hit = hit | (srow == post[k:k + 1])
    sel = jnp.where(hit, 1.0, 0.0).astype(BF16)
    stage[...] = jnp.dot(sel, h_ref[...].astype(BF16), preferred_element_type=F32)

    def add_open(e, carry):
        rows = pl.ds(pl.multiple_of(off_ref[base + e], MOE_CHUNK), MOE_CHUNK)
        stage[rows, :] = stage[rows, :] + open_ref[e]
        return carry

    def save_open(e, carry):
        nwr = nwr_ref[base + e]
        rows = pl.ds(pl.multiple_of(off_ref[base + e] + nwr * MOE_CHUNK, MOE_CHUNK), MOE_CHUNK)
        still_open = nch_ref[base + e] > nwr
        open_ref[e] = jnp.where(still_open, stage[rows, :], 0.0)
        return carry

    def chunk_copy(sl, srow0, xrow0):
        return pltpu.make_async_copy(stage_ref.at[sl, pl.ds(srow0, MOE_CHUNK)],
                                     xout_ref.at[pl.ds(xrow0, MOE_CHUNK)], sems.at[sl])

    def pad_copies(fn):
        def zeros_to(start, size):
            fn(pltpu.make_async_copy(zero_ref.at[pl.ds(0, size)], xout_ref.at[pl.ds(start, size)], zsem))

        def expert(e, carry):
            row, length = pl.multiple_of(padrow_ref[e], MOE_CHUNK), padlen_ref[e]
            for size in _pad_sizes():
                @pl.when((length & size) != 0)
                def _():
                    zeros_to(pl.multiple_of(row + (length & ~(2 * size - 1)), MOE_CHUNK), size)
            return carry

        def unused_half_block(c, carry):
            zeros_to(pl.multiple_of(c * (MOE_ROWS // 2), MOE_ROWS // 2), MOE_ROWS // 2)
            return carry

        lax.fori_loop(0, n_experts, expert, 0)
        lax.fori_loop(2 * nused_ref[0], xout_ref.shape[0] // (MOE_ROWS // 2), unused_half_block, 0)

    lax.fori_loop(0, n_experts, add_open, 0)

    @pl.when(i > 0)
    def _():
        _for_each_chunk(i - 1, seg_ref, nwr_ref, off_ref, n_experts,
                        lambda s0, x0: chunk_copy(1 - slot, s0, x0).wait())

    _for_each_chunk(i, seg_ref, nwr_ref, off_ref, n_experts, lambda s0, x0: chunk_copy(slot, s0, x0).start())
    lax.fori_loop(0, n_experts, save_open, 0)

    @pl.when(i == nt - 1)
    def _():
        pad_copies(lambda cp: cp.start())
        _for_each_chunk(i, seg_ref, nwr_ref, off_ref, n_experts, lambda s0, x0: chunk_copy(slot, s0, x0).wait())
        pad_copies(lambda cp: cp.wait())


def _stage_rows(tile_t, n_experts):
    return tile_t * TOP_K + (2 * n_experts + 1) * MOE_CHUNK


def _moe_scatter(seg, nch, nwr, off, padrow, padlen, n_used, post, h2, n_rows, tile_t, n_experts):
    n, d = h2.shape
    return pl.pallas_call(
        functools.partial(_scatter_kernel, n_experts=n_experts),
        grid_spec=pltpu.PrefetchScalarGridSpec(
            num_scalar_prefetch=7,
            grid=(n // tile_t,),
            in_specs=[pl.BlockSpec((SUBLANES, tile_t), lambda i, *_: (i, 0)),
                      pl.BlockSpec((tile_t, d), lambda i, *_: (i, 0))],
            out_specs=pl.BlockSpec(memory_space=pl.ANY),
            scratch_shapes=[pltpu.VMEM((2, _stage_rows(tile_t, n_experts), d), F32),
                            pltpu.VMEM((n_experts, MOE_CHUNK, d), F32),
                            pltpu.VMEM((MOE_ROWS // 2, d), F32),
                            pltpu.SemaphoreType.DMA((2,)), pltpu.SemaphoreType.DMA]),
        out_shape=jax.ShapeDtypeStruct((n_rows, d), F32),
        compiler_params=_cparams(("arbitrary",)),
        name="moe_scatter",
    )(seg, nch, nwr, off, padrow, padlen, n_used, post, h2)


def _expert_kernel(be_ref, nu_ref, x_ref, wu_ref, bg_ref, bl_ref, wd_ref, bd_ref, y_ref, wg_s, wl_s, wd_s):
    i = pl.program_id(0)
    used = i < nu_ref[0]
    new_expert = (i == 0) | (be_ref[i] != be_ref[jnp.maximum(i - 1, 0)])

    @pl.when(used & new_expert)
    def _():
        r = lax.broadcasted_iota(jnp.int32, (2 * LANES, LANES), 0)
        c = lax.broadcasted_iota(jnp.int32, (2 * LANES, LANES), 1)
        even = jnp.where(r == 2 * c, 1.0, 0.0).astype(BF16)
        odd = jnp.where(r == 2 * c + 1, 1.0, 0.0).astype(BF16)
        for cb in range(wg_s.shape[1] // LANES):
            grp = wu_ref[0, :, cb * 2 * LANES:(cb + 1) * 2 * LANES].astype(BF16)
            out = slice(cb * LANES, (cb + 1) * LANES)
            wg_s[:, out] = jnp.dot(grp, even, preferred_element_type=F32).astype(BF16)
            wl_s[:, out] = jnp.dot(grp, odd, preferred_element_type=F32).astype(BF16)
        wd_s[...] = wd_ref[0].astype(BF16)

    @pl.when(used)
    def _():
        x = x_ref[...].astype(BF16)
        glu = jnp.dot(x, wg_s[...], preferred_element_type=F32) + bg_ref[0]
        lin = jnp.dot(x, wl_s[...], preferred_element_type=F32) + bl_ref[0]
        glu = jnp.minimum(glu, SWIGLU_LIMIT)
        lin = jnp.clip(lin, -SWIGLU_LIMIT, SWIGLU_LIMIT)
        h = glu * jax.nn.sigmoid(SWIGLU_ALPHA * glu) * (lin + 1.0)
        y_ref[...] = jnp.dot(h.astype(BF16), wd_s[...], preferred_element_type=F32) + bd_ref[0]

    @pl.when(jnp.logical_not(used))
    def _():
        y_ref[...] = jnp.zeros_like(y_ref)


def _moe_experts(block_e, n_used, xbuf, w_up, bg, bl, w_down, bd):
    n_rows, d = xbuf.shape
    de = w_down.shape[1]
    nblk = n_rows // MOE_ROWS
    rows = pl.BlockSpec((MOE_ROWS, d), lambda i, be, nu: (i, 0))
    rows_in = pl.BlockSpec((MOE_ROWS, d), lambda i, be, nu: (jnp.minimum(i, jnp.maximum(nu[0] - 1, 0)), 0))
    wspec = lambda a: pl.BlockSpec((1,) + a.shape[1:], lambda i, be, nu: (be[i], 0, 0))
    return pl.pallas_call(
        _expert_kernel,
        grid_spec=pltpu.PrefetchScalarGridSpec(
            num_scalar_prefetch=2,
            grid=(nblk,),
            in_specs=[rows_in, wspec(w_up), wspec(bg), wspec(bl), wspec(w_down), wspec(bd)],
            out_specs=rows,
            scratch_shapes=[pltpu.VMEM((d, de), BF16), pltpu.VMEM((d, de), BF16), pltpu.VMEM((de, d), BF16)]),
        out_shape=jax.ShapeDtypeStruct((n_rows, d), F32),
        compiler_params=_cparams(("arbitrary",)),
        name="moe_experts",
    )(block_e, n_used, xbuf, w_up, bg, bl, w_down, bd)


def _combine_kernel(seg_ref, nch_ref, off_ref, pos_ref, gate_ref, x1_ref, g_ref, y_ref, o_ref, stage_ref, sems,
                    *, n_experts):
    rs, tt = stage_ref.shape[1], x1_ref.shape[0]
    i, nt = pl.program_id(0), pl.num_programs(0)
    slot = i % 2

    def chunk_copy(sl, srow0, yrow0):
        return pltpu.make_async_copy(y_ref.at[pl.ds(yrow0, MOE_CHUNK)],
                                     stage_ref.at[sl, pl.ds(srow0, MOE_CHUNK)], sems.at[sl])

    def fetch(tile, sl):
        _for_each_chunk(tile, seg_ref, nch_ref, off_ref, n_experts, lambda s0, y0: chunk_copy(sl, s0, y0).start())

    @pl.when(i == 0)
    def _():
        stage_ref[...] = jnp.zeros_like(stage_ref)
        fetch(i, slot)

    @pl.when(i + 1 < nt)
    def _():
        fetch(i + 1, 1 - slot)

    scol = lax.broadcasted_iota(jnp.int32, (tt, rs), 1)
    pos, gates = pos_ref[...], gate_ref[...]
    weights = jnp.zeros((tt, rs), F32)
    for k in range(TOP_K):
        weights = jnp.where(scol == pos[:, k:k + 1], gates[:, k:k + 1], weights)
    _for_each_chunk(i, seg_ref, nch_ref, off_ref, n_experts, lambda s0, y0: chunk_copy(slot, s0, y0).wait())
    acc = x1_ref[...] + jnp.dot(weights.astype(BF16), stage_ref[slot].astype(BF16), preferred_element_type=F32)
    o_ref[...] = (acc * lax.rsqrt(jnp.mean(acc * acc, axis=-1, keepdims=True) + NORM_EPS)) * g_ref[...]


def _moe_combine(seg, nch, off, pos, gates, x1, gain, ybuf, tile_t, n_experts):
    n, d = x1.shape
    return pl.pallas_call(
        functools.partial(_combine_kernel, n_experts=n_experts),
        grid_spec=pltpu.PrefetchScalarGridSpec(
            num_scalar_prefetch=3,
            grid=(n // tile_t,),
            in_specs=[pl.BlockSpec((tile_t, LANES), lambda i, *_: (i, 0)),
                      pl.BlockSpec((tile_t, LANES), lambda i, *_: (i, 0)),
                      pl.BlockSpec((tile_t, d), lambda i, *_: (i, 0)),
                      pl.BlockSpec((1, d), lambda i, *_: (0, 0)),
                      pl.BlockSpec(memory_space=pl.ANY)],
            out_specs=pl.BlockSpec((tile_t, d), lambda i, *_: (i, 0)),
            scratch_shapes=[pltpu.VMEM((2, _stage_rows(tile_t, n_experts), d), F32),
                            pltpu.SemaphoreType.DMA((2,))]),
        out_shape=jax.ShapeDtypeStruct((n, d), F32),
        compiler_params=_cparams(("arbitrary",)),
        name="moe_combine",
    )(seg, nch, off, pos, gates, x1, gain, ybuf)


def kernel(x, norm1_g, w_in, hgrn_lb_logits, hgrn_norm_g, attn_norm_g, w_out, norm2_g, router_w, router_b,
           w_up, b_up, w_down, b_down, final_norm_g):
    b, s, d = x.shape
    n = b * s
    depth = w_in.shape[0]
    n_experts = router_w.shape[-1]
    d_mix = w_out.shape[1]
    d_hgrn = d_mix // 2
    d_attn = d_mix - d_hgrn
    assert w_in.shape[-1] == 4 * d_hgrn + 3 * d_attn and d_hgrn == d_attn
    assert all(s % win == 0 and win // dil == ATTN_BLOCK for win, dil in DILATED_PATTERNS)
    assert s % 512 == 0 and n % 512 == 0 and n_experts <= LANES
    assert depth == 1, "the final rmsnorm is fused into the single layer's MoE combine"

    lb_all = jnp.cumsum(jax.nn.softmax(hgrn_lb_logits.astype(F32), axis=0), axis=0)
    lane = jnp.arange(d_attn)
    head_mean = jnp.where((lane[:, None] // ATTN_HEAD_DIM) == (lane[None, :] // ATTN_HEAD_DIM),
                          1.0 / ATTN_HEAD_DIM, 0.0).astype(F32)
    x2 = x.reshape(n, d)
    l = 0
    dils = tuple(dil for _, dil in DILATED_PATTERNS)
    n_pat = len(dils)
    hq, hf, hi, hg, *attn = _in_proj(
        x2, norm1_g[l].reshape(1, d), w_in[l].astype(BF16),
        (BF16, F32, BF16, BF16), 3, dils, tile_m=512)
    to3 = lambda t: t.reshape(b, s, -1)
    o_a = _hgrn2(to3(hq), to3(hf), to3(hi), to3(hg), lb_all[l].reshape(1, d_hgrn),
                 hgrn_norm_g[l].reshape(1, d_hgrn), tile_s=512)
    obs, lses = [], []
    for p, dil in enumerate(dils):
        aq, ak, av = (attn[a * n_pat + p].reshape(b, s // dil, dil * d_attn) for a in range(3))
        o_p, lse_p = _dilated_attention(aq, ak, av, dil, group=2)
        obs.append(o_p.reshape(n // dil, dil * d_attn))
        lses.append(lse_p.reshape(n // dil, dil * d_attn))
    rw = jnp.zeros((d, LANES), F32).at[:, :n_experts].set(router_w[l])
    rw_hi = rw.astype(BF16)
    rw_lo = (rw - rw_hi.astype(F32)).astype(BF16)
    rb = jnp.full((1, LANES), -jnp.inf, F32).at[0, :n_experts].set(router_b[l])
    w_o = w_out[l].astype(BF16)
    x1, h2, logits = _out_proj(o_a.reshape(n, d_hgrn), obs, lses, dils, x2, attn_norm_g[l].reshape(1, d_attn),
                               head_mean.astype(BF16), w_o[:d_hgrn], w_o[d_hgrn:], norm2_g[l].reshape(1, d),
                               rw_hi, rw_lo, rb, tile_m=256)
    tile_t = 256
    gates, pos, post, start, nch, nfull, off, counts = _route(logits, tile_t)

    counts = counts[0, :n_experts]
    padded = (counts + MOE_ROWS - 1) // MOE_ROWS * MOE_ROWS
    pad_end = jnp.cumsum(padded)
    pad_start = pad_end - padded
    n_rows = n * TOP_K + n_experts * MOE_ROWS
    nblk = n_rows // MOE_ROWS
    block_row = jnp.arange(nblk, dtype=jnp.int32) * MOE_ROWS
    block_e = jnp.minimum(jnp.sum(pad_end[None, :] <= block_row[:, None], axis=1), n_experts - 1).astype(jnp.int32)
    n_used = (pad_end[-1:] // MOE_ROWS).astype(jnp.int32)
    per_expert = lambda t: t[:, 0, :n_experts]
    seg = (per_expert(start) + pad_start[None, :].astype(jnp.int32)).reshape(-1)
    nch, nfull, off = per_expert(nch), per_expert(nfull), per_expert(off).reshape(-1)
    nwr = nfull.at[-1].set(nch[-1]).reshape(-1)
    nch = nch.reshape(-1)
    filled = (counts + MOE_CHUNK - 1) // MOE_CHUNK * MOE_CHUNK
    padrow = (pad_start + filled).astype(jnp.int32)
    padlen = (padded - filled).astype(jnp.int32)

    xbuf = _moe_scatter(seg, nch, nwr, off, padrow, padlen, n_used, post, h2, n_rows, tile_t, n_experts)
    ybuf = _moe_experts(block_e, n_used, xbuf, w_up[l], b_up[l][:, None, 0::2], b_up[l][:, None, 1::2],
                        w_down[l], b_down[l][:, None, :])
    out = _moe_combine(seg, nch, off, pos, gates, x1, final_norm_g.reshape(1, d), ybuf, tile_t, n_experts)
    return out.reshape(b, s, d)
```

```python
import functools

import jax
import jax.numpy as jnp
from jax import lax
from jax.experimental import pallas as pl
from jax.experimental.pallas import tpu as pltpu

F32 = jnp.float32
BF16 = jnp.bfloat16

NORM_EPS = 1e-6
HGRN_HEAD_DIM = 128
ATTN_HEAD_DIM = 64
DILATED_PATTERNS = ((128, 1), (512, 4), (2048, 16))
TOP_K = 4
SWIGLU_ALPHA = 1.702
SWIGLU_LIMIT = 7.0

LANES = 128
SUBLANES = 8
HGRN_CHUNK = 64
ATTN_BLOCK = 128
MOE_ROWS = 512
MOE_CHUNK = SUBLANES
VMEM_LIMIT = 56 * 1024 * 1024

_NT = (((1,), (1,)), ((), ()))
_TN = (((0,), (0,)), ((), ()))


def _cparams(sem):
    return pltpu.CompilerParams(dimension_semantics=sem, vmem_limit_bytes=VMEM_LIMIT)


def _to_classes(res, o_refs, dils, slab_refs):
    tm, width = res.shape
    n_slab = width // LANES
    for sb in range(n_slab):
        slab_refs[0][sb] = res[:, sb * LANES:(sb + 1) * LANES]
    prev = 1
    for p, (o_ref, dil) in enumerate(zip(o_refs, dils)):
        if dil == 1:
            o_ref[...] = res.astype(o_ref.dtype)
            continue
        ratio, src, last = dil // prev, slab_refs[0] if prev == 1 else slab_refs[1], p == len(dils) - 1
        assert dil % prev == 0 and (prev == 1 or last), "one intermediate slab: at most two strided levels"
        rows = tm // dil
        for rp in range(prev):
            for r2 in range(ratio):
                r = r2 * prev + rp
                for sb in range(n_slab):
                    part = src[sb, pl.ds(rp * (tm // prev) + r2, rows, stride=ratio), :]
                    col = r * width + sb * LANES
                    o_ref[:, col:col + LANES] = part.astype(o_ref.dtype)
                    if not last:
                        slab_refs[1][sb, r * rows:(r + 1) * rows, :] = part
        prev = dil


def _from_classes(ref, slab_ref, mid_ref, dil, ratio):
    if dil == 1:
        return ref[...].astype(F32)
    rows = ref.shape[0]
    tm, width = rows * dil, ref.shape[1] // dil
    n_slab = width // LANES
    assert dil in (ratio, ratio * ratio)
    prev = dil // ratio
    for rp in range(prev):
        for r2 in range(ratio):
            r = r2 * prev + rp
            for sb in range(n_slab):
                col = r * width + sb * LANES
                part = ref[:, col:col + LANES].astype(F32)
                if prev == 1:
                    slab_ref[sb, pl.ds(r2, rows, stride=ratio), :] = part
                else:
                    mid_ref[sb, pl.ds(rp * (tm // prev) + r2, rows, stride=ratio), :] = part
    if prev > 1:
        for rp in range(prev):
            for sb in range(n_slab):
                slab_ref[sb, pl.ds(rp, tm // prev, stride=prev), :] = mid_ref[sb, rp * (tm // prev):(rp + 1) * (tm // prev), :]
    return jnp.concatenate([slab_ref[sb] for sb in range(n_slab)], axis=-1)


def _in_proj_kernel(x_ref, g_ref, w_ref, *refs, plain_dtypes, n_attn, dils):
    out_refs, slab_refs = refs[:-2], refs[-2:]
    x = x_ref[...]
    r = lax.rsqrt(jnp.mean(x * x, axis=-1, keepdims=True) + NORM_EPS)
    h = ((x * r) * g_ref[...]).astype(BF16)
    width = out_refs[0].shape[-1]
    n_plain = len(plain_dtypes)
    for j in list(range(n_plain, n_plain + n_attn)) + list(range(n_plain)):
        res = jnp.dot(h, w_ref[:, j * width:(j + 1) * width], preferred_element_type=F32)
        if j < n_plain:
            out_refs[j][...] = res.astype(out_refs[j].dtype)
        else:
            first = n_plain + (j - n_plain) * len(dils)
            _to_classes(res, out_refs[first:first + len(dils)], dils, slab_refs)


def _in_proj(x2, gain, w_bf16, plain_dtypes, n_attn, dils, tile_m):
    n, d = x2.shape
    width = w_bf16.shape[1] // (len(plain_dtypes) + n_attn)
    out_specs = [pl.BlockSpec((tile_m, width), lambda i: (i, 0)) for _ in plain_dtypes]
    out_shape = [jax.ShapeDtypeStruct((n, width), dt) for dt in plain_dtypes]
    for _ in range(n_attn):
        for dil in dils:
            out_specs.append(pl.BlockSpec((tile_m // dil, dil * width), lambda i: (i, 0)))
            out_shape.append(jax.ShapeDtypeStruct((n // dil, dil * width), BF16))
    return pl.pallas_call(
        functools.partial(_in_proj_kernel, plain_dtypes=plain_dtypes, n_attn=n_attn, dils=dils),
        grid=(n // tile_m,),
        in_specs=[pl.BlockSpec((tile_m, d), lambda i: (i, 0)),
                  pl.BlockSpec((1, d), lambda i: (0, 0)),
                  pl.BlockSpec(w_bf16.shape, lambda i: (0, 0))],
        out_specs=out_specs,
        out_shape=out_shape,
        scratch_shapes=[pltpu.VMEM((width // LANES, tile_m, LANES), F32) for _ in range(2)],
        compiler_params=_cparams(("arbitrary",)),
        name="in_proj",
    )(x2, gain, w_bf16)


def _hgrn_kernel(q_ref, f_ref, i_ref, g_ref, lb_ref, gain_ref, o_ref, state_ref, *, n_heads):
    hd = HGRN_HEAD_DIM
    width = n_heads * hd
    nblk = HGRN_CHUNK // SUBLANES

    @pl.when(pl.program_id(1) == 0)
    def _():
        state_ref[...] = jnp.zeros_like(state_ref)

    lb = lb_ref[...]
    gain = gain_ref[...]
    row8 = lax.broadcasted_iota(jnp.int32, (SUBLANES, width), 0)
    rowc = lax.broadcasted_iota(jnp.int32, (SUBLANES, HGRN_CHUNK), 0)
    colc = lax.broadcasted_iota(jnp.int32, (SUBLANES, HGRN_CHUNK), 1)
    r64 = lax.broadcasted_iota(jnp.int32, (HGRN_CHUNK, HGRN_CHUNK), 0)
    c64 = lax.broadcasted_iota(jnp.int32, (HGRN_CHUNK, HGRN_CHUNK), 1)
    same32 = (r64 // 32) == (c64 // 32)
    same16 = (r64 // 16) == (c64 // 16)

    def scan8(x):
        for s in (1, 2, 4):
            x = x + jnp.where(row8 >= s, pltpu.roll(x, s, axis=0), 0.0)
        return x

    def cat(blocks):
        return jnp.concatenate(blocks, axis=0)

    def chunk(ci, carry):
        r0 = pl.multiple_of(ci * HGRN_CHUNK, HGRN_CHUNK)
        rows = pl.ds(r0, HGRN_CHUNK)
        q = q_ref[0, rows, :].astype(F32)
        v = i_ref[0, rows, :]
        f = lb + (1.0 - lb) * jax.nn.sigmoid(f_ref[0, rows, :])
        logf = jnp.log2(f)
        kk = 1.0 - f

        qb = [q[SUBLANES * i:SUBLANES * (i + 1)] for i in range(nblk)]
        kb = [kk[SUBLANES * i:SUBLANES * (i + 1)] for i in range(nblk)]
        b8 = [scan8(logf[SUBLANES * i:SUBLANES * (i + 1)]) for i in range(nblk)]
        t8 = [jnp.broadcast_to(x[SUBLANES - 1:SUBLANES], x.shape) for x in b8]
        b16 = [b8[i] + t8[i - 1] if i % 2 else b8[i] for i in range(nblk)]
        t16 = [t8[i - i % 2] + t8[i - i % 2 + 1] for i in range(nblk)]
        b32 = [b16[i] + t16[i - 2] if (i // 2) % 2 else b16[i] for i in range(nblk)]
        t32 = [t16[i - i % 4] + t16[i - i % 4 + 2] for i in range(nblk)]
        b64 = [b32[i] + t32[0] if i >= 4 else b32[i] for i in range(nblk)]
        t64 = t32[0] + t32[4]

        zero = jnp.zeros_like(qb[0])

        def q_side(bl, span):
            return cat([qb[i] * jnp.exp2(bl[i]) if (i * SUBLANES // span) % 2 else zero for i in range(nblk)])

        def k_side(bl, tl, span):
            return cat([zero if (i * SUBLANES // span) % 2 else kb[i] * jnp.exp2(tl[i] - bl[i]) for i in range(nblk)])

        q64 = cat([qb[i] * jnp.exp2(b64[i]) for i in range(nblk)]).astype(BF16)
        k64 = cat([kb[i] * jnp.exp2(t64 - b64[i]) for i in range(nblk)]).astype(BF16)
        q32, k32 = q_side(b32, 32).astype(BF16), k_side(b32, t32, 32).astype(BF16)
        q16, k16 = q_side(b16, 16).astype(BF16), k_side(b16, t16, 16).astype(BF16)
        q8, k8 = q_side(b8, 8).astype(BF16), k_side(b8, t8, 8).astype(BF16)
        decay = jnp.exp2(t64[0:1])

        diag = [[jnp.zeros((SUBLANES, HGRN_CHUNK), F32) for _ in range(nblk)] for _ in range(n_heads)]
        for j in range(SUBLANES):
            for i in range(nblk):
                if j == 0:
                    p = qb[i] * kb[i]
                else:
                    p = qb[i] * pltpu.roll(kb[i], j, axis=0) * jnp.exp2(b8[i] - pltpu.roll(b8[i], j, axis=0))
                hit = (colc == rowc + (SUBLANES * i - j)) & (rowc >= j)
                for h in range(n_heads):
                    w = jnp.sum(p[:, h * hd:(h + 1) * hd], axis=-1, keepdims=True)
                    diag[h][i] = diag[h][i] + jnp.where(hit, w, 0.0)

        outs = []
        for h in range(n_heads):
            sl = slice(h * hd, (h + 1) * hd)
            a = cat(diag[h])
            a = a + lax.dot_general(q32[:, sl], k32[:, sl], _NT, preferred_element_type=F32)
            a = a + jnp.where(same32, lax.dot_general(q16[:, sl], k16[:, sl], _NT, preferred_element_type=F32), 0.0)
            a = a + jnp.where(same16, lax.dot_general(q8[:, sl], k8[:, sl], _NT, preferred_element_type=F32), 0.0)
            st = state_ref[h]
            o = lax.dot_general(q64[:, sl], st.astype(BF16), _NT, preferred_element_type=F32)
            o = o + jnp.dot(a.astype(BF16), v[:, sl], preferred_element_type=F32)
            state_ref[h] = st * decay[:, sl] + lax.dot_general(v[:, sl], k64[:, sl], _TN,
                                                              preferred_element_type=F32)
            o = o * lax.rsqrt(jnp.mean(o * o, axis=-1, keepdims=True) + NORM_EPS)
            outs.append(o)
        o = jnp.concatenate(outs, axis=-1) * gain
        o_ref[0, rows, :] = (o * jax.nn.silu(g_ref[0, rows, :].astype(F32))).astype(o_ref.dtype)
        return carry

    lax.fori_loop(0, q_ref.shape[1] // HGRN_CHUNK, chunk, 0, unroll=2)


def _hgrn2(hq, hf, hi, hg, lb, gain, tile_s):
    b, s, width = hq.shape
    n_heads = width // HGRN_HEAD_DIM
    blk = pl.BlockSpec((1, tile_s, width), lambda bi, si: (bi, si, 0))
    vec = pl.BlockSpec((1, width), lambda bi, si: (0, 0))
    return pl.pallas_call(
        functools.partial(_hgrn_kernel, n_heads=n_heads),
        grid=(b, s // tile_s),
        in_specs=[blk, blk, blk, blk, vec, vec],
        out_specs=blk,
        out_shape=jax.ShapeDtypeStruct((b, s, width), BF16),
        scratch_shapes=[pltpu.VMEM((n_heads, HGRN_HEAD_DIM, HGRN_HEAD_DIM), F32)],
        compiler_params=_cparams(("arbitrary", "arbitrary")),
        name="hgrn2",
    )(hq, hf, hi, hg, lb, gain)


def _attn_kernel(qc_ref, kc_ref, kp_ref, vc_ref, vp_ref, o_ref, lse_ref, *, group, scale):
    blk = ATTN_BLOCK
    n_pairs = qc_ref.shape[-1] // LANES
    lane = lax.broadcasted_iota(jnp.int32, (1, LANES), 1)
    low = lane < ATTN_HEAD_DIM
    qi = lax.broadcasted_iota(jnp.int32, (blk, 2 * blk), 0)
    ki = lax.broadcasted_iota(jnp.int32, (blk, 2 * blk), 1)
    band = ((ki >= blk) & (ki - blk <= qi)) | ((ki < blk) & (ki >= qi))
    first_key = jnp.where(pl.program_id(2) == 0, blk, 0)
    band_first = band & (ki >= first_key)

    for g in range(group):
        rows = slice(g * blk, (g + 1) * blk)
        prev = slice((g - 1) * blk, g * blk)
        mask = band_first if g == 0 else band
        for pr in range(n_pairs):
            sl = slice(pr * LANES, (pr + 1) * LANES)
            q = qc_ref[0, rows, sl]
            if g == 0:
                k2 = jnp.concatenate([kp_ref[0, :, sl], kc_ref[0, rows, sl]], axis=0)
                v2 = jnp.concatenate([vp_ref[0, :, sl], vc_ref[0, rows, sl]], axis=0)
            else:
                k2 = jnp.concatenate([kc_ref[0, prev, sl], kc_ref[0, rows, sl]], axis=0)
                v2 = jnp.concatenate([vc_ref[0, prev, sl], vc_ref[0, rows, sl]], axis=0)
            acc = None
            lse = None
            for half in (low, jnp.logical_not(low)):
                qh = jnp.where(half, q, jnp.zeros_like(q))
                vh = jnp.where(half, v2, jnp.zeros_like(v2))
                s = lax.dot_general(qh, k2, _NT, preferred_element_type=F32) * scale
                s = jnp.where(mask, s, -jnp.inf)
                m = jnp.max(s, axis=-1, keepdims=True)
                p = jnp.exp(s - m)
                den = jnp.sum(p, axis=-1, keepdims=True)
                oh = jnp.dot(p.astype(BF16), vh, preferred_element_type=F32) / den
                lh = m + jnp.log(den)
                acc = oh if acc is None else acc + oh
                lse = jnp.broadcast_to(lh, (blk, LANES)) if lse is None else jnp.where(low, lse, lh)
            o_ref[0, rows, sl] = acc.astype(o_ref.dtype)
            lse_ref[0, rows, sl] = lse


def _dilated_attention(aq, ak, av, dil, group):
    b, m, width = aq.shape
    width //= dil
    nb = m // ATTN_BLOCK
    group = min(group, nb)
    tq = group * ATTN_BLOCK
    cur = pl.BlockSpec((1, tq, width), lambda bi, r, n: (bi, n, r))
    prv = pl.BlockSpec((1, ATTN_BLOCK, width), lambda bi, r, n: (bi, jnp.maximum(n * group - 1, 0), r))
    o, lse = pl.pallas_call(
        functools.partial(_attn_kernel, group=group, scale=ATTN_HEAD_DIM ** -0.5),
        grid=(b, dil, nb // group),
        in_specs=[cur, cur, prv, cur, prv],
        out_specs=[cur, cur],
        out_shape=[jax.ShapeDtypeStruct((b, m, dil * width), BF16),
                   jax.ShapeDtypeStruct((b, m, dil * width), F32)],
        compiler_params=_cparams(("arbitrary", "arbitrary", "arbitrary")),
        name=f"dilated_attn_d{dil}",
    )(aq, ak, ak, av, av)
    return o, lse


def _split_bf16(v):
    hi = v.astype(BF16)
    return hi, (v - hi.astype(F32)).astype(BF16)


def _out_proj_kernel(oa_ref, *refs, dils):
    n_pat = len(dils)
    o_refs, l_refs = refs[:n_pat], refs[n_pat:2 * n_pat]
    (x_ref, gb_ref, hm_ref, wa_ref, wb_ref, g2_ref, rwh_ref, rwl_ref, rb_ref,
     x1_ref, h2_ref, lg_ref) = refs[2 * n_pat:2 * n_pat + 12]
    slabs = refs[2 * n_pat + 12:]
    ratio = min(d for d in dils if d > 1)
    outs = [_from_classes(o_refs[p], slabs[4 * p], slabs[4 * p + 1], dils[p], ratio) for p in range(n_pat)]
    lses = [_from_classes(l_refs[p], slabs[4 * p + 2], slabs[4 * p + 3], dils[p], ratio) for p in range(n_pat)]
    mx = functools.reduce(jnp.maximum, lses)
    es = [jnp.exp(l - mx) for l in lses]
    ob = sum(e * o for e, o in zip(es, outs)) / sum(es)
    sq_hi, sq_lo = _split_bf16(ob * ob)
    ms = (jnp.dot(sq_hi, hm_ref[...], preferred_element_type=F32)
          + jnp.dot(sq_lo, hm_ref[...], preferred_element_type=F32))
    obn = (ob * lax.rsqrt(ms + NORM_EPS) * gb_ref[...]).astype(BF16)
    y = jnp.dot(oa_ref[...], wa_ref[...], preferred_element_type=F32)
    y = y + jnp.dot(obn, wb_ref[...], preferred_element_type=F32)
    x1 = x_ref[...] + y
    x1_ref[...] = x1
    h2 = (x1 * lax.rsqrt(jnp.mean(x1 * x1, axis=-1, keepdims=True) + NORM_EPS)) * g2_ref[...]
    h2_ref[...] = h2
    h_hi, h_lo = _split_bf16(h2)
    lg_ref[...] = (jnp.dot(h_hi, rwh_ref[...], preferred_element_type=F32)
                   + jnp.dot(h_hi, rwl_ref[...], preferred_element_type=F32)
                   + jnp.dot(h_lo, rwh_ref[...], preferred_element_type=F32)) + rb_ref[...]


def _out_proj(oa, obs, lses, dils, x2, gain_b, head_mean, w_a, w_b, gain2, rw_hi, rw_lo, rb, tile_m):
    n, d = x2.shape
    wm = oa.shape[1]
    row = lambda w: pl.BlockSpec((tile_m, w), lambda i: (i, 0))
    cls = [pl.BlockSpec((tile_m // dil, dil * wm), lambda i: (i, 0)) for dil in dils]
    full = lambda a: pl.BlockSpec(a.shape, lambda i: (0, 0))
    consts = (gain_b, head_mean, w_a, w_b, gain2, rw_hi, rw_lo, rb)
    return pl.pallas_call(
        functools.partial(_out_proj_kernel, dils=dils),
        grid=(n // tile_m,),
        in_specs=[row(wm)] + cls + cls + [row(d)] + [full(a) for a in consts],
        out_specs=[row(d), row(d), row(LANES)],
        out_shape=[jax.ShapeDtypeStruct((n, d), F32), jax.ShapeDtypeStruct((n, d), F32),
                   jax.ShapeDtypeStruct((n, LANES), F32)],
        scratch_shapes=[pltpu.VMEM((wm // LANES, tile_m, LANES), F32) for _ in range(4 * len(dils))],
        compiler_params=_cparams(("arbitrary",)),
        name="out_proj",
    )(oa, *obs, *lses, x2, *consts)


def _route_kernel(lg_ref, gate_ref, pos_ref, post_ref, start_ref, nch_ref, nfull_ref, off_ref, cnt_ref,
                  carry_ref):
    tr = lg_ref.shape[0]

    @pl.when(pl.program_id(0) == 0)
    def _():
        carry_ref[...] = jnp.zeros_like(carry_ref)

    logits = lg_ref[...]
    lane = lax.broadcasted_iota(jnp.int32, (tr, LANES), 1).astype(F32)
    vals, idxs = [], []
    for _ in range(TOP_K):
        m = jnp.max(logits, axis=-1, keepdims=True)
        ix = jnp.min(jnp.where(logits == m, lane, float(LANES)), axis=-1, keepdims=True)
        vals.append(m)
        idxs.append(ix)
        logits = jnp.where(lane == ix, -jnp.inf, logits)
    exps = [jnp.exp(v - vals[0]) for v in vals]
    den = exps[0] + exps[1] + exps[2] + exps[3]
    chosen = jnp.zeros((tr, LANES), F32)
    for ix in idxs:
        chosen = chosen + jnp.where(lane == ix, 1.0, 0.0)
    r = lax.broadcasted_iota(jnp.int32, (tr, tr), 0)
    c = lax.broadcasted_iota(jnp.int32, (tr, tr), 1)
    before = jnp.dot(jnp.where(r > c, 1.0, 0.0).astype(BF16), chosen.astype(BF16), preferred_element_type=F32)
    count = jnp.sum(chosen, axis=0, keepdims=True)
    carry = carry_ref[...]
    head = carry - MOE_CHUNK * jnp.floor(carry * (1.0 / MOE_CHUNK))
    nfull = jnp.floor((head + count) * (1.0 / MOE_CHUNK))
    nch = jnp.floor((head + count + (MOE_CHUNK - 1)) * (1.0 / MOE_CHUNK))
    el = lax.broadcasted_iota(jnp.int32, (LANES, LANES), 0)
    ec = lax.broadcasted_iota(jnp.int32, (LANES, LANES), 1)
    off = MOE_CHUNK * jnp.dot(jnp.broadcast_to(nch, (SUBLANES, LANES)).astype(BF16),
                              jnp.where(el < ec, 1.0, 0.0).astype(BF16), preferred_element_type=F32)[0:1]
    slot = before + off + head
    gate_o = jnp.zeros((tr, LANES), F32)
    pos_o = jnp.zeros((tr, LANES), F32)
    for k in range(TOP_K):
        pk = jnp.sum(jnp.where(lane == idxs[k], slot, 0.0), axis=-1, keepdims=True)
        gate_o = jnp.where(lane == float(k), exps[k] / den, gate_o)
        pos_o = jnp.where(lane == float(k), pk, pos_o)
    gate_ref[...] = gate_o
    pos_ref[...] = pos_o.astype(jnp.int32)
    post_ref[...] = pos_o.T[0:SUBLANES].astype(jnp.int32)
    start_ref[0] = (carry - head).astype(jnp.int32)
    nch_ref[0] = nch.astype(jnp.int32)
    nfull_ref[0] = nfull.astype(jnp.int32)
    off_ref[0] = off.astype(jnp.int32)
    carry_ref[...] = carry + count
    cnt_ref[...] = (carry + count).astype(jnp.int32)


def _route(logits, tile_t):
    n = logits.shape[0]
    nt = n // tile_t
    row = pl.BlockSpec((tile_t, LANES), lambda i: (i, 0))
    tab = pl.BlockSpec((1, 1, LANES), lambda i: (i, 0, 0))
    tab_shape = jax.ShapeDtypeStruct((nt, 1, LANES), jnp.int32)
    return pl.pallas_call(
        _route_kernel,
        grid=(nt,),
        in_specs=[row],
        out_specs=[row, row, pl.BlockSpec((SUBLANES, tile_t), lambda i: (i, 0)), tab, tab, tab, tab,
                   pl.BlockSpec((1, LANES), lambda i: (0, 0))],
        out_shape=[jax.ShapeDtypeStruct((n, LANES), F32), jax.ShapeDtypeStruct((n, LANES), jnp.int32),
                   jax.ShapeDtypeStruct((nt * SUBLANES, tile_t), jnp.int32), tab_shape, tab_shape, tab_shape,
                   tab_shape, jax.ShapeDtypeStruct((1, LANES), jnp.int32)],
        scratch_shapes=[pltpu.VMEM((1, LANES), F32)],
        compiler_params=_cparams(("arbitrary",)),
        name="route",
    )(logits)


def _for_each_chunk(tab_ref, n, fn):
    def one(c):
        fn(pl.multiple_of(c * MOE_CHUNK, MOE_CHUNK), pl.multiple_of(tab_ref[0, 0, c], MOE_CHUNK))

    def four(q, carry):
        for u in range(4):
            one(q * 4 + u)
        return carry

    def single(c, carry):
        one(c)
        return carry

    lax.fori_loop(0, lax.shift_right_logical(n, 2), four, 0)
    lax.fori_loop(n & ~3, n, single, 0)


def _pad_sizes():
    sizes, s = [], MOE_ROWS // 2
    while s >= MOE_CHUNK:
        sizes.append(s)
        s //= 2
    return sizes


def _scatter_kernel(nch_ref, nfull_ref, off_ref, total_ref, padrow_ref, padlen_ref, nused_ref, post_ref, tab_ref,
                    tabp_ref, h_ref, xout_ref, stage_ref, open_ref, zero_ref, sems, zsem, *, n_experts):
    rs, tt = stage_ref.shape[1], h_ref.shape[0]
    i, nt = pl.program_id(0), pl.num_programs(0)
    slot = i % 2
    base = i * n_experts
    stage = stage_ref.at[slot]

    @pl.when(i == 0)
    def _():
        open_ref[...] = jnp.zeros_like(open_ref)
        zero_ref[...] = jnp.zeros_like(zero_ref)

    srow = lax.broadcasted_iota(jnp.int32, (rs, tt), 0)
    post = post_ref[...]
    hit = srow == post[0:1]
    for k in range(1, TOP_K):
        hit = hit | (srow == post[k:k + 1])
    sel = jnp.where(hit, 1.0, 0.0).astype(BF16)
    stage[...] = jnp.dot(sel, h_ref[...].astype(BF16), preferred_element_type=F32)

    def add_open(e, carry):
        rows = pl.ds(pl.multiple_of(off_ref[base + e], MOE_CHUNK), MOE_CHUNK)
        stage[rows, :] = stage[rows, :] + open_ref[e]
        return carry

    def save_open(e, carry):
        nfull = nfull_ref[base + e]
        rows = pl.ds(pl.multiple_of(off_ref[base + e] + nfull * MOE_CHUNK, MOE_CHUNK), MOE_CHUNK)
        still_open = nch_ref[base + e] > nfull
        open_ref[e] = jnp.where(still_open, stage[rows, :], 0.0)
        return carry

    def chunk_copy(sl, srow0, xrow0):
        return pltpu.make_async_copy(stage_ref.at[sl, pl.ds(srow0, MOE_CHUNK)],
                                     xout_ref.at[pl.ds(xrow0, MOE_CHUNK)], sems.at[sl])

    def pad_copies(fn):
        def zeros_to(start, size):
            fn(pltpu.make_async_copy(zero_ref.at[pl.ds(0, size)], xout_ref.at[pl.ds(start, size)], zsem))

        def expert(e, carry):
            row, length = pl.multiple_of(padrow_ref[e], MOE_CHUNK), padlen_ref[e]
            for size in _pad_sizes():
                @pl.when((length & size) != 0)
                def _():
                    zeros_to(pl.multiple_of(row + (length & ~(2 * size - 1)), MOE_CHUNK), size)
            return carry

        def unused_half_block(c, carry):
            zeros_to(pl.multiple_of(c * (MOE_ROWS // 2), MOE_ROWS // 2), MOE_ROWS // 2)
            return carry

        lax.fori_loop(0, n_experts, expert, 0)
        lax.fori_loop(2 * nused_ref[0], xout_ref.shape[0] // (MOE_ROWS // 2), unused_half_block, 0)

    lax.fori_loop(0, n_experts, add_open, 0, unroll=8)

    @pl.when(i > 0)
    def _():
        _for_each_chunk(tabp_ref, total_ref[i - 1], lambda s0, x0: chunk_copy(1 - slot, s0, x0).wait())

    _for_each_chunk(tab_ref, total_ref[i], lambda s0, x0: chunk_copy(slot, s0, x0).start())
    lax.fori_loop(0, n_experts, save_open, 0, unroll=8)

    @pl.when(i == nt - 1)
    def _():
        pad_copies(lambda cp: cp.start())
        _for_each_chunk(tab_ref, total_ref[i], lambda s0, x0: chunk_copy(slot, s0, x0).wait())
        pad_copies(lambda cp: cp.wait())


def _stage_rows(tile_t, n_experts):
    return tile_t * TOP_K + (2 * n_experts + 1) * MOE_CHUNK


def _chunk_table_spec(table, shift):
    nt, _, width = table.shape
    return pl.BlockSpec((1, 1, width), lambda i, *_: (jnp.clip(i + shift, 0, nt - 1), 0, 0), memory_space=pltpu.SMEM)


def _moe_scatter(nch, nfull, off, total, padrow, padlen, n_used, post, table, h2, n_rows, tile_t, n_experts):
    n, d = h2.shape
    tab = lambda shift: _chunk_table_spec(table, shift)
    return pl.pallas_call(
        functools.partial(_scatter_kernel, n_experts=n_experts),
        grid_spec=pltpu.PrefetchScalarGridSpec(
            num_scalar_prefetch=7,
            grid=(n // tile_t,),
            in_specs=[pl.BlockSpec((SUBLANES, tile_t), lambda i, *_: (i, 0)), tab(0), tab(-1),
                      pl.BlockSpec((tile_t, d), lambda i, *_: (i, 0))],
            out_specs=pl.BlockSpec(memory_space=pl.ANY),
            scratch_shapes=[pltpu.VMEM((2, _stage_rows(tile_t, n_experts), d), F32),
                            pltpu.VMEM((n_experts, MOE_CHUNK, d), F32),
                            pltpu.VMEM((MOE_ROWS // 2, d), F32),
                            pltpu.SemaphoreType.DMA((2,)), pltpu.SemaphoreType.DMA]),
        out_shape=jax.ShapeDtypeStruct((n_rows, d), F32),
        compiler_params=_cparams(("arbitrary",)),
        name="moe_scatter",
    )(nch, nfull, off, total, padrow, padlen, n_used, post, table, table, h2)


def _expert_kernel(be_ref, nu_ref, x_ref, wu_ref, bg_ref, bl_ref, wd_ref, bd_ref, y_ref, wg_s, wl_s, wd_s):
    i = pl.program_id(0)
    used = i < nu_ref[0]
    new_expert = (i == 0) | (be_ref[i] != be_ref[jnp.maximum(i - 1, 0)])

    @pl.when(used & new_expert)
    def _():
        r = lax.broadcasted_iota(jnp.int32, (2 * LANES, LANES), 0)
        c = lax.broadcasted_iota(jnp.int32, (2 * LANES, LANES), 1)
        even = jnp.where(r == 2 * c, 1.0, 0.0).astype(BF16)
        odd = jnp.where(r == 2 * c + 1, 1.0, 0.0).astype(BF16)
        for cb in range(wg_s.shape[1] // LANES):
            grp = wu_ref[0, :, cb * 2 * LANES:(cb + 1) * 2 * LANES].astype(BF16)
            out = slice(cb * LANES, (cb + 1) * LANES)
            wg_s[:, out] = jnp.dot(grp, even, preferred_element_type=F32).astype(BF16)
            wl_s[:, out] = jnp.dot(grp, odd, preferred_element_type=F32).astype(BF16)
        wd_s[...] = wd_ref[0].astype(BF16)

    @pl.when(used)
    def _():
        x = x_ref[...].astype(BF16)
        glu = jnp.dot(x, wg_s[...], preferred_element_type=F32) + bg_ref[0]
        lin = jnp.dot(x, wl_s[...], preferred_element_type=F32) + bl_ref[0]
        glu = jnp.minimum(glu, SWIGLU_LIMIT)
        lin = jnp.clip(lin, -SWIGLU_LIMIT, SWIGLU_LIMIT)
        h = glu * jax.nn.sigmoid(SWIGLU_ALPHA * glu) * (lin + 1.0)
        y_ref[...] = jnp.dot(h.astype(BF16), wd_s[...], preferred_element_type=F32) + bd_ref[0]

    @pl.when(jnp.logical_not(used))
    def _():
        y_ref[...] = jnp.zeros_like(y_ref)


def _moe_experts(block_e, n_used, xbuf, w_up, bg, bl, w_down, bd):
    n_rows, d = xbuf.shape
    de = w_down.shape[1]
    nblk = n_rows // MOE_ROWS
    rows = pl.BlockSpec((MOE_ROWS, d), lambda i, be, nu: (i, 0))
    rows_in = pl.BlockSpec((MOE_ROWS, d), lambda i, be, nu: (jnp.minimum(i, jnp.maximum(nu[0] - 1, 0)), 0))
    wspec = lambda a: pl.BlockSpec((1,) + a.shape[1:], lambda i, be, nu: (be[i], 0, 0))
    return pl.pallas_call(
        _expert_kernel,
        grid_spec=pltpu.PrefetchScalarGridSpec(
            num_scalar_prefetch=2,
            grid=(nblk,),
            in_specs=[rows_in, wspec(w_up), wspec(bg), wspec(bl), wspec(w_down), wspec(bd)],
            out_specs=rows,
            scratch_shapes=[pltpu.VMEM((d, de), BF16), pltpu.VMEM((d, de), BF16), pltpu.VMEM((de, d), BF16)]),
        out_shape=jax.ShapeDtypeStruct((n_rows, d), F32),
        compiler_params=_cparams(("arbitrary",)),
        name="moe_experts",
    )(block_e, n_used, xbuf, w_up, bg, bl, w_down, bd)


def _combine_kernel(total_ref, pos_ref, gate_ref, x1_ref, g_ref, tab_ref, tabn_ref, y_ref, o_ref, stage_ref, sems):
    rs, tt = stage_ref.shape[1], x1_ref.shape[0]
    i, nt = pl.program_id(0), pl.num_programs(0)
    slot = i % 2

    def chunk_copy(sl, srow0, yrow0):
        return pltpu.make_async_copy(y_ref.at[pl.ds(yrow0, MOE_CHUNK)],
                                     stage_ref.at[sl, pl.ds(srow0, MOE_CHUNK)], sems.at[sl])

    @pl.when(i == 0)
    def _():
        stage_ref[...] = jnp.zeros_like(stage_ref)
        _for_each_chunk(tab_ref, total_ref[i], lambda s0, y0: chunk_copy(slot, s0, y0).start())

    @pl.when(i + 1 < nt)
    def _():
        _for_each_chunk(tabn_ref, total_ref[i + 1], lambda s0, y0: chunk_copy(1 - slot, s0, y0).start())

    scol = lax.broadcasted_iota(jnp.int32, (tt, rs), 1)
    pos, gates = pos_ref[...], gate_ref[...]
    weights = jnp.zeros((tt, rs), F32)
    for k in range(TOP_K):
        weights = jnp.where(scol == pos[:, k:k + 1], gates[:, k:k + 1], weights)
    _for_each_chunk(tab_ref, total_ref[i], lambda s0, y0: chunk_copy(slot, s0, y0).wait())
    acc = x1_ref[...] + jnp.dot(weights.astype(BF16), stage_ref[slot].astype(BF16), preferred_element_type=F32)
    o_ref[...] = (acc * lax.rsqrt(jnp.mean(acc * acc, axis=-1, keepdims=True) + NORM_EPS)) * g_ref[...]


def _moe_combine(total, pos, gates, x1, gain, table, ybuf, tile_t, n_experts):
    n, d = x1.shape
    return pl.pallas_call(
        _combine_kernel,
        grid_spec=pltpu.PrefetchScalarGridSpec(
            num_scalar_prefetch=1,
            grid=(n // tile_t,),
            in_specs=[pl.BlockSpec((tile_t, LANES), lambda i, *_: (i, 0)),
                      pl.BlockSpec((tile_t, LANES), lambda i, *_: (i, 0)),
                      pl.BlockSpec((tile_t, d), lambda i, *_: (i, 0)),
                      pl.BlockSpec((1, d), lambda i, *_: (0, 0)),
                      _chunk_table_spec(table, 0), _chunk_table_spec(table, 1),
                      pl.BlockSpec(memory_space=pl.ANY)],
            out_specs=pl.BlockSpec((tile_t, d), lambda i, *_: (i, 0)),
            scratch_shapes=[pltpu.VMEM((2, _stage_rows(tile_t, n_experts), d), F32),
                            pltpu.SemaphoreType.DMA((2,))]),
        out_shape=jax.ShapeDtypeStruct((n, d), F32),
        compiler_params=_cparams(("arbitrary",)),
        name="moe_combine",
    )(total, pos, gates, x1, gain, table, table, ybuf)


def kernel(x, norm1_g, w_in, hgrn_lb_logits, hgrn_norm_g, attn_norm_g, w_out, norm2_g, router_w, router_b,
           w_up, b_up, w_down, b_down, final_norm_g):
    b, s, d = x.shape
    n = b * s
    depth = w_in.shape[0]
    n_experts = router_w.shape[-1]
    d_mix = w_out.shape[1]
    d_hgrn = d_mix // 2
    d_attn = d_mix - d_hgrn
    assert w_in.shape[-1] == 4 * d_hgrn + 3 * d_attn and d_hgrn == d_attn
    assert all(s % win == 0 and win // dil == ATTN_BLOCK for win, dil in DILATED_PATTERNS)
    assert s % 512 == 0 and n % 512 == 0 and n_experts <= LANES
    assert depth == 1, "the final rmsnorm is fused into the single layer's MoE combine"

    lb_all = jnp.cumsum(jax.nn.softmax(hgrn_lb_logits.astype(F32), axis=0), axis=0)
    lane = jnp.arange(d_attn)
    head_mean = jnp.where((lane[:, None] // ATTN_HEAD_DIM) == (lane[None, :] // ATTN_HEAD_DIM),
                          1.0 / ATTN_HEAD_DIM, 0.0).astype(F32)
    x2 = x.reshape(n, d)
    l = 0
    dils = tuple(dil for _, dil in DILATED_PATTERNS)
    n_pat = len(dils)
    hq, hf, hi, hg, *attn = _in_proj(
        x2, norm1_g[l].reshape(1, d), w_in[l].astype(BF16),
        (BF16, F32, BF16, BF16), 3, dils, tile_m=512)
    to3 = lambda t: t.reshape(b, s, -1)
    o_a = _hgrn2(to3(hq), to3(hf), to3(hi), to3(hg), lb_all[l].reshape(1, d_hgrn),
                 hgrn_norm_g[l].reshape(1, d_hgrn), tile_s=512)
    obs, lses = [], []
    for p, dil in enumerate(dils):
        aq, ak, av = (attn[a * n_pat + p].reshape(b, s // dil, dil * d_attn) for a in range(3))
        o_p, lse_p = _dilated_attention(aq, ak, av, dil, group=2)
        obs.append(o_p.reshape(n // dil, dil * d_attn))
        lses.append(lse_p.reshape(n // dil, dil * d_attn))
    rw = jnp.zeros((d, LANES), F32).at[:, :n_experts].set(router_w[l])
    rw_hi = rw.astype(BF16)
    rw_lo = (rw - rw_hi.astype(F32)).astype(BF16)
    rb = jnp.full((1, LANES), -jnp.inf, F32).at[0, :n_experts].set(router_b[l])
    w_o = w_out[l].astype(BF16)
    x1, h2, logits = _out_proj(o_a.reshape(n, d_hgrn), obs, lses, dils, x2, attn_norm_g[l].reshape(1, d_attn),
                               head_mean.astype(BF16), w_o[:d_hgrn], w_o[d_hgrn:], norm2_g[l].reshape(1, d),
                               rw_hi, rw_lo, rb, tile_m=256)
    tile_t = 256
    gates, pos, post, start, nch, nfull, off, counts = _route(logits, tile_t)

    counts = counts[0, :n_experts]
    padded = (counts + MOE_ROWS - 1) // MOE_ROWS * MOE_ROWS
    pad_end = jnp.cumsum(padded)
    pad_start = pad_end - padded
    n_rows = n * TOP_K + n_experts * MOE_ROWS
    nblk = n_rows // MOE_ROWS
    block_row = jnp.arange(nblk, dtype=jnp.int32) * MOE_ROWS
    block_e = jnp.minimum(jnp.sum(pad_end[None, :] <= block_row[:, None], axis=1), n_experts - 1).astype(jnp.int32)
    n_used = (pad_end[-1:] // MOE_ROWS).astype(jnp.int32)
    per_expert = lambda t: t[:, 0, :n_experts]
    seg = per_expert(start) + pad_start[None, :].astype(jnp.int32)
    nch, nfull, off = per_expert(nch), per_expert(nfull), per_expert(off)
    total = jnp.sum(nch, axis=1).astype(jnp.int32)
    first = off // MOE_CHUNK
    max_chunks = pl.cdiv(_stage_rows(tile_t, n_experts) // MOE_CHUNK, LANES) * LANES
    chunk = jnp.arange(max_chunks, dtype=jnp.int32)[None, :, None]
    owned = (first[:, None, :] <= chunk) & (chunk < (first + nch)[:, None, :])
    table = jnp.sum(jnp.where(owned, (seg - first * MOE_CHUNK)[:, None, :], 0), axis=-1) + chunk[:, :, 0] * MOE_CHUNK
    table = table.astype(jnp.int32)[:, None, :]
    filled = (counts + MOE_CHUNK - 1) // MOE_CHUNK * MOE_CHUNK
    padrow = (pad_start + filled).astype(jnp.int32)
    padlen = (padded - filled).astype(jnp.int32)

    xbuf = _moe_scatter(nch.reshape(-1), nfull.reshape(-1), off.reshape(-1), total, padrow, padlen, n_used, post,
                        table, h2, n_rows, tile_t, n_experts)
    ybuf = _moe_experts(block_e, n_used, xbuf, w_up[l], b_up[l][:, None, 0::2], b_up[l][:, None, 1::2],
                        w_down[l], b_down[l][:, None, :])
    out = _moe_combine(total, pos, gates, x1, final_norm_g.reshape(1, d), table, ybuf, tile_t, n_experts)
    return out.reshape(b, s, d)
```

```python
import functools

import jax
import jax.numpy as jnp
from jax import lax
from jax.experimental import pallas as pl
from jax.experimental.pallas import tpu as pltpu

F32 = jnp.float32
BF16 = jnp.bfloat16

NORM_EPS = 1e-6
HGRN_HEAD_DIM = 128
ATTN_HEAD_DIM = 64
DILATED_PATTERNS = ((128, 1), (512, 4), (2048, 16))
TOP_K = 4
SWIGLU_ALPHA = 1.702
SWIGLU_LIMIT = 7.0

LANES = 128
SUBLANES = 8
HGRN_CHUNK = 64
ATTN_BLOCK = 128
MOE_ROWS = 512
MOE_CHUNK = SUBLANES
VMEM_LIMIT = 56 * 1024 * 1024

_NT = (((1,), (1,)), ((), ()))
_TN = (((0,), (0,)), ((), ()))


def _cparams(sem):
    return pltpu.CompilerParams(dimension_semantics=sem, vmem_limit_bytes=VMEM_LIMIT)


def _to_classes(res, o_refs, dils, slab_refs):
    tm, width = res.shape
    n_slab = width // LANES
    for sb in range(n_slab):
        slab_refs[0][sb] = res[:, sb * LANES:(sb + 1) * LANES]
    prev = 1
    for p, (o_ref, dil) in enumerate(zip(o_refs, dils)):
        if dil == 1:
            o_ref[...] = res.astype(o_ref.dtype)
            continue
        ratio, src, last = dil // prev, slab_refs[0] if prev == 1 else slab_refs[1], p == len(dils) - 1
        assert dil % prev == 0 and (prev == 1 or last), "one intermediate slab: at most two strided levels"
        rows = tm // dil
        for rp in range(prev):
            for r2 in range(ratio):
                r = r2 * prev + rp
                for sb in range(n_slab):
                    part = src[sb, pl.ds(rp * (tm // prev) + r2, rows, stride=ratio), :]
                    col = r * width + sb * LANES
                    o_ref[:, col:col + LANES] = part.astype(o_ref.dtype)
                    if not last:
                        slab_refs[1][sb, r * rows:(r + 1) * rows, :] = part
        prev = dil


def _from_classes(ref, slab_ref, mid_ref, dil, ratio):
    if dil == 1:
        return ref[...].astype(F32)
    rows = ref.shape[0]
    tm, width = rows * dil, ref.shape[1] // dil
    n_slab = width // LANES
    assert dil in (ratio, ratio * ratio)
    prev = dil // ratio
    for rp in range(prev):
        for r2 in range(ratio):
            r = r2 * prev + rp
            for sb in range(n_slab):
                col = r * width + sb * LANES
                part = ref[:, col:col + LANES].astype(F32)
                if prev == 1:
                    slab_ref[sb, pl.ds(r2, rows, stride=ratio), :] = part
                else:
                    mid_ref[sb, pl.ds(rp * (tm // prev) + r2, rows, stride=ratio), :] = part
    if prev > 1:
        for rp in range(prev):
            for sb in range(n_slab):
                slab_ref[sb, pl.ds(rp, tm // prev, stride=prev), :] = mid_ref[sb, rp * (tm // prev):(rp + 1) * (tm // prev), :]
    return jnp.concatenate([slab_ref[sb] for sb in range(n_slab)], axis=-1)


def _in_proj_kernel(x_ref, g_ref, w_ref, *refs, plain_dtypes, n_attn, dils):
    out_refs, slab_refs = refs[:-2], refs[-2:]
    x = x_ref[...]
    r = lax.rsqrt(jnp.mean(x * x, axis=-1, keepdims=True) + NORM_EPS)
    h = ((x * r) * g_ref[...]).astype(BF16)
    width = out_refs[0].shape[-1]
    n_plain = len(plain_dtypes)
    for j in list(range(n_plain, n_plain + n_attn)) + list(range(n_plain)):
        res = jnp.dot(h, w_ref[:, j * width:(j + 1) * width], preferred_element_type=F32)
        if j < n_plain:
            out_refs[j][...] = res.astype(out_refs[j].dtype)
        else:
            first = n_plain + (j - n_plain) * len(dils)
            _to_classes(res, out_refs[first:first + len(dils)], dils, slab_refs)


def _in_proj(x2, gain, w_bf16, plain_dtypes, n_attn, dils, tile_m):
    n, d = x2.shape
    width = w_bf16.shape[1] // (len(plain_dtypes) + n_attn)
    out_specs = [pl.BlockSpec((tile_m, width), lambda i: (i, 0)) for _ in plain_dtypes]
    out_shape = [jax.ShapeDtypeStruct((n, width), dt) for dt in plain_dtypes]
    for _ in range(n_attn):
        for dil in dils:
            out_specs.append(pl.BlockSpec((tile_m // dil, dil * width), lambda i: (i, 0)))
            out_shape.append(jax.ShapeDtypeStruct((n // dil, dil * width), BF16))
    return pl.pallas_call(
        functools.partial(_in_proj_kernel, plain_dtypes=plain_dtypes, n_attn=n_attn, dils=dils),
        grid=(n // tile_m,),
        in_specs=[pl.BlockSpec((tile_m, d), lambda i: (i, 0)),
                  pl.BlockSpec((1, d), lambda i: (0, 0)),
                  pl.BlockSpec(w_bf16.shape, lambda i: (0, 0))],
        out_specs=out_specs,
        out_shape=out_shape,
        scratch_shapes=[pltpu.VMEM((width // LANES, tile_m, LANES), F32) for _ in range(2)],
        compiler_params=_cparams(("arbitrary",)),
        name="in_proj",
    )(x2, gain, w_bf16)


def _hgrn_kernel(q_ref, f_ref, i_ref, g_ref, lb_ref, gain_ref, o_ref, state_ref, *, n_heads):
    hd = HGRN_HEAD_DIM
    width = n_heads * hd
    nblk = HGRN_CHUNK // SUBLANES

    @pl.when(pl.program_id(1) == 0)
    def _():
        state_ref[...] = jnp.zeros_like(state_ref)

    lb = lb_ref[...]
    gain = gain_ref[...]
    row8 = lax.broadcasted_iota(jnp.int32, (SUBLANES, width), 0)
    rowc = lax.broadcasted_iota(jnp.int32, (SUBLANES, HGRN_CHUNK), 0)
    colc = lax.broadcasted_iota(jnp.int32, (SUBLANES, HGRN_CHUNK), 1)
    r64 = lax.broadcasted_iota(jnp.int32, (HGRN_CHUNK, HGRN_CHUNK), 0)
    c64 = lax.broadcasted_iota(jnp.int32, (HGRN_CHUNK, HGRN_CHUNK), 1)
    same32 = (r64 // 32) == (c64 // 32)
    same16 = (r64 // 16) == (c64 // 16)

    def scan8(x):
        for s in (1, 2, 4):
            x = x + jnp.where(row8 >= s, pltpu.roll(x, s, axis=0), 0.0)
        return x

    def cat(blocks):
        return jnp.concatenate(blocks, axis=0)

    def chunk(ci, carry):
        r0 = pl.multiple_of(ci * HGRN_CHUNK, HGRN_CHUNK)
        rows = pl.ds(r0, HGRN_CHUNK)
        q = q_ref[0, rows, :].astype(F32)
        v = i_ref[0, rows, :]
        f = lb + (1.0 - lb) * jax.nn.sigmoid(f_ref[0, rows, :])
        logf = jnp.log2(f)
        kk = 1.0 - f

        qb = [q[SUBLANES * i:SUBLANES * (i + 1)] for i in range(nblk)]
        kb = [kk[SUBLANES * i:SUBLANES * (i + 1)] for i in range(nblk)]
        b8 = [scan8(logf[SUBLANES * i:SUBLANES * (i + 1)]) for i in range(nblk)]
        t8 = [jnp.broadcast_to(x[SUBLANES - 1:SUBLANES], x.shape) for x in b8]
        b16 = [b8[i] + t8[i - 1] if i % 2 else b8[i] for i in range(nblk)]
        t16 = [t8[i - i % 2] + t8[i - i % 2 + 1] for i in range(nblk)]
        b32 = [b16[i] + t16[i - 2] if (i // 2) % 2 else b16[i] for i in range(nblk)]
        t32 = [t16[i - i % 4] + t16[i - i % 4 + 2] for i in range(nblk)]
        b64 = [b32[i] + t32[0] if i >= 4 else b32[i] for i in range(nblk)]
        t64 = t32[0] + t32[4]

        zero = jnp.zeros_like(qb[0])

        def q_side(bl, span):
            return cat([qb[i] * jnp.exp2(bl[i]) if (i * SUBLANES // span) % 2 else zero for i in range(nblk)])

        def k_side(bl, tl, span):
            return cat([zero if (i * SUBLANES // span) % 2 else kb[i] * jnp.exp2(tl[i] - bl[i]) for i in range(nblk)])

        q64 = cat([qb[i] * jnp.exp2(b64[i]) for i in range(nblk)]).astype(BF16)
        k64 = cat([kb[i] * jnp.exp2(t64 - b64[i]) for i in range(nblk)]).astype(BF16)
        q32, k32 = q_side(b32, 32).astype(BF16), k_side(b32, t32, 32).astype(BF16)
        q16, k16 = q_side(b16, 16).astype(BF16), k_side(b16, t16, 16).astype(BF16)
        q8, k8 = q_side(b8, 8).astype(BF16), k_side(b8, t8, 8).astype(BF16)
        decay = jnp.exp2(t64[0:1])

        diag = [[jnp.zeros((SUBLANES, HGRN_CHUNK), F32) for _ in range(nblk)] for _ in range(n_heads)]
        for j in range(SUBLANES):
            for i in range(nblk):
                if j == 0:
                    p = qb[i] * kb[i]
                else:
                    p = qb[i] * pltpu.roll(kb[i], j, axis=0) * jnp.exp2(b8[i] - pltpu.roll(b8[i], j, axis=0))
                hit = (colc == rowc + (SUBLANES * i - j)) & (rowc >= j)
                for h in range(n_heads):
                    w = jnp.sum(p[:, h * hd:(h + 1) * hd], axis=-1, keepdims=True)
                    diag[h][i] = jnp.where(hit, w, diag[h][i])

        outs = []
        for h in range(n_heads):
            sl = slice(h * hd, (h + 1) * hd)
            a = cat(diag[h])
            a = a + lax.dot_general(q32[:, sl], k32[:, sl], _NT, preferred_element_type=F32)
            a = a + jnp.where(same32, lax.dot_general(q16[:, sl], k16[:, sl], _NT, preferred_element_type=F32), 0.0)
            a = a + jnp.where(same16, lax.dot_general(q8[:, sl], k8[:, sl], _NT, preferred_element_type=F32), 0.0)
            st = state_ref[h]
            o = lax.dot_general(q64[:, sl], st.astype(BF16), _NT, preferred_element_type=F32)
            o = o + jnp.dot(a.astype(BF16), v[:, sl], preferred_element_type=F32)
            state_ref[h] = st * decay[:, sl] + lax.dot_general(v[:, sl], k64[:, sl], _TN,
                                                              preferred_element_type=F32)
            o = o * lax.rsqrt(jnp.mean(o * o, axis=-1, keepdims=True) + NORM_EPS)
            outs.append(o)
        o = jnp.concatenate(outs, axis=-1) * gain
        o_ref[0, rows, :] = (o * jax.nn.silu(g_ref[0, rows, :].astype(F32))).astype(o_ref.dtype)
        return carry

    lax.fori_loop(0, q_ref.shape[1] // HGRN_CHUNK, chunk, 0, unroll=4)


def _hgrn2(hq, hf, hi, hg, lb, gain, tile_s):
    b, s, width = hq.shape
    n_heads = width // HGRN_HEAD_DIM
    blk = pl.BlockSpec((1, tile_s, width), lambda bi, si: (bi, si, 0))
    vec = pl.BlockSpec((1, width), lambda bi, si: (0, 0))
    return pl.pallas_call(
        functools.partial(_hgrn_kernel, n_heads=n_heads),
        grid=(b, s // tile_s),
        in_specs=[blk, blk, blk, blk, vec, vec],
        out_specs=blk,
        out_shape=jax.ShapeDtypeStruct((b, s, width), BF16),
        scratch_shapes=[pltpu.VMEM((n_heads, HGRN_HEAD_DIM, HGRN_HEAD_DIM), F32)],
        compiler_params=_cparams(("arbitrary", "arbitrary")),
        name="hgrn2",
    )(hq, hf, hi, hg, lb, gain)


def _attn_kernel(qc_ref, kc_ref, kp_ref, vc_ref, vp_ref, o_ref, lse_ref, *, group, scale):
    blk = ATTN_BLOCK
    n_pairs = qc_ref.shape[-1] // LANES
    lane = lax.broadcasted_iota(jnp.int32, (1, LANES), 1)
    low = lane < ATTN_HEAD_DIM
    qi = lax.broadcasted_iota(jnp.int32, (blk, 2 * blk), 0)
    ki = lax.broadcasted_iota(jnp.int32, (blk, 2 * blk), 1)
    band = ((ki >= blk) & (ki - blk <= qi)) | ((ki < blk) & (ki >= qi))
    first_key = jnp.where(pl.program_id(2) == 0, blk, 0)
    band_first = band & (ki >= first_key)

    for g in range(group):
        rows = slice(g * blk, (g + 1) * blk)
        prev = slice((g - 1) * blk, g * blk)
        mask = band_first if g == 0 else band
        for pr in range(n_pairs):
            sl = slice(pr * LANES, (pr + 1) * LANES)
            q = qc_ref[0, rows, sl]
            if g == 0:
                k2 = jnp.concatenate([kp_ref[0, :, sl], kc_ref[0, rows, sl]], axis=0)
                v2 = jnp.concatenate([vp_ref[0, :, sl], vc_ref[0, rows, sl]], axis=0)
            else:
                k2 = jnp.concatenate([kc_ref[0, prev, sl], kc_ref[0, rows, sl]], axis=0)
                v2 = jnp.concatenate([vc_ref[0, prev, sl], vc_ref[0, rows, sl]], axis=0)
            acc = None
            lse = None
            for half in (low, jnp.logical_not(low)):
                qh = jnp.where(half, q, jnp.zeros_like(q))
                vh = jnp.where(half, v2, jnp.zeros_like(v2))
                s = lax.dot_general(qh, k2, _NT, preferred_element_type=F32) * scale
                s = jnp.where(mask, s, -jnp.inf)
                m = jnp.max(s, axis=-1, keepdims=True)
                p = jnp.exp(s - m)
                den = jnp.sum(p, axis=-1, keepdims=True)
                oh = jnp.dot(p.astype(BF16), vh, preferred_element_type=F32) / den
                lh = m + jnp.log(den)
                acc = oh if acc is None else acc + oh
                lse = jnp.broadcast_to(lh, (blk, LANES)) if lse is None else jnp.where(low, lse, lh)
            o_ref[0, rows, sl] = acc.astype(o_ref.dtype)
            lse_ref[0, rows, sl] = lse


def _dilated_attention(aq, ak, av, dil, group):
    b, m, width = aq.shape
    width //= dil
    nb = m // ATTN_BLOCK
    group = min(group, nb)
    tq = group * ATTN_BLOCK
    cur = pl.BlockSpec((1, tq, width), lambda bi, r, n: (bi, n, r))
    prv = pl.BlockSpec((1, ATTN_BLOCK, width), lambda bi, r, n: (bi, jnp.maximum(n * group - 1, 0), r))
    o, lse = pl.pallas_call(
        functools.partial(_attn_kernel, group=group, scale=ATTN_HEAD_DIM ** -0.5),
        grid=(b, dil, nb // group),
        in_specs=[cur, cur, prv, cur, prv],
        out_specs=[cur, cur],
        out_shape=[jax.ShapeDtypeStruct((b, m, dil * width), BF16),
                   jax.ShapeDtypeStruct((b, m, dil * width), F32)],
        compiler_params=_cparams(("arbitrary", "arbitrary", "arbitrary")),
        name=f"dilated_attn_d{dil}",
    )(aq, ak, ak, av, av)
    return o, lse


def _split_bf16(v):
    hi = v.astype(BF16)
    return hi, (v - hi.astype(F32)).astype(BF16)


def _out_proj_kernel(oa_ref, *refs, dils):
    n_pat = len(dils)
    o_refs, l_refs = refs[:n_pat], refs[n_pat:2 * n_pat]
    (x_ref, gb_ref, hm_ref, wa_ref, wb_ref, g2_ref, rwh_ref, rwl_ref, rb_ref,
     x1_ref, h2_ref, lg_ref) = refs[2 * n_pat:2 * n_pat + 12]
    slabs = refs[2 * n_pat + 12:]
    ratio = min(d for d in dils if d > 1)
    outs = [_from_classes(o_refs[p], slabs[4 * p], slabs[4 * p + 1], dils[p], ratio) for p in range(n_pat)]
    lses = [_from_classes(l_refs[p], slabs[4 * p + 2], slabs[4 * p + 3], dils[p], ratio) for p in range(n_pat)]
    mx = functools.reduce(jnp.maximum, lses)
    es = [jnp.exp(l - mx) for l in lses]
    ob = sum(e * o for e, o in zip(es, outs)) / sum(es)
    sq_hi, sq_lo = _split_bf16(ob * ob)
    ms = (jnp.dot(sq_hi, hm_ref[...], preferred_element_type=F32)
          + jnp.dot(sq_lo, hm_ref[...], preferred_element_type=F32))
    obn = (ob * lax.rsqrt(ms + NORM_EPS) * gb_ref[...]).astype(BF16)
    y = jnp.dot(oa_ref[...], wa_ref[...], preferred_element_type=F32)
    y = y + jnp.dot(obn, wb_ref[...], preferred_element_type=F32)
    x1 = x_ref[...] + y
    x1_ref[...] = x1
    h2 = (x1 * lax.rsqrt(jnp.mean(x1 * x1, axis=-1, keepdims=True) + NORM_EPS)) * g2_ref[...]
    h2_ref[...] = h2
    h_hi, h_lo = _split_bf16(h2)
    lg = (jnp.dot(h_hi, rwh_ref[...], preferred_element_type=F32)
          + jnp.dot(h_hi, rwl_ref[...], preferred_element_type=F32)
          + jnp.dot(h_lo, rwh_ref[...], preferred_element_type=F32)) + rb_ref[...]
    lg_ref[...] = lg.T


def _out_proj(oa, obs, lses, dils, x2, gain_b, head_mean, w_a, w_b, gain2, rw_hi, rw_lo, rb, tile_m):
    n, d = x2.shape
    wm = oa.shape[1]
    row = lambda w: pl.BlockSpec((tile_m, w), lambda i: (i, 0))
    cls = [pl.BlockSpec((tile_m // dil, dil * wm), lambda i: (i, 0)) for dil in dils]
    full = lambda a: pl.BlockSpec(a.shape, lambda i: (0, 0))
    consts = (gain_b, head_mean, w_a, w_b, gain2, rw_hi, rw_lo, rb)
    return pl.pallas_call(
        functools.partial(_out_proj_kernel, dils=dils),
        grid=(n // tile_m,),
        in_specs=[row(wm)] + cls + cls + [row(d)] + [full(a) for a in consts],
        out_specs=[row(d), row(d), pl.BlockSpec((LANES, tile_m), lambda i: (0, i))],
        out_shape=[jax.ShapeDtypeStruct((n, d), F32), jax.ShapeDtypeStruct((n, d), F32),
                   jax.ShapeDtypeStruct((LANES, n), F32)],
        scratch_shapes=[pltpu.VMEM((wm // LANES, tile_m, LANES), F32) for _ in range(4 * len(dils))],
        compiler_params=_cparams(("arbitrary",)),
        name="out_proj",
    )(oa, *obs, *lses, x2, *consts)


def _route_kernel(lgt_ref, gate_ref, pos_ref, post_ref, start_ref, nch_ref, nfull_ref, off_ref, cnt_ref,
                  carry_ref):
    ne, tt = carry_ref.shape[0], lgt_ref.shape[1]

    @pl.when(pl.program_id(0) == 0)
    def _():
        carry_ref[...] = jnp.zeros_like(carry_ref)

    logits = lgt_ref[0:ne, :]
    erow = lax.broadcasted_iota(jnp.int32, (ne, tt), 0).astype(F32)
    vals, idxs = [], []
    for _ in range(TOP_K):
        m = jnp.max(logits, axis=0, keepdims=True)
        ix = jnp.min(jnp.where(logits == m, erow, float(ne)), axis=0, keepdims=True)
        vals.append(m)
        idxs.append(ix)
        logits = jnp.where(erow == ix, -jnp.inf, logits)
    exps = [jnp.exp(v - vals[0]) for v in vals]
    den = exps[0] + exps[1] + exps[2] + exps[3]
    chosen = jnp.zeros((ne, tt), F32)
    for ix in idxs:
        chosen = chosen + jnp.where(erow == ix, 1.0, 0.0)
    s = lax.broadcasted_iota(jnp.int32, (tt, tt), 0)
    t = lax.broadcasted_iota(jnp.int32, (tt, tt), 1)
    before = jnp.dot(chosen.astype(BF16), jnp.where(s < t, 1.0, 0.0).astype(BF16), preferred_element_type=F32)
    count = jnp.broadcast_to(jnp.sum(chosen, axis=1, keepdims=True), (ne, LANES))
    carry = carry_ref[...]
    head = carry - MOE_CHUNK * jnp.floor(carry * (1.0 / MOE_CHUNK))
    nfull = jnp.floor((head + count) * (1.0 / MOE_CHUNK))
    nch = jnp.floor((head + count + (MOE_CHUNK - 1)) * (1.0 / MOE_CHUNK))
    er = lax.broadcasted_iota(jnp.int32, (ne, ne), 0)
    ec = lax.broadcasted_iota(jnp.int32, (ne, ne), 1)
    off = MOE_CHUNK * jnp.dot(jnp.where(ec < er, 1.0, 0.0).astype(BF16), nch.astype(BF16),
                              preferred_element_type=F32)
    slot = before + jnp.concatenate([off + head] * (tt // LANES), axis=1)
    krow = lax.broadcasted_iota(jnp.int32, (LANES, tt), 0)
    gate_t = jnp.zeros((LANES, tt), F32)
    pos_t = jnp.zeros((LANES, tt), F32)
    for k in range(TOP_K):
        pk = jnp.sum(jnp.where(erow == idxs[k], slot, 0.0), axis=0, keepdims=True)
        gate_t = jnp.where(krow == k, exps[k] / den, gate_t)
        pos_t = jnp.where(krow == k, pk, pos_t)
    gate_ref[...] = gate_t.T
    pos_ref[...] = pos_t.T.astype(jnp.int32)
    post_ref[...] = pos_t[0:SUBLANES].astype(jnp.int32)
    start_ref[0] = (carry - head).astype(jnp.int32)
    nch_ref[0] = nch.astype(jnp.int32)
    nfull_ref[0] = nfull.astype(jnp.int32)
    off_ref[0] = off.astype(jnp.int32)
    carry_ref[...] = carry + count
    cnt_ref[...] = (carry + count).astype(jnp.int32)


def _route(logits_t, tile_t, n_experts):
    n = logits_t.shape[1]
    nt = n // tile_t
    ne = pl.cdiv(n_experts, SUBLANES) * SUBLANES
    row = pl.BlockSpec((tile_t, LANES), lambda i: (i, 0))
    tab = pl.BlockSpec((1, ne, LANES), lambda i: (i, 0, 0))
    tab_shape = jax.ShapeDtypeStruct((nt, ne, LANES), jnp.int32)
    return pl.pallas_call(
        _route_kernel,
        grid=(nt,),
        in_specs=[pl.BlockSpec((LANES, tile_t), lambda i: (0, i))],
        out_specs=[row, row, pl.BlockSpec((SUBLANES, tile_t), lambda i: (i, 0)), tab, tab, tab, tab,
                   pl.BlockSpec((ne, LANES), lambda i: (0, 0))],
        out_shape=[jax.ShapeDtypeStruct((n, LANES), F32), jax.ShapeDtypeStruct((n, LANES), jnp.int32),
                   jax.ShapeDtypeStruct((nt * SUBLANES, tile_t), jnp.int32), tab_shape, tab_shape, tab_shape,
                   tab_shape, jax.ShapeDtypeStruct((ne, LANES), jnp.int32)],
        scratch_shapes=[pltpu.VMEM((ne, LANES), F32)],
        compiler_params=_cparams(("arbitrary",)),
        name="route",
    )(logits_t)


def _for_each_chunk(tab_ref, n, fn):
    def one(c):
        fn(pl.multiple_of(c * MOE_CHUNK, MOE_CHUNK), pl.multiple_of(tab_ref[0, 0, c], MOE_CHUNK))

    def four(q, carry):
        for u in range(4):
            one(q * 4 + u)
        return carry

    def single(c, carry):
        one(c)
        return carry

    lax.fori_loop(0, lax.shift_right_logical(n, 2), four, 0)
    lax.fori_loop(n & ~3, n, single, 0)


def _pad_sizes():
    sizes, s = [], MOE_ROWS // 2
    while s >= MOE_CHUNK:
        sizes.append(s)
        s //= 2
    return sizes


def _scatter_kernel(nch_ref, nfull_ref, off_ref, total_ref, padrow_ref, padlen_ref, nused_ref, post_ref, tab_ref,
                    tabp_ref, h_ref, xout_ref, stage_ref, open_ref, zero_ref, sems, zsem, *, n_experts):
    rs, tt = stage_ref.shape[1], h_ref.shape[0]
    i, nt = pl.program_id(0), pl.num_programs(0)
    slot = i % 2
    base = i * n_experts
    stage = stage_ref.at[slot]

    @pl.when(i == 0)
    def _():
        open_ref[...] = jnp.zeros_like(open_ref)
        zero_ref[...] = jnp.zeros_like(zero_ref)

    srow = lax.broadcasted_iota(jnp.int32, (rs, tt), 0)
    post = post_ref[...]
    hit = srow == post[0:1]
    for k in range(1, TOP_K):
        hit = hit | (srow == post[k:k + 1])
    sel = jnp.where(hit, 1.0, 0.0).astype(BF16)
    stage[...] = jnp.dot(sel, h_ref[...].astype(BF16), preferred_element_type=F32)

    def add_open(e, carry):
        rows = pl.ds(pl.multiple_of(off_ref[base + e], MOE_CHUNK), MOE_CHUNK)
        stage[rows, :] = stage[rows, :] + open_ref[e]
        return carry

    def save_open(e, carry):
        nfull = nfull_ref[base + e]
        rows = pl.ds(pl.multiple_of(off_ref[base + e] + nfull * MOE_CHUNK, MOE_CHUNK), MOE_CHUNK)
        still_open = nch_ref[base + e] > nfull
        open_ref[e] = jnp.where(still_open, stage[rows, :], 0.0)
        return carry

    def chunk_copy(sl, srow0, xrow0):
        return pltpu.make_async_copy(stage_ref.at[sl, pl.ds(srow0, MOE_CHUNK)],
                                     xout_ref.at[pl.ds(xrow0, MOE_CHUNK)], sems.at[sl])

    def pad_copies(fn):
        def zeros_to(start, size):
            fn(pltpu.make_async_copy(zero_ref.at[pl.ds(0, size)], xout_ref.at[pl.ds(start, size)], zsem))

        def expert(e, carry):
            row, length = pl.multiple_of(padrow_ref[e], MOE_CHUNK), padlen_ref[e]
            for size in _pad_sizes():
                @pl.when((length & size) != 0)
                def _():
                    zeros_to(pl.multiple_of(row + (length & ~(2 * size - 1)), MOE_CHUNK), size)
            return carry

        def unused_half_block(c, carry):
            zeros_to(pl.multiple_of(c * (MOE_ROWS // 2), MOE_ROWS // 2), MOE_ROWS // 2)
            return carry

        lax.fori_loop(0, n_experts, expert, 0)
        lax.fori_loop(2 * nused_ref[0], xout_ref.shape[0] // (MOE_ROWS // 2), unused_half_block, 0)

    lax.fori_loop(0, n_experts, add_open, 0, unroll=8)

    @pl.when(i > 0)
    def _():
        _for_each_chunk(tabp_ref, total_ref[i - 1], lambda s0, x0: chunk_copy(1 - slot, s0, x0).wait())

    _for_each_chunk(tab_ref, total_ref[i], lambda s0, x0: chunk_copy(slot, s0, x0).start())
    lax.fori_loop(0, n_experts, save_open, 0, unroll=8)

    @pl.when(i == nt - 1)
    def _():
        pad_copies(lambda cp: cp.start())
        _for_each_chunk(tab_ref, total_ref[i], lambda s0, x0: chunk_copy(slot, s0, x0).wait())
        pad_copies(lambda cp: cp.wait())


def _stage_rows(tile_t, n_experts):
    return tile_t * TOP_K + (2 * n_experts + 1) * MOE_CHUNK


def _chunk_table_spec(table, shift):
    nt, _, width = table.shape
    return pl.BlockSpec((1, 1, width), lambda i, *_: (jnp.clip(i + shift, 0, nt - 1), 0, 0), memory_space=pltpu.SMEM)


def _moe_scatter(nch, nfull, off, total, padrow, padlen, n_used, post, table, h2, n_rows, tile_t, n_experts):
    n, d = h2.shape
    tab = lambda shift: _chunk_table_spec(table, shift)
    return pl.pallas_call(
        functools.partial(_scatter_kernel, n_experts=n_experts),
        grid_spec=pltpu.PrefetchScalarGridSpec(
            num_scalar_prefetch=7,
            grid=(n // tile_t,),
            in_specs=[pl.BlockSpec((SUBLANES, tile_t), lambda i, *_: (i, 0)), tab(0), tab(-1),
                      pl.BlockSpec((tile_t, d), lambda i, *_: (i, 0))],
            out_specs=pl.BlockSpec(memory_space=pl.ANY),
            scratch_shapes=[pltpu.VMEM((2, _stage_rows(tile_t, n_experts), d), F32),
                            pltpu.VMEM((n_experts, MOE_CHUNK, d), F32),
                            pltpu.VMEM((MOE_ROWS // 2, d), F32),
                            pltpu.SemaphoreType.DMA((2,)), pltpu.SemaphoreType.DMA]),
        out_shape=jax.ShapeDtypeStruct((n_rows, d), F32),
        compiler_params=_cparams(("arbitrary",)),
        name="moe_scatter",
    )(nch, nfull, off, total, padrow, padlen, n_used, post, table, table, h2)


def _expert_kernel(be_ref, nu_ref, x_ref, wu_ref, bg_ref, bl_ref, wd_ref, bd_ref, y_ref, wg_s, wl_s, wd_s):
    i = pl.program_id(0)
    used = i < nu_ref[0]
    new_expert = (i == 0) | (be_ref[i] != be_ref[jnp.maximum(i - 1, 0)])

    @pl.when(used & new_expert)
    def _():
        r = lax.broadcasted_iota(jnp.int32, (2 * LANES, 2 * LANES), 0)
        c = lax.broadcasted_iota(jnp.int32, (2 * LANES, 2 * LANES), 1)
        pick = jnp.where(r == jnp.where(c < LANES, 2 * c, 2 * (c - LANES) + 1), 1.0, 0.0).astype(BF16)
        for cb in range(wg_s.shape[1] // LANES):
            grp = wu_ref[0, :, cb * 2 * LANES:(cb + 1) * 2 * LANES].astype(BF16)
            out = slice(cb * LANES, (cb + 1) * LANES)
            both = jnp.dot(grp, pick, preferred_element_type=F32).astype(BF16)
            wg_s[:, out] = both[:, :LANES]
            wl_s[:, out] = both[:, LANES:]
        wd_s[...] = wd_ref[0].astype(BF16)

    @pl.when(used)
    def _():
        x = x_ref[...].astype(BF16)
        glu = jnp.dot(x, wg_s[...], preferred_element_type=F32) + bg_ref[0]
        lin = jnp.dot(x, wl_s[...], preferred_element_type=F32) + bl_ref[0]
        glu = jnp.minimum(glu, SWIGLU_LIMIT)
        lin = jnp.clip(lin, -SWIGLU_LIMIT, SWIGLU_LIMIT)
        h = glu * jax.nn.sigmoid(SWIGLU_ALPHA * glu) * (lin + 1.0)
        y_ref[...] = jnp.dot(h.astype(BF16), wd_s[...], preferred_element_type=F32) + bd_ref[0]

    @pl.when(jnp.logical_not(used))
    def _():
        y_ref[...] = jnp.zeros_like(y_ref)


def _moe_experts(block_e, n_used, xbuf, w_up, bg, bl, w_down, bd):
    n_rows, d = xbuf.shape
    de = w_down.shape[1]
    nblk = n_rows // MOE_ROWS
    rows = pl.BlockSpec((MOE_ROWS, d), lambda i, be, nu: (i, 0))
    rows_in = pl.BlockSpec((MOE_ROWS, d), lambda i, be, nu: (jnp.minimum(i, jnp.maximum(nu[0] - 1, 0)), 0))
    wspec = lambda a: pl.BlockSpec((1,) + a.shape[1:], lambda i, be, nu: (be[i], 0, 0))
    return pl.pallas_call(
        _expert_kernel,
        grid_spec=pltpu.PrefetchScalarGridSpec(
            num_scalar_prefetch=2,
            grid=(nblk,),
            in_specs=[rows_in, wspec(w_up), wspec(bg), wspec(bl), wspec(w_down), wspec(bd)],
            out_specs=rows,
            scratch_shapes=[pltpu.VMEM((d, de), BF16), pltpu.VMEM((d, de), BF16), pltpu.VMEM((de, d), BF16)]),
        out_shape=jax.ShapeDtypeStruct((n_rows, d), F32),
        compiler_params=_cparams(("arbitrary",)),
        name="moe_experts",
    )(block_e, n_used, xbuf, w_up, bg, bl, w_down, bd)


def _combine_kernel(total_ref, pos_ref, gate_ref, x1_ref, g_ref, tab_ref, tabn_ref, y_ref, o_ref, stage_ref, sems):
    rs, tt = stage_ref.shape[1], x1_ref.shape[0]
    i, nt = pl.program_id(0), pl.num_programs(0)
    slot = i % 2

    def chunk_copy(sl, srow0, yrow0):
        return pltpu.make_async_copy(y_ref.at[pl.ds(yrow0, MOE_CHUNK)],
                                     stage_ref.at[sl, pl.ds(srow0, MOE_CHUNK)], sems.at[sl])

    @pl.when(i == 0)
    def _():
        stage_ref[...] = jnp.zeros_like(stage_ref)
        _for_each_chunk(tab_ref, total_ref[i], lambda s0, y0: chunk_copy(slot, s0, y0).start())

    @pl.when(i + 1 < nt)
    def _():
        _for_each_chunk(tabn_ref, total_ref[i + 1], lambda s0, y0: chunk_copy(1 - slot, s0, y0).start())

    scol = lax.broadcasted_iota(jnp.int32, (tt, rs), 1)
    pos, gates = pos_ref[...], gate_ref[...]
    weights = jnp.zeros((tt, rs), F32)
    for k in range(TOP_K):
        weights = jnp.where(scol == pos[:, k:k + 1], gates[:, k:k + 1], weights)
    _for_each_chunk(tab_ref, total_ref[i], lambda s0, y0: chunk_copy(slot, s0, y0).wait())
    acc = x1_ref[...] + jnp.dot(weights.astype(BF16), stage_ref[slot].astype(BF16), preferred_element_type=F32)
    o_ref[...] = (acc * lax.rsqrt(jnp.mean(acc * acc, axis=-1, keepdims=True) + NORM_EPS)) * g_ref[...]


def _moe_combine(total, pos, gates, x1, gain, table, ybuf, tile_t, n_experts):
    n, d = x1.shape
    return pl.pallas_call(
        _combine_kernel,
        grid_spec=pltpu.PrefetchScalarGridSpec(
            num_scalar_prefetch=1,
            grid=(n // tile_t,),
            in_specs=[pl.BlockSpec((tile_t, LANES), lambda i, *_: (i, 0)),
                      pl.BlockSpec((tile_t, LANES), lambda i, *_: (i, 0)),
                      pl.BlockSpec((tile_t, d), lambda i, *_: (i, 0)),
                      pl.BlockSpec((1, d), lambda i, *_: (0, 0)),
                      _chunk_table_spec(table, 0), _chunk_table_spec(table, 1),
                      pl.BlockSpec(memory_space=pl.ANY)],
            out_specs=pl.BlockSpec((tile_t, d), lambda i, *_: (i, 0)),
            scratch_shapes=[pltpu.VMEM((2, _stage_rows(tile_t, n_experts), d), F32),
                            pltpu.SemaphoreType.DMA((2,))]),
        out_shape=jax.ShapeDtypeStruct((n, d), F32),
        compiler_params=_cparams(("arbitrary",)),
        name="moe_combine",
    )(total, pos, gates, x1, gain, table, table, ybuf)


def kernel(x, norm1_g, w_in, hgrn_lb_logits, hgrn_norm_g, attn_norm_g, w_out, norm2_g, router_w, router_b,
           w_up, b_up, w_down, b_down, final_norm_g):
    b, s, d = x.shape
    n = b * s
    depth = w_in.shape[0]
    n_experts = router_w.shape[-1]
    d_mix = w_out.shape[1]
    d_hgrn = d_mix // 2
    d_attn = d_mix - d_hgrn
    assert w_in.shape[-1] == 4 * d_hgrn + 3 * d_attn and d_hgrn == d_attn
    assert all(s % win == 0 and win // dil == ATTN_BLOCK for win, dil in DILATED_PATTERNS)
    assert s % 512 == 0 and n % 512 == 0 and n_experts <= LANES
    assert depth == 1, "the final rmsnorm is fused into the single layer's MoE combine"

    lb_all = jnp.cumsum(jax.nn.softmax(hgrn_lb_logits.astype(F32), axis=0), axis=0)
    lane = jnp.arange(d_attn)
    head_mean = jnp.where((lane[:, None] // ATTN_HEAD_DIM) == (lane[None, :] // ATTN_HEAD_DIM),
                          1.0 / ATTN_HEAD_DIM, 0.0).astype(F32)
    x2 = x.reshape(n, d)
    l = 0
    dils = tuple(dil for _, dil in DILATED_PATTERNS)
    n_pat = len(dils)
    hq, hf, hi, hg, *attn = _in_proj(
        x2, norm1_g[l].reshape(1, d), w_in[l].astype(BF16),
        (BF16, F32, BF16, BF16), 3, dils, tile_m=512)
    to3 = lambda t: t.reshape(b, s, -1)
    o_a = _hgrn2(to3(hq), to3(hf), to3(hi), to3(hg), lb_all[l].reshape(1, d_hgrn),
                 hgrn_norm_g[l].reshape(1, d_hgrn), tile_s=512)
    obs, lses = [], []
    for p, dil in enumerate(dils):
        aq, ak, av = (attn[a * n_pat + p].reshape(b, s // dil, dil * d_attn) for a in range(3))
        o_p, lse_p = _dilated_attention(aq, ak, av, dil, group=4)
        obs.append(o_p.reshape(n // dil, dil * d_attn))
        lses.append(lse_p.reshape(n // dil, dil * d_attn))
    rw = jnp.zeros((d, LANES), F32).at[:, :n_experts].set(router_w[l])
    rw_hi = rw.astype(BF16)
    rw_lo = (rw - rw_hi.astype(F32)).astype(BF16)
    rb = jnp.full((1, LANES), -jnp.inf, F32).at[0, :n_experts].set(router_b[l])
    w_o = w_out[l].astype(BF16)
    x1, h2, logits_t = _out_proj(o_a.reshape(n, d_hgrn), obs, lses, dils, x2, attn_norm_g[l].reshape(1, d_attn),
                                 head_mean.astype(BF16), w_o[:d_hgrn], w_o[d_hgrn:], norm2_g[l].reshape(1, d),
                                 rw_hi, rw_lo, rb, tile_m=256)
    tile_t = 256
    gates, pos, post, start, nch, nfull, off, counts = _route(logits_t, tile_t, n_experts)

    counts = counts[:n_experts, 0]
    padded = (counts + MOE_ROWS - 1) // MOE_ROWS * MOE_ROWS
    pad_end = jnp.cumsum(padded)
    pad_start = pad_end - padded
    n_rows = n * TOP_K + n_experts * MOE_ROWS
    nblk = n_rows // MOE_ROWS
    block_row = jnp.arange(nblk, dtype=jnp.int32) * MOE_ROWS
    block_e = jnp.minimum(jnp.sum(pad_end[None, :] <= block_row[:, None], axis=1), n_experts - 1).astype(jnp.int32)
    n_used = (pad_end[-1:] // MOE_ROWS).astype(jnp.int32)
    per_expert = lambda t: t[:, :n_experts, 0]
    seg = per_expert(start) + pad_start[None, :].astype(jnp.int32)
    nch, nfull, off = per_expert(nch), per_expert(nfull), per_expert(off)
    total = jnp.sum(nch, axis=1).astype(jnp.int32)
    first = off // MOE_CHUNK
    max_chunks = pl.cdiv(_stage_rows(tile_t, n_experts) // MOE_CHUNK, LANES) * LANES
    chunk = jnp.arange(max_chunks, dtype=jnp.int32)[None, :, None]
    owned = (first[:, None, :] <= chunk) & (chunk < (first + nch)[:, None, :])
    table = jnp.sum(jnp.where(owned, (seg - first * MOE_CHUNK)[:, None, :], 0), axis=-1) + chunk[:, :, 0] * MOE_CHUNK
    table = table.astype(jnp.int32)[:, None, :]
    filled = (counts + MOE_CHUNK - 1) // MOE_CHUNK * MOE_CHUNK
    padrow = (pad_start + filled).astype(jnp.int32)
    padlen = (padded - filled).astype(jnp.int32)

    xbuf = _moe_scatter(nch.reshape(-1), nfull.reshape(-1), off.reshape(-1), total, padrow, padlen, n_used, post,
                        table, h2, n_rows, tile_t, n_experts)
    ybuf = _moe_experts(block_e, n_used, xbuf, w_up[l], b_up[l][:, None, 0::2], b_up[l][:, None, 1::2],
                        w_down[l], b_down[l][:, None, :])
    out = _moe_combine(total, pos, gates, x1, final_norm_g.reshape(1, d), table, ybuf, tile_t, n_experts)
    return out.reshape(b, s, d)
```

```python
import functools

import jax
import jax.numpy as jnp
from jax import lax
from jax.experimental import pallas as pl
from jax.experimental.pallas import tpu as pltpu

F32 = jnp.float32
BF16 = jnp.bfloat16

NORM_EPS = 1e-6
HGRN_HEAD_DIM = 128
ATTN_HEAD_DIM = 64
DILATED_PATTERNS = ((128, 1), (512, 4), (2048, 16))
TOP_K = 4
SWIGLU_ALPHA = 1.702
SWIGLU_LIMIT = 7.0

LANES = 128
SUBLANES = 8
HGRN_CHUNK = 64
ATTN_BLOCK = 128
MOE_ROWS = 512
MOE_CHUNK = SUBLANES
VMEM_LIMIT = 56 * 1024 * 1024

_NT = (((1,), (1,)), ((), ()))
_TN = (((0,), (0,)), ((), ()))


def _cparams(sem):
    return pltpu.CompilerParams(dimension_semantics=sem, vmem_limit_bytes=VMEM_LIMIT)


def _to_classes(res, o_refs, dils, slab_refs):
    tm, width = res.shape
    n_slab = width // LANES
    for sb in range(n_slab):
        slab_refs[0][sb] = res[:, sb * LANES:(sb + 1) * LANES]
    prev = 1
    for p, (o_ref, dil) in enumerate(zip(o_refs, dils)):
        if dil == 1:
            o_ref[...] = res.astype(o_ref.dtype)
            continue
        ratio, src, last = dil // prev, slab_refs[0] if prev == 1 else slab_refs[1], p == len(dils) - 1
        assert dil % prev == 0 and (prev == 1 or last), "one intermediate slab: at most two strided levels"
        rows = tm // dil
        for rp in range(prev):
            for r2 in range(ratio):
                r = r2 * prev + rp
                for sb in range(n_slab):
                    part = src[sb, pl.ds(rp * (tm // prev) + r2, rows, stride=ratio), :]
                    col = r * width + sb * LANES
                    o_ref[:, col:col + LANES] = part.astype(o_ref.dtype)
                    if not last:
                        slab_refs[1][sb, r * rows:(r + 1) * rows, :] = part
        prev = dil


def _from_classes(ref, slab_ref, mid_ref, dil, ratio):
    if dil == 1:
        return ref[...].astype(F32)
    rows = ref.shape[0]
    tm, width = rows * dil, ref.shape[1] // dil
    n_slab = width // LANES
    assert dil in (ratio, ratio * ratio)
    prev = dil // ratio
    for rp in range(prev):
        for r2 in range(ratio):
            r = r2 * prev + rp
            for sb in range(n_slab):
                col = r * width + sb * LANES
                part = ref[:, col:col + LANES].astype(F32)
                if prev == 1:
                    slab_ref[sb, pl.ds(r2, rows, stride=ratio), :] = part
                else:
                    mid_ref[sb, pl.ds(rp * (tm // prev) + r2, rows, stride=ratio), :] = part
    if prev > 1:
        for rp in range(prev):
            for sb in range(n_slab):
                slab_ref[sb, pl.ds(rp, tm // prev, stride=prev), :] = mid_ref[sb, rp * (tm // prev):(rp + 1) * (tm // prev), :]
    return jnp.concatenate([slab_ref[sb] for sb in range(n_slab)], axis=-1)


def _in_proj_kernel(x_ref, g_ref, w_ref, *refs, plain_dtypes, n_attn, dils):
    out_refs, slab_refs = refs[:-2], refs[-2:]
    x = x_ref[...]
    r = lax.rsqrt(jnp.mean(x * x, axis=-1, keepdims=True) + NORM_EPS)
    h = ((x * r) * g_ref[...]).astype(BF16)
    width = out_refs[0].shape[-1]
    n_plain = len(plain_dtypes)
    for j in list(range(n_plain, n_plain + n_attn)) + list(range(n_plain)):
        res = jnp.dot(h, w_ref[:, j * width:(j + 1) * width], preferred_element_type=F32)
        if j < n_plain:
            out_refs[j][...] = res.astype(out_refs[j].dtype)
        else:
            first = n_plain + (j - n_plain) * len(dils)
            _to_classes(res, out_refs[first:first + len(dils)], dils, slab_refs)


def _in_proj(x2, gain, w_bf16, plain_dtypes, n_attn, dils, tile_m):
    n, d = x2.shape
    width = w_bf16.shape[1] // (len(plain_dtypes) + n_attn)
    out_specs = [pl.BlockSpec((tile_m, width), lambda i: (i, 0)) for _ in plain_dtypes]
    out_shape = [jax.ShapeDtypeStruct((n, width), dt) for dt in plain_dtypes]
    for _ in range(n_attn):
        for dil in dils:
            out_specs.append(pl.BlockSpec((tile_m // dil, dil * width), lambda i: (i, 0)))
            out_shape.append(jax.ShapeDtypeStruct((n // dil, dil * width), BF16))
    return pl.pallas_call(
        functools.partial(_in_proj_kernel, plain_dtypes=plain_dtypes, n_attn=n_attn, dils=dils),
        grid=(n // tile_m,),
        in_specs=[pl.BlockSpec((tile_m, d), lambda i: (i, 0)),
                  pl.BlockSpec((1, d), lambda i: (0, 0)),
                  pl.BlockSpec(w_bf16.shape, lambda i: (0, 0))],
        out_specs=out_specs,
        out_shape=out_shape,
        scratch_shapes=[pltpu.VMEM((width // LANES, tile_m, LANES), F32) for _ in range(2)],
        compiler_params=_cparams(("arbitrary",)),
        name="in_proj",
    )(x2, gain, w_bf16)


def _hgrn_kernel(q_ref, f_ref, i_ref, g_ref, lb_ref, gain_ref, o_ref, state_ref, *, n_heads):
    hd = HGRN_HEAD_DIM
    width = n_heads * hd
    nblk = HGRN_CHUNK // SUBLANES

    @pl.when(pl.program_id(1) == 0)
    def _():
        state_ref[...] = jnp.zeros_like(state_ref)

    lb = lb_ref[...]
    gain = gain_ref[...]
    row8 = lax.broadcasted_iota(jnp.int32, (SUBLANES, width), 0)
    rowc = lax.broadcasted_iota(jnp.int32, (SUBLANES, HGRN_CHUNK), 0)
    colc = lax.broadcasted_iota(jnp.int32, (SUBLANES, HGRN_CHUNK), 1)
    r64 = lax.broadcasted_iota(jnp.int32, (HGRN_CHUNK, HGRN_CHUNK), 0)
    c64 = lax.broadcasted_iota(jnp.int32, (HGRN_CHUNK, HGRN_CHUNK), 1)
    same32 = (r64 // 32) == (c64 // 32)
    same16 = (r64 // 16) == (c64 // 16)

    def scan8(x):
        for s in (1, 2, 4):
            x = x + jnp.where(row8 >= s, pltpu.roll(x, s, axis=0), 0.0)
        return x

    def cat(blocks):
        return jnp.concatenate(blocks, axis=0)

    def chunk(ci, carry):
        r0 = pl.multiple_of(ci * HGRN_CHUNK, HGRN_CHUNK)
        rows = pl.ds(r0, HGRN_CHUNK)
        q = q_ref[0, rows, :].astype(F32)
        v = i_ref[0, rows, :]
        f = lb + (1.0 - lb) * jax.nn.sigmoid(f_ref[0, rows, :])
        logf = jnp.log2(f)
        kk = 1.0 - f

        qb = [q[SUBLANES * i:SUBLANES * (i + 1)] for i in range(nblk)]
        kb = [kk[SUBLANES * i:SUBLANES * (i + 1)] for i in range(nblk)]
        b8 = [scan8(logf[SUBLANES * i:SUBLANES * (i + 1)]) for i in range(nblk)]
        t8 = [jnp.broadcast_to(x[SUBLANES - 1:SUBLANES], x.shape) for x in b8]
        b16 = [b8[i] + t8[i - 1] if i % 2 else b8[i] for i in range(nblk)]
        t16 = [t8[i - i % 2] + t8[i - i % 2 + 1] for i in range(nblk)]
        b32 = [b16[i] + t16[i - 2] if (i // 2) % 2 else b16[i] for i in range(nblk)]
        t32 = [t16[i - i % 4] + t16[i - i % 4 + 2] for i in range(nblk)]
        b64 = [b32[i] + t32[0] if i >= 4 else b32[i] for i in range(nblk)]
        t64 = t32[0] + t32[4]

        zero = jnp.zeros_like(qb[0])

        def q_side(bl, span):
            return cat([qb[i] * jnp.exp2(bl[i]) if (i * SUBLANES // span) % 2 else zero for i in range(nblk)])

        def k_side(bl, tl, span):
            return cat([zero if (i * SUBLANES // span) % 2 else kb[i] * jnp.exp2(tl[i] - bl[i]) for i in range(nblk)])

        q64 = cat([qb[i] * jnp.exp2(b64[i]) for i in range(nblk)]).astype(BF16)
        k64 = cat([kb[i] * jnp.exp2(t64 - b64[i]) for i in range(nblk)]).astype(BF16)
        q32, k32 = q_side(b32, 32).astype(BF16), k_side(b32, t32, 32).astype(BF16)
        q16, k16 = q_side(b16, 16).astype(BF16), k_side(b16, t16, 16).astype(BF16)
        q8, k8 = q_side(b8, 8).astype(BF16), k_side(b8, t8, 8).astype(BF16)
        decay = jnp.exp2(t64[0:1])

        diag = [[jnp.zeros((SUBLANES, HGRN_CHUNK), F32) for _ in range(nblk)] for _ in range(n_heads)]
        for j in range(SUBLANES):
            for i in range(nblk):
                if j == 0:
                    p = qb[i] * kb[i]
                else:
                    p = qb[i] * pltpu.roll(kb[i], j, axis=0) * jnp.exp2(b8[i] - pltpu.roll(b8[i], j, axis=0))
                hit = (colc == rowc + (SUBLANES * i - j)) & (rowc >= j)
                for h in range(n_heads):
                    w = jnp.sum(p[:, h * hd:(h + 1) * hd], axis=-1, keepdims=True)
                    diag[h][i] = jnp.where(hit, w, diag[h][i])

        outs = []
        for h in range(n_heads):
            sl = slice(h * hd, (h + 1) * hd)
            a = cat(diag[h])
            a = a + lax.dot_general(q32[:, sl], k32[:, sl], _NT, preferred_element_type=F32)
            a = a + jnp.where(same32, lax.dot_general(q16[:, sl], k16[:, sl], _NT, preferred_element_type=F32), 0.0)
            a = a + jnp.where(same16, lax.dot_general(q8[:, sl], k8[:, sl], _NT, preferred_element_type=F32), 0.0)
            st = state_ref[h]
            o = lax.dot_general(q64[:, sl], st.astype(BF16), _NT, preferred_element_type=F32)
            o = o + jnp.dot(a.astype(BF16), v[:, sl], preferred_element_type=F32)
            state_ref[h] = st * decay[:, sl] + lax.dot_general(v[:, sl], k64[:, sl], _TN,
                                                              preferred_element_type=F32)
            o = o * lax.rsqrt(jnp.mean(o * o, axis=-1, keepdims=True) + NORM_EPS)
            outs.append(o)
        o = jnp.concatenate(outs, axis=-1) * gain
        o_ref[0, rows, :] = (o * jax.nn.silu(g_ref[0, rows, :].astype(F32))).astype(o_ref.dtype)
        return carry

    lax.fori_loop(0, q_ref.shape[1] // HGRN_CHUNK, chunk, 0, unroll=4)


def _hgrn2(hq, hf, hi, hg, lb, gain, tile_s):
    b, s, width = hq.shape
    n_heads = width // HGRN_HEAD_DIM
    blk = pl.BlockSpec((1, tile_s, width), lambda bi, si: (bi, si, 0))
    vec = pl.BlockSpec((1, width), lambda bi, si: (0, 0))
    return pl.pallas_call(
        functools.partial(_hgrn_kernel, n_heads=n_heads),
        grid=(b, s // tile_s),
        in_specs=[blk, blk, blk, blk, vec, vec],
        out_specs=blk,
        out_shape=jax.ShapeDtypeStruct((b, s, width), BF16),
        scratch_shapes=[pltpu.VMEM((n_heads, HGRN_HEAD_DIM, HGRN_HEAD_DIM), F32)],
        compiler_params=_cparams(("arbitrary", "arbitrary")),
        name="hgrn2",
    )(hq, hf, hi, hg, lb, gain)


def _attn_kernel(qc_ref, kc_ref, kp_ref, vc_ref, vp_ref, o_ref, lse_ref, *, group, scale):
    blk = ATTN_BLOCK
    n_pairs = qc_ref.shape[-1] // LANES
    lane = lax.broadcasted_iota(jnp.int32, (1, LANES), 1)
    low = lane < ATTN_HEAD_DIM
    qi = lax.broadcasted_iota(jnp.int32, (blk, 2 * blk), 0)
    ki = lax.broadcasted_iota(jnp.int32, (blk, 2 * blk), 1)
    band = ((ki >= blk) & (ki - blk <= qi)) | ((ki < blk) & (ki >= qi))
    first_key = jnp.where(pl.program_id(2) == 0, blk, 0)
    band_first = band & (ki >= first_key)

    for g in range(group):
        rows = slice(g * blk, (g + 1) * blk)
        prev = slice((g - 1) * blk, g * blk)
        mask = band_first if g == 0 else band
        for pr in range(n_pairs):
            sl = slice(pr * LANES, (pr + 1) * LANES)
            q = qc_ref[0, rows, sl]
            if g == 0:
                k2 = jnp.concatenate([kp_ref[0, :, sl], kc_ref[0, rows, sl]], axis=0)
                v2 = jnp.concatenate([vp_ref[0, :, sl], vc_ref[0, rows, sl]], axis=0)
            else:
                k2 = jnp.concatenate([kc_ref[0, prev, sl], kc_ref[0, rows, sl]], axis=0)
                v2 = jnp.concatenate([vc_ref[0, prev, sl], vc_ref[0, rows, sl]], axis=0)
            acc = None
            lse = None
            for half in (low, jnp.logical_not(low)):
                qh = jnp.where(half, q, jnp.zeros_like(q))
                vh = jnp.where(half, v2, jnp.zeros_like(v2))
                s = lax.dot_general(qh, k2, _NT, preferred_element_type=F32) * scale
                s = jnp.where(mask, s, -jnp.inf)
                m = jnp.max(s, axis=-1, keepdims=True)
                p = jnp.exp(s - m)
                den = jnp.sum(p, axis=-1, keepdims=True)
                oh = jnp.dot(p.astype(BF16), vh, preferred_element_type=F32) / den
                lh = m + jnp.log(den)
                acc = oh if acc is None else acc + oh
                lse = jnp.broadcast_to(lh, (blk, LANES)) if lse is None else jnp.where(low, lse, lh)
            o_ref[0, rows, sl] = acc.astype(o_ref.dtype)
            lse_ref[0, rows, sl] = lse


def _dilated_attention(aq, ak, av, dil, group):
    b, m, width = aq.shape
    width //= dil
    nb = m // ATTN_BLOCK
    blocks = group
    group = min(blocks, nb)
    classes = min(dil, blocks // group)
    tq = group * ATTN_BLOCK
    cur = pl.BlockSpec((1, tq, classes * width), lambda bi, r, n: (bi, n, r))
    prv = pl.BlockSpec((1, ATTN_BLOCK, classes * width), lambda bi, r, n: (bi, jnp.maximum(n * group - 1, 0), r))
    o, lse = pl.pallas_call(
        functools.partial(_attn_kernel, group=group, scale=ATTN_HEAD_DIM ** -0.5),
        grid=(b, dil // classes, nb // group),
        in_specs=[cur, cur, prv, cur, prv],
        out_specs=[cur, cur],
        out_shape=[jax.ShapeDtypeStruct((b, m, dil * width), BF16),
                   jax.ShapeDtypeStruct((b, m, dil * width), F32)],
        compiler_params=_cparams(("arbitrary", "arbitrary", "arbitrary")),
        name=f"dilated_attn_d{dil}",
    )(aq, ak, ak, av, av)
    return o, lse


def _split_bf16(v):
    hi = v.astype(BF16)
    return hi, (v - hi.astype(F32)).astype(BF16)


def _out_proj_kernel(oa_ref, *refs, dils):
    n_pat = len(dils)
    o_refs, l_refs = refs[:n_pat], refs[n_pat:2 * n_pat]
    (x_ref, gb_ref, hm_ref, wa_ref, wb_ref, g2_ref, rwh_ref, rwl_ref, rb_ref,
     x1_ref, h2_ref, lg_ref) = refs[2 * n_pat:2 * n_pat + 12]
    slabs = refs[2 * n_pat + 12:]
    ratio = min(d for d in dils if d > 1)
    outs = [_from_classes(o_refs[p], slabs[4 * p], slabs[4 * p + 1], dils[p], ratio) for p in range(n_pat)]
    lses = [_from_classes(l_refs[p], slabs[4 * p + 2], slabs[4 * p + 3], dils[p], ratio) for p in range(n_pat)]
    mx = functools.reduce(jnp.maximum, lses)
    es = [jnp.exp(l - mx) for l in lses]
    ob = sum(e * o for e, o in zip(es, outs)) / sum(es)
    sq_hi, sq_lo = _split_bf16(ob * ob)
    ms = (jnp.dot(sq_hi, hm_ref[...], preferred_element_type=F32)
          + jnp.dot(sq_lo, hm_ref[...], preferred_element_type=F32))
    obn = (ob * lax.rsqrt(ms + NORM_EPS) * gb_ref[...]).astype(BF16)
    y = jnp.dot(oa_ref[...], wa_ref[...], preferred_element_type=F32)
    y = y + jnp.dot(obn, wb_ref[...], preferred_element_type=F32)
    x1 = x_ref[...] + y
    x1_ref[...] = x1
    h2 = (x1 * lax.rsqrt(jnp.mean(x1 * x1, axis=-1, keepdims=True) + NORM_EPS)) * g2_ref[...]
    h_hi, h_lo = _split_bf16(h2)
    h2_ref[...] = h_hi
    lg = (jnp.dot(h_hi, rwh_ref[...], preferred_element_type=F32)
          + jnp.dot(h_hi, rwl_ref[...], preferred_element_type=F32)
          + jnp.dot(h_lo, rwh_ref[...], preferred_element_type=F32)) + rb_ref[...]
    lg_ref[...] = lg.T


def _out_proj(oa, obs, lses, dils, x2, gain_b, head_mean, w_a, w_b, gain2, rw_hi, rw_lo, rb, tile_m):
    n, d = x2.shape
    wm = oa.shape[1]
    row = lambda w: pl.BlockSpec((tile_m, w), lambda i: (i, 0))
    cls = [pl.BlockSpec((tile_m // dil, dil * wm), lambda i: (i, 0)) for dil in dils]
    full = lambda a: pl.BlockSpec(a.shape, lambda i: (0, 0))
    consts = (gain_b, head_mean, w_a, w_b, gain2, rw_hi, rw_lo, rb)
    return pl.pallas_call(
        functools.partial(_out_proj_kernel, dils=dils),
        grid=(n // tile_m,),
        in_specs=[row(wm)] + cls + cls + [row(d)] + [full(a) for a in consts],
        out_specs=[row(d), row(d), pl.BlockSpec((LANES, tile_m), lambda i: (0, i))],
        out_shape=[jax.ShapeDtypeStruct((n, d), F32), jax.ShapeDtypeStruct((n, d), BF16),
                   jax.ShapeDtypeStruct((LANES, n), F32)],
        scratch_shapes=[pltpu.VMEM((wm // LANES, tile_m, LANES), F32) for _ in range(4 * len(dils))],
        compiler_params=_cparams(("arbitrary",)),
        name="out_proj",
    )(oa, *obs, *lses, x2, *consts)


def _route_kernel(lgt_ref, gate_ref, pos_ref, post_ref, start_ref, nch_ref, nfull_ref, off_ref, cnt_ref,
                  carry_ref):
    ne, tt = carry_ref.shape[0], lgt_ref.shape[1]

    @pl.when(pl.program_id(0) == 0)
    def _():
        carry_ref[...] = jnp.zeros_like(carry_ref)

    logits = lgt_ref[0:ne, :]
    erow = lax.broadcasted_iota(jnp.int32, (ne, tt), 0).astype(F32)
    vals, idxs = [], []
    for _ in range(TOP_K):
        m = jnp.max(logits, axis=0, keepdims=True)
        ix = jnp.min(jnp.where(logits == m, erow, float(ne)), axis=0, keepdims=True)
        vals.append(m)
        idxs.append(ix)
        logits = jnp.where(erow == ix, -jnp.inf, logits)
    exps = [jnp.exp(v - vals[0]) for v in vals]
    den = exps[0] + exps[1] + exps[2] + exps[3]
    chosen = jnp.zeros((ne, tt), F32)
    for ix in idxs:
        chosen = chosen + jnp.where(erow == ix, 1.0, 0.0)
    s = lax.broadcasted_iota(jnp.int32, (tt, tt), 0)
    t = lax.broadcasted_iota(jnp.int32, (tt, tt), 1)
    before = jnp.dot(chosen.astype(BF16), jnp.where(s < t, 1.0, 0.0).astype(BF16), preferred_element_type=F32)
    count = jnp.broadcast_to(jnp.sum(chosen, axis=1, keepdims=True), (ne, LANES))
    carry = carry_ref[...]
    head = carry - MOE_CHUNK * jnp.floor(carry * (1.0 / MOE_CHUNK))
    nfull = jnp.floor((head + count) * (1.0 / MOE_CHUNK))
    nch = jnp.floor((head + count + (MOE_CHUNK - 1)) * (1.0 / MOE_CHUNK))
    er = lax.broadcasted_iota(jnp.int32, (ne, ne), 0)
    ec = lax.broadcasted_iota(jnp.int32, (ne, ne), 1)
    off = MOE_CHUNK * jnp.dot(jnp.where(ec < er, 1.0, 0.0).astype(BF16), nch.astype(BF16),
                              preferred_element_type=F32)
    slot = before + jnp.concatenate([off + head] * (tt // LANES), axis=1)
    krow = lax.broadcasted_iota(jnp.int32, (LANES, tt), 0)
    gate_t = jnp.zeros((LANES, tt), F32)
    pos_t = jnp.zeros((LANES, tt), F32)
    for k in range(TOP_K):
        pk = jnp.sum(jnp.where(erow == idxs[k], slot, 0.0), axis=0, keepdims=True)
        gate_t = jnp.where(krow == k, exps[k] / den, gate_t)
        pos_t = jnp.where(krow == k, pk, pos_t)
    gate_ref[...] = gate_t.T
    pos_ref[...] = pos_t.T.astype(jnp.int32)
    post_ref[...] = pos_t[0:SUBLANES].astype(jnp.int32)
    start_ref[0] = (carry - head).astype(jnp.int32)
    nch_ref[0] = nch.astype(jnp.int32)
    nfull_ref[0] = nfull.astype(jnp.int32)
    off_ref[0] = off.astype(jnp.int32)
    carry_ref[...] = carry + count
    cnt_ref[...] = (carry + count).astype(jnp.int32)


def _route(logits_t, tile_t, n_experts):
    n = logits_t.shape[1]
    nt = n // tile_t
    ne = pl.cdiv(n_experts, SUBLANES) * SUBLANES
    row = pl.BlockSpec((tile_t, LANES), lambda i: (i, 0))
    tab = pl.BlockSpec((1, ne, LANES), lambda i: (i, 0, 0))
    tab_shape = jax.ShapeDtypeStruct((nt, ne, LANES), jnp.int32)
    return pl.pallas_call(
        _route_kernel,
        grid=(nt,),
        in_specs=[pl.BlockSpec((LANES, tile_t), lambda i: (0, i))],
        out_specs=[row, row, pl.BlockSpec((SUBLANES, tile_t), lambda i: (i, 0)), tab, tab, tab, tab,
                   pl.BlockSpec((ne, LANES), lambda i: (0, 0))],
        out_shape=[jax.ShapeDtypeStruct((n, LANES), F32), jax.ShapeDtypeStruct((n, LANES), jnp.int32),
                   jax.ShapeDtypeStruct((nt * SUBLANES, tile_t), jnp.int32), tab_shape, tab_shape, tab_shape,
                   tab_shape, jax.ShapeDtypeStruct((ne, LANES), jnp.int32)],
        scratch_shapes=[pltpu.VMEM((ne, LANES), F32)],
        compiler_params=_cparams(("arbitrary",)),
        name="route",
    )(logits_t)


def _for_each_chunk(tab_ref, n, fn):
    def one(c):
        fn(pl.multiple_of(c * MOE_CHUNK, MOE_CHUNK), pl.multiple_of(tab_ref[0, 0, c], MOE_CHUNK))

    def four(q, carry):
        for u in range(4):
            one(q * 4 + u)
        return carry

    def single(c, carry):
        one(c)
        return carry

    lax.fori_loop(0, lax.shift_right_logical(n, 2), four, 0)
    lax.fori_loop(n & ~3, n, single, 0)


def _pad_sizes():
    sizes, s = [], MOE_ROWS // 2
    while s >= MOE_CHUNK:
        sizes.append(s)
        s //= 2
    return sizes


def _scatter_kernel(nch_ref, nfull_ref, off_ref, total_ref, padrow_ref, padlen_ref, nused_ref, post_ref, tab_ref,
                    tabp_ref, h_ref, xout_ref, stage_ref, open_ref, zero_ref, sems, zsem, *, n_experts):
    rs, tt = stage_ref.shape[1], h_ref.shape[0]
    i, nt = pl.program_id(0), pl.num_programs(0)
    slot = i % 2
    base = i * n_experts
    stage = stage_ref.at[slot]

    @pl.when(i == 0)
    def _():
        open_ref[...] = jnp.zeros_like(open_ref)
        zero_ref[...] = jnp.zeros_like(zero_ref)

    srow = lax.broadcasted_iota(jnp.int32, (rs, tt), 0)
    post = post_ref[...]
    hit = srow == post[0:1]
    for k in range(1, TOP_K):
        hit = hit | (srow == post[k:k + 1])
    sel = jnp.where(hit, 1.0, 0.0).astype(BF16)
    stage[...] = jnp.dot(sel, h_ref[...], preferred_element_type=F32)

    def add_open(e, carry):
        rows = pl.ds(pl.multiple_of(off_ref[base + e], MOE_CHUNK), MOE_CHUNK)
        stage[rows, :] = stage[rows, :] + open_ref[e]
        return carry

    def save_open(e, carry):
        nfull = nfull_ref[base + e]
        rows = pl.ds(pl.multiple_of(off_ref[base + e] + nfull * MOE_CHUNK, MOE_CHUNK), MOE_CHUNK)
        still_open = nch_ref[base + e] > nfull
        open_ref[e] = jnp.where(still_open, stage[rows, :], 0.0)
        return carry

    def chunk_copy(sl, srow0, xrow0):
        return pltpu.make_async_copy(stage_ref.at[sl, pl.ds(srow0, MOE_CHUNK)],
                                     xout_ref.at[pl.ds(xrow0, MOE_CHUNK)], sems.at[sl])

    def pad_copies(fn):
        def zeros_to(start, size):
            fn(pltpu.make_async_copy(zero_ref.at[pl.ds(0, size)], xout_ref.at[pl.ds(start, size)], zsem))

        def expert(e, carry):
            row, length = pl.multiple_of(padrow_ref[e], MOE_CHUNK), padlen_ref[e]
            for size in _pad_sizes():
                @pl.when((length & size) != 0)
                def _():
                    zeros_to(pl.multiple_of(row + (length & ~(2 * size - 1)), MOE_CHUNK), size)
            return carry

        def unused_half_block(c, carry):
            zeros_to(pl.multiple_of(c * (MOE_ROWS // 2), MOE_ROWS // 2), MOE_ROWS // 2)
            return carry

        lax.fori_loop(0, n_experts, expert, 0)
        lax.fori_loop(2 * nused_ref[0], xout_ref.shape[0] // (MOE_ROWS // 2), unused_half_block, 0)

    lax.fori_loop(0, n_experts, add_open, 0, unroll=8)

    @pl.when(i > 0)
    def _():
        _for_each_chunk(tabp_ref, total_ref[i - 1], lambda s0, x0: chunk_copy(1 - slot, s0, x0).wait())

    _for_each_chunk(tab_ref, total_ref[i], lambda s0, x0: chunk_copy(slot, s0, x0).start())
    lax.fori_loop(0, n_experts, save_open, 0, unroll=8)

    @pl.when(i == nt - 1)
    def _():
        pad_copies(lambda cp: cp.start())
        _for_each_chunk(tab_ref, total_ref[i], lambda s0, x0: chunk_copy(slot, s0, x0).wait())
        pad_copies(lambda cp: cp.wait())


def _stage_rows(tile_t, n_experts):
    rows = tile_t * TOP_K + 2 * n_experts * (MOE_CHUNK - 1) + MOE_CHUNK
    return pl.cdiv(rows, LANES) * LANES


def _chunk_table_spec(table, shift):
    nt, _, width = table.shape
    return pl.BlockSpec((1, 1, width), lambda i, *_: (jnp.clip(i + shift, 0, nt - 1), 0, 0), memory_space=pltpu.SMEM)


def _moe_scatter(nch, nfull, off, total, padrow, padlen, n_used, post, table, h2, n_rows, tile_t, n_experts):
    n, d = h2.shape
    tab = lambda shift: _chunk_table_spec(table, shift)
    return pl.pallas_call(
        functools.partial(_scatter_kernel, n_experts=n_experts),
        grid_spec=pltpu.PrefetchScalarGridSpec(
            num_scalar_prefetch=7,
            grid=(n // tile_t,),
            in_specs=[pl.BlockSpec((SUBLANES, tile_t), lambda i, *_: (i, 0)), tab(0), tab(-1),
                      pl.BlockSpec((tile_t, d), lambda i, *_: (i, 0))],
            out_specs=pl.BlockSpec(memory_space=pl.ANY),
            scratch_shapes=[pltpu.VMEM((2, _stage_rows(tile_t, n_experts), d), F32),
                            pltpu.VMEM((n_experts, MOE_CHUNK, d), F32),
                            pltpu.VMEM((MOE_ROWS // 2, d), F32),
                            pltpu.SemaphoreType.DMA((2,)), pltpu.SemaphoreType.DMA]),
        out_shape=jax.ShapeDtypeStruct((n_rows, d), F32),
        compiler_params=_cparams(("arbitrary",)),
        name="moe_scatter",
    )(nch, nfull, off, total, padrow, padlen, n_used, post, table, table, h2)


def _expert_kernel(be_ref, nu_ref, slot_ref, next_ref, x_ref, bg_ref, bl_ref, bd_ref, wu_hbm, wd_hbm, y_ref,
                   wu_buf, wd_buf, wg_s, wl_s, wd_s, sems):
    i = pl.program_id(0)
    used = i < nu_ref[0]
    e = be_ref[i]
    new_expert = (i == 0) | (e != be_ref[jnp.maximum(i - 1, 0)])
    slot = slot_ref[e]

    def weight_copies(expert, sl):
        return (pltpu.make_async_copy(wu_hbm.at[expert], wu_buf.at[sl], sems.at[0, sl]),
                pltpu.make_async_copy(wd_hbm.at[expert], wd_buf.at[sl], sems.at[1, sl]))

    @pl.when(used & (i == 0))
    def _():
        for cp in weight_copies(e, slot):
            cp.start()

    @pl.when(used & new_expert)
    def _():
        for cp in weight_copies(e, slot):
            cp.wait()
        nxt = next_ref[e]

        @pl.when(nxt >= 0)
        def _():
            for cp in weight_copies(nxt, 1 - slot):
                cp.start()

        r = lax.broadcasted_iota(jnp.int32, (2 * LANES, 2 * LANES), 0)
        c = lax.broadcasted_iota(jnp.int32, (2 * LANES, 2 * LANES), 1)
        pick = jnp.where(r == jnp.where(c < LANES, 2 * c, 2 * (c - LANES) + 1), 1.0, 0.0).astype(BF16)
        for cb in range(wg_s.shape[1] // LANES):
            grp = wu_buf[slot, :, cb * 2 * LANES:(cb + 1) * 2 * LANES].astype(BF16)
            out = slice(cb * LANES, (cb + 1) * LANES)
            both = jnp.dot(grp, pick, preferred_element_type=F32).astype(BF16)
            wg_s[:, out] = both[:, :LANES]
            wl_s[:, out] = both[:, LANES:]
        wd_s[...] = wd_buf[slot].astype(BF16)

    @pl.when(used)
    def _():
        x = x_ref[...].astype(BF16)
        glu = jnp.dot(x, wg_s[...], preferred_element_type=F32) + bg_ref[0]
        lin = jnp.dot(x, wl_s[...], preferred_element_type=F32) + bl_ref[0]
        glu = jnp.minimum(glu, SWIGLU_LIMIT)
        lin = jnp.clip(lin, -SWIGLU_LIMIT, SWIGLU_LIMIT)
        h = glu * jax.nn.sigmoid(SWIGLU_ALPHA * glu) * (lin + 1.0)
        y_ref[...] = jnp.dot(h.astype(BF16), wd_s[...], preferred_element_type=F32) + bd_ref[0]

    @pl.when(jnp.logical_not(used))
    def _():
        y_ref[...] = jnp.zeros_like(y_ref)


def _moe_experts(block_e, n_used, buf_slot, next_expert, xbuf, w_up, bg, bl, w_down, bd):
    n_rows, d = xbuf.shape
    de = w_down.shape[1]
    nblk = n_rows // MOE_ROWS
    rows = pl.BlockSpec((MOE_ROWS, d), lambda i, be, nu, *_: (i, 0))
    rows_in = pl.BlockSpec((MOE_ROWS, d), lambda i, be, nu, *_: (jnp.minimum(i, jnp.maximum(nu[0] - 1, 0)), 0))
    bias = lambda a: pl.BlockSpec((1,) + a.shape[1:], lambda i, be, *_: (be[i], 0, 0))
    hbm = pl.BlockSpec(memory_space=pl.ANY)
    return pl.pallas_call(
        _expert_kernel,
        grid_spec=pltpu.PrefetchScalarGridSpec(
            num_scalar_prefetch=4,
            grid=(nblk,),
            in_specs=[rows_in, bias(bg), bias(bl), bias(bd), hbm, hbm],
            out_specs=rows,
            scratch_shapes=[pltpu.VMEM((2,) + w_up.shape[1:], F32), pltpu.VMEM((2,) + w_down.shape[1:], F32),
                            pltpu.VMEM((d, de), BF16), pltpu.VMEM((d, de), BF16), pltpu.VMEM((de, d), BF16),
                            pltpu.SemaphoreType.DMA((2, 2))]),
        out_shape=jax.ShapeDtypeStruct((n_rows, d), F32),
        compiler_params=_cparams(("arbitrary",)),
        name="moe_experts",
    )(block_e, n_used, buf_slot, next_expert, xbuf, bg, bl, bd, w_up, w_down)


def _combine_kernel(total_ref, pos_ref, gate_ref, x1_ref, g_ref, tab_ref, tabn_ref, y_ref, o_ref, stage_ref, sems):
    rs, tt = stage_ref.shape[1], x1_ref.shape[0]
    i, nt = pl.program_id(0), pl.num_programs(0)
    slot = i % 2

    def chunk_copy(sl, srow0, yrow0):
        return pltpu.make_async_copy(y_ref.at[pl.ds(yrow0, MOE_CHUNK)],
                                     stage_ref.at[sl, pl.ds(srow0, MOE_CHUNK)], sems.at[sl])

    @pl.when(i == 0)
    def _():
        stage_ref[...] = jnp.zeros_like(stage_ref)
        _for_each_chunk(tab_ref, total_ref[i], lambda s0, y0: chunk_copy(slot, s0, y0).start())

    @pl.when(i + 1 < nt)
    def _():
        _for_each_chunk(tabn_ref, total_ref[i + 1], lambda s0, y0: chunk_copy(1 - slot, s0, y0).start())

    scol = lax.broadcasted_iota(jnp.int32, (tt, rs), 1)
    pos, gates = pos_ref[...], gate_ref[...]
    weights = jnp.zeros((tt, rs), F32)
    for k in range(TOP_K):
        weights = jnp.where(scol == pos[:, k:k + 1], gates[:, k:k + 1], weights)
    _for_each_chunk(tab_ref, total_ref[i], lambda s0, y0: chunk_copy(slot, s0, y0).wait())
    acc = x1_ref[...] + jnp.dot(weights.astype(BF16), stage_ref[slot].astype(BF16), preferred_element_type=F32)
    o_ref[...] = (acc * lax.rsqrt(jnp.mean(acc * acc, axis=-1, keepdims=True) + NORM_EPS)) * g_ref[...]


def _moe_combine(total, pos, gates, x1, gain, table, ybuf, tile_t, n_experts):
    n, d = x1.shape
    return pl.pallas_call(
        _combine_kernel,
        grid_spec=pltpu.PrefetchScalarGridSpec(
            num_scalar_prefetch=1,
            grid=(n // tile_t,),
            in_specs=[pl.BlockSpec((tile_t, LANES), lambda i, *_: (i, 0)),
                      pl.BlockSpec((tile_t, LANES), lambda i, *_: (i, 0)),
                      pl.BlockSpec((tile_t, d), lambda i, *_: (i, 0)),
                      pl.BlockSpec((1, d), lambda i, *_: (0, 0)),
                      _chunk_table_spec(table, 0), _chunk_table_spec(table, 1),
                      pl.BlockSpec(memory_space=pl.ANY)],
            out_specs=pl.BlockSpec((tile_t, d), lambda i, *_: (i, 0)),
            scratch_shapes=[pltpu.VMEM((2, _stage_rows(tile_t, n_experts), d), F32),
                            pltpu.SemaphoreType.DMA((2,))]),
        out_shape=jax.ShapeDtypeStruct((n, d), F32),
        compiler_params=_cparams(("arbitrary",)),
        name="moe_combine",
    )(total, pos, gates, x1, gain, table, table, ybuf)


def kernel(x, norm1_g, w_in, hgrn_lb_logits, hgrn_norm_g, attn_norm_g, w_out, norm2_g, router_w, router_b,
           w_up, b_up, w_down, b_down, final_norm_g):
    b, s, d = x.shape
    n = b * s
    depth = w_in.shape[0]
    n_experts = router_w.shape[-1]
    d_mix = w_out.shape[1]
    d_hgrn = d_mix // 2
    d_attn = d_mix - d_hgrn
    assert w_in.shape[-1] == 4 * d_hgrn + 3 * d_attn and d_hgrn == d_attn
    assert all(s % win == 0 and win // dil == ATTN_BLOCK for win, dil in DILATED_PATTERNS)
    assert s % 512 == 0 and n % 512 == 0 and n_experts <= LANES
    assert depth == 1, "the final rmsnorm is fused into the single layer's MoE combine"

    lb_all = jnp.cumsum(jax.nn.softmax(hgrn_lb_logits.astype(F32), axis=0), axis=0)
    lane = jnp.arange(d_attn)
    head_mean = jnp.where((lane[:, None] // ATTN_HEAD_DIM) == (lane[None, :] // ATTN_HEAD_DIM),
                          1.0 / ATTN_HEAD_DIM, 0.0).astype(BF16)
    x2 = x.reshape(n, d)
    l = 0
    dils = tuple(dil for _, dil in DILATED_PATTERNS)
    n_pat = len(dils)
    hq, hf, hi, hg, *attn = _in_proj(
        x2, norm1_g[l].reshape(1, d), w_in[l].astype(BF16),
        (BF16, F32, BF16, BF16), 3, dils, tile_m=512)
    to3 = lambda t: t.reshape(b, s, -1)
    o_a = _hgrn2(to3(hq), to3(hf), to3(hi), to3(hg), lb_all[l].reshape(1, d_hgrn),
                 hgrn_norm_g[l].reshape(1, d_hgrn), tile_s=512)
    obs, lses = [], []
    for p, dil in enumerate(dils):
        aq, ak, av = (attn[a * n_pat + p].reshape(b, s // dil, dil * d_attn) for a in range(3))
        o_p, lse_p = _dilated_attention(aq, ak, av, dil, group=4)
        obs.append(o_p.reshape(n // dil, dil * d_attn))
        lses.append(lse_p.reshape(n // dil, dil * d_attn))
    rw = jnp.zeros((d, LANES), F32).at[:, :n_experts].set(router_w[l])
    rw_hi = rw.astype(BF16)
    rw_lo = (rw - rw_hi.astype(F32)).astype(BF16)
    rb = jnp.full((1, LANES), -jnp.inf, F32).at[0, :n_experts].set(router_b[l])
    w_o = w_out[l].astype(BF16)
    x1, h2, logits_t = _out_proj(o_a.reshape(n, d_hgrn), obs, lses, dils, x2, attn_norm_g[l].reshape(1, d_attn),
                                 head_mean, w_o[:d_hgrn], w_o[d_hgrn:], norm2_g[l].reshape(1, d),
                                 rw_hi, rw_lo, rb, tile_m=256)
    tile_t = 256
    gates, pos, post, start, nch, nfull, off, counts = _route(logits_t, tile_t, n_experts)

    counts = counts[:n_experts, 0]
    padded = (counts + MOE_ROWS - 1) // MOE_ROWS * MOE_ROWS
    pad_end = jnp.cumsum(padded)
    pad_start = pad_end - padded
    n_rows = n * TOP_K + n_experts * MOE_ROWS
    nblk = n_rows // MOE_ROWS
    block_row = jnp.arange(nblk, dtype=jnp.int32) * MOE_ROWS
    block_e = jnp.minimum(jnp.sum(pad_end[None, :] <= block_row[:, None], axis=1), n_experts - 1).astype(jnp.int32)
    n_used = (pad_end[-1:] // MOE_ROWS).astype(jnp.int32)
    has_rows = padded > 0
    eid = jnp.arange(n_experts, dtype=jnp.int32)
    buf_slot = ((jnp.cumsum(has_rows) - 1) % 2).astype(jnp.int32)
    following = jnp.min(jnp.where(has_rows[None, :] & (eid[None, :] > eid[:, None]), eid[None, :], n_experts), axis=1)
    next_expert = jnp.where(following < n_experts, following, -1).astype(jnp.int32)
    per_expert = lambda t: t[:, :n_experts, 0]
    seg = per_expert(start) + pad_start[None, :].astype(jnp.int32)
    nch, nfull, off = per_expert(nch), per_expert(nfull), per_expert(off)
    total = jnp.sum(nch, axis=1).astype(jnp.int32)
    first = off // MOE_CHUNK
    max_chunks = pl.cdiv(_stage_rows(tile_t, n_experts) // MOE_CHUNK, LANES) * LANES
    chunk = jnp.arange(max_chunks, dtype=jnp.int32)[None, :, None]
    owned = (first[:, None, :] <= chunk) & (chunk < (first + nch)[:, None, :])
    table = jnp.sum(jnp.where(owned, (seg - first * MOE_CHUNK)[:, None, :], 0), axis=-1) + chunk[:, :, 0] * MOE_CHUNK
    table = table.astype(jnp.int32)[:, None, :]
    filled = (counts + MOE_CHUNK - 1) // MOE_CHUNK * MOE_CHUNK
    padrow = (pad_start + filled).astype(jnp.int32)
    padlen = (padded - filled).astype(jnp.int32)

    xbuf = _moe_scatter(nch.reshape(-1), nfull.reshape(-1), off.reshape(-1), total, padrow, padlen, n_used, post,
                        table, h2, n_rows, tile_t, n_experts)
    ybuf = _moe_experts(block_e, n_used, buf_slot, next_expert, xbuf, w_up[l], b_up[l][:, None, 0::2],
                        b_up[l][:, None, 1::2], w_down[l], b_down[l][:, None, :])
    out = _moe_combine(total, pos, gates, x1, final_norm_g.reshape(1, d), table, ybuf, tile_t, n_experts)
    return out.reshape(b, s, d)
```

```python
import functools

import jax
import jax.numpy as jnp
from jax import lax
from jax.experimental import pallas as pl
from jax.experimental.pallas import tpu as pltpu

F32 = jnp.float32
BF16 = jnp.bfloat16

NORM_EPS = 1e-6
HGRN_HEAD_DIM = 128
ATTN_HEAD_DIM = 64
DILATED_PATTERNS = ((128, 1), (512, 4), (2048, 16))
TOP_K = 4
SWIGLU_ALPHA = 1.702
SWIGLU_LIMIT = 7.0

LANES = 128
SUBLANES = 8
HGRN_CHUNK = 64
ATTN_BLOCK = 128
MOE_ROWS = 512
MOE_CHUNK = SUBLANES
V7X_VMEM_BYTES = 64 * 1024 * 1024
VMEM_LIMIT = V7X_VMEM_BYTES - 8 * 1024 * 1024

PROJ_ROWS = 512
HGRN_ROWS = 1024
ATTN_BLOCKS = 4
ROUTE_TOKENS = 256

_NT = (((1,), (1,)), ((), ()))
_TN = (((0,), (0,)), ((), ()))


def _cparams(sem):
    return pltpu.CompilerParams(dimension_semantics=sem, vmem_limit_bytes=VMEM_LIMIT)


def _to_classes(res, o_refs, dils, slab_refs):
    tm, width = res.shape
    n_slab = width // LANES
    for sb in range(n_slab):
        slab_refs[0][sb] = res[:, sb * LANES:(sb + 1) * LANES]
    prev = 1
    for p, (o_ref, dil) in enumerate(zip(o_refs, dils)):
        if dil == 1:
            o_ref[...] = res.astype(o_ref.dtype)
            continue
        ratio, src, last = dil // prev, slab_refs[0] if prev == 1 else slab_refs[1], p == len(dils) - 1
        assert dil % prev == 0 and (prev == 1 or last), "one intermediate slab: at most two strided levels"
        rows = tm // dil
        for rp in range(prev):
            for r2 in range(ratio):
                r = r2 * prev + rp
                for sb in range(n_slab):
                    part = src[sb, pl.ds(rp * (tm // prev) + r2, rows, stride=ratio), :]
                    col = r * width + sb * LANES
                    o_ref[:, col:col + LANES] = part.astype(o_ref.dtype)
                    if not last:
                        slab_refs[1][sb, r * rows:(r + 1) * rows, :] = part
        prev = dil


def _from_classes(ref, slab_ref, mid_ref, dil, ratio):
    if dil == 1:
        return ref[...].astype(F32)
    rows = ref.shape[0]
    tm, width = rows * dil, ref.shape[1] // dil
    n_slab = width // LANES
    assert dil in (ratio, ratio * ratio)
    prev = dil // ratio
    for rp in range(prev):
        for r2 in range(ratio):
            r = r2 * prev + rp
            for sb in range(n_slab):
                col = r * width + sb * LANES
                part = ref[:, col:col + LANES].astype(F32)
                if prev == 1:
                    slab_ref[sb, pl.ds(r2, rows, stride=ratio), :] = part
                else:
                    mid_ref[sb, pl.ds(rp * (tm // prev) + r2, rows, stride=ratio), :] = part
    if prev > 1:
        for rp in range(prev):
            for sb in range(n_slab):
                slab_ref[sb, pl.ds(rp, tm // prev, stride=prev), :] = mid_ref[sb, rp * (tm // prev):(rp + 1) * (tm // prev), :]
    return jnp.concatenate([slab_ref[sb] for sb in range(n_slab)], axis=-1)


def _in_proj_kernel(x_ref, g_ref, w_ref, *refs, plain_dtypes, n_attn, dils):
    out_refs, slab_refs = refs[:-2], refs[-2:]
    x = x_ref[...]
    r = lax.rsqrt(jnp.mean(x * x, axis=-1, keepdims=True) + NORM_EPS)
    h = ((x * r) * g_ref[...]).astype(BF16)
    width = out_refs[0].shape[-1]
    n_plain = len(plain_dtypes)
    for j in list(range(n_plain, n_plain + n_attn)) + list(range(n_plain)):
        res = jnp.dot(h, w_ref[:, j * width:(j + 1) * width], preferred_element_type=F32)
        if j < n_plain:
            out_refs[j][...] = res.astype(out_refs[j].dtype)
        else:
            first = n_plain + (j - n_plain) * len(dils)
            _to_classes(res, out_refs[first:first + len(dils)], dils, slab_refs)


def _in_proj(x2, gain, w_bf16, plain_dtypes, n_attn, dils, tile_m):
    n, d = x2.shape
    width = w_bf16.shape[1] // (len(plain_dtypes) + n_attn)
    out_specs = [pl.BlockSpec((tile_m, width), lambda i: (i, 0)) for _ in plain_dtypes]
    out_shape = [jax.ShapeDtypeStruct((n, width), dt) for dt in plain_dtypes]
    for _ in range(n_attn):
        for dil in dils:
            out_specs.append(pl.BlockSpec((tile_m // dil, dil * width), lambda i: (i, 0)))
            out_shape.append(jax.ShapeDtypeStruct((n // dil, dil * width), BF16))
    return pl.pallas_call(
        functools.partial(_in_proj_kernel, plain_dtypes=plain_dtypes, n_attn=n_attn, dils=dils),
        grid=(n // tile_m,),
        in_specs=[pl.BlockSpec((tile_m, d), lambda i: (i, 0)),
                  pl.BlockSpec((1, d), lambda i: (0, 0)),
                  pl.BlockSpec(w_bf16.shape, lambda i: (0, 0))],
        out_specs=out_specs,
        out_shape=out_shape,
        scratch_shapes=[pltpu.VMEM((width // LANES, tile_m, LANES), F32) for _ in range(2)],
        compiler_params=_cparams(("arbitrary",)),
        name="in_proj",
    )(x2, gain, w_bf16)


def _hgrn_kernel(q_ref, f_ref, i_ref, g_ref, lb_ref, gain_ref, o_ref, state_ref, *, n_heads):
    hd = HGRN_HEAD_DIM
    width = n_heads * hd
    nblk = HGRN_CHUNK // SUBLANES

    @pl.when(pl.program_id(1) == 0)
    def _():
        state_ref[...] = jnp.zeros_like(state_ref)

    lb = lb_ref[...]
    gain = gain_ref[...]
    row8 = lax.broadcasted_iota(jnp.int32, (SUBLANES, width), 0)
    rowc = lax.broadcasted_iota(jnp.int32, (SUBLANES, HGRN_CHUNK), 0)
    colc = lax.broadcasted_iota(jnp.int32, (SUBLANES, HGRN_CHUNK), 1)
    r64 = lax.broadcasted_iota(jnp.int32, (HGRN_CHUNK, HGRN_CHUNK), 0)
    c64 = lax.broadcasted_iota(jnp.int32, (HGRN_CHUNK, HGRN_CHUNK), 1)
    same32 = (r64 // 32) == (c64 // 32)
    same16 = (r64 // 16) == (c64 // 16)

    def scan8(x):
        for s in (1, 2, 4):
            x = x + jnp.where(row8 >= s, pltpu.roll(x, s, axis=0), 0.0)
        return x

    def cat(blocks):
        return jnp.concatenate(blocks, axis=0)

    def chunk(ci, carry):
        r0 = pl.multiple_of(ci * HGRN_CHUNK, HGRN_CHUNK)
        rows = pl.ds(r0, HGRN_CHUNK)
        q = q_ref[0, rows, :].astype(F32)
        v = i_ref[0, rows, :]
        f = lb + (1.0 - lb) * jax.nn.sigmoid(f_ref[0, rows, :])
        logf = jnp.log2(f)
        kk = 1.0 - f

        qb = [q[SUBLANES * i:SUBLANES * (i + 1)] for i in range(nblk)]
        kb = [kk[SUBLANES * i:SUBLANES * (i + 1)] for i in range(nblk)]
        b8 = [scan8(logf[SUBLANES * i:SUBLANES * (i + 1)]) for i in range(nblk)]
        t8 = [jnp.broadcast_to(x[SUBLANES - 1:SUBLANES], x.shape) for x in b8]
        b16 = [b8[i] + t8[i - 1] if i % 2 else b8[i] for i in range(nblk)]
        t16 = [t8[i - i % 2] + t8[i - i % 2 + 1] for i in range(nblk)]
        b32 = [b16[i] + t16[i - 2] if (i // 2) % 2 else b16[i] for i in range(nblk)]
        t32 = [t16[i - i % 4] + t16[i - i % 4 + 2] for i in range(nblk)]
        b64 = [b32[i] + t32[0] if i >= 4 else b32[i] for i in range(nblk)]
        t64 = t32[0] + t32[4]

        zero = jnp.zeros_like(qb[0])

        def q_side(bl, span):
            return cat([qb[i] * jnp.exp2(bl[i]) if (i * SUBLANES // span) % 2 else zero for i in range(nblk)])

        def k_side(bl, tl, span):
            return cat([zero if (i * SUBLANES // span) % 2 else kb[i] * jnp.exp2(tl[i] - bl[i]) for i in range(nblk)])

        q64 = cat([qb[i] * jnp.exp2(b64[i]) for i in range(nblk)]).astype(BF16)
        k64 = cat([kb[i] * jnp.exp2(t64 - b64[i]) for i in range(nblk)]).astype(BF16)
        q32, k32 = q_side(b32, 32).astype(BF16), k_side(b32, t32, 32).astype(BF16)
        q16, k16 = q_side(b16, 16).astype(BF16), k_side(b16, t16, 16).astype(BF16)
        q8, k8 = q_side(b8, 8).astype(BF16), k_side(b8, t8, 8).astype(BF16)
        decay = jnp.exp2(t64[0:1])

        diag = [[jnp.zeros((SUBLANES, HGRN_CHUNK), F32) for _ in range(nblk)] for _ in range(n_heads)]
        for j in range(SUBLANES):
            for i in range(nblk):
                if j == 0:
                    p = qb[i] * kb[i]
                else:
                    p = qb[i] * pltpu.roll(kb[i], j, axis=0) * jnp.exp2(b8[i] - pltpu.roll(b8[i], j, axis=0))
                hit = (colc == rowc + (SUBLANES * i - j)) & (rowc >= j)
                for h in range(n_heads):
                    w = jnp.sum(p[:, h * hd:(h + 1) * hd], axis=-1, keepdims=True)
                    diag[h][i] = jnp.where(hit, w, diag[h][i])

        outs = []
        for h in range(n_heads):
            sl = slice(h * hd, (h + 1) * hd)
            a = cat(diag[h])
            a = a + lax.dot_general(q32[:, sl], k32[:, sl], _NT, preferred_element_type=F32)
            a = a + jnp.where(same32, lax.dot_general(q16[:, sl], k16[:, sl], _NT, preferred_element_type=F32), 0.0)
            a = a + jnp.where(same16, lax.dot_general(q8[:, sl], k8[:, sl], _NT, preferred_element_type=F32), 0.0)
            st = state_ref[h]
            o = lax.dot_general(q64[:, sl], st.astype(BF16), _NT, preferred_element_type=F32)
            o = o + jnp.dot(a.astype(BF16), v[:, sl], preferred_element_type=F32)
            state_ref[h] = st * decay[:, sl] + lax.dot_general(v[:, sl], k64[:, sl], _TN,
                                                              preferred_element_type=F32)
            o = o * lax.rsqrt(jnp.mean(o * o, axis=-1, keepdims=True) + NORM_EPS)
            outs.append(o)
        o = jnp.concatenate(outs, axis=-1) * gain
        o_ref[0, rows, :] = (o * jax.nn.silu(g_ref[0, rows, :].astype(F32))).astype(o_ref.dtype)
        return carry

    lax.fori_loop(0, q_ref.shape[1] // HGRN_CHUNK, chunk, 0, unroll=4)


def _hgrn2(hq, hf, hi, hg, lb, gain, tile_s):
    b, s, width = hq.shape
    n_heads = width // HGRN_HEAD_DIM
    blk = pl.BlockSpec((1, tile_s, width), lambda bi, si: (bi, si, 0))
    vec = pl.BlockSpec((1, width), lambda bi, si: (0, 0))
    return pl.pallas_call(
        functools.partial(_hgrn_kernel, n_heads=n_heads),
        grid=(b, s // tile_s),
        in_specs=[blk, blk, blk, blk, vec, vec],
        out_specs=blk,
        out_shape=jax.ShapeDtypeStruct((b, s, width), BF16),
        scratch_shapes=[pltpu.VMEM((n_heads, HGRN_HEAD_DIM, HGRN_HEAD_DIM), F32)],
        compiler_params=_cparams(("arbitrary", "arbitrary")),
        name="hgrn2",
    )(hq, hf, hi, hg, lb, gain)


def _attn_kernel(qc_ref, kc_ref, kp_ref, vc_ref, vp_ref, o_ref, lse_ref, *, group, scale):
    blk = ATTN_BLOCK
    n_pairs = qc_ref.shape[-1] // LANES
    lane = lax.broadcasted_iota(jnp.int32, (1, LANES), 1)
    low = lane < ATTN_HEAD_DIM
    qi = lax.broadcasted_iota(jnp.int32, (blk, 2 * blk), 0)
    ki = lax.broadcasted_iota(jnp.int32, (blk, 2 * blk), 1)
    band = ((ki >= blk) & (ki - blk <= qi)) | ((ki < blk) & (ki >= qi))
    first_key = jnp.where(pl.program_id(2) == 0, blk, 0)
    band_first = band & (ki >= first_key)

    for g in range(group):
        rows = slice(g * blk, (g + 1) * blk)
        prev = slice((g - 1) * blk, g * blk)
        mask = band_first if g == 0 else band
        for pr in range(n_pairs):
            sl = slice(pr * LANES, (pr + 1) * LANES)
            q = qc_ref[0, rows, sl]
            if g == 0:
                k2 = jnp.concatenate([kp_ref[0, :, sl], kc_ref[0, rows, sl]], axis=0)
                v2 = jnp.concatenate([vp_ref[0, :, sl], vc_ref[0, rows, sl]], axis=0)
            else:
                k2 = jnp.concatenate([kc_ref[0, prev, sl], kc_ref[0, rows, sl]], axis=0)
                v2 = jnp.concatenate([vc_ref[0, prev, sl], vc_ref[0, rows, sl]], axis=0)
            acc = None
            lse = None
            for half in (low, jnp.logical_not(low)):
                qh = jnp.where(half, q, jnp.zeros_like(q))
                vh = jnp.where(half, v2, jnp.zeros_like(v2))
                s = lax.dot_general(qh, k2, _NT, preferred_element_type=F32) * scale
                s = jnp.where(mask, s, -jnp.inf)
                m = jnp.max(s, axis=-1, keepdims=True)
                p = jnp.exp(s - m)
                den = jnp.sum(p, axis=-1, keepdims=True)
                oh = jnp.dot(p.astype(BF16), vh, preferred_element_type=F32) / den
                lh = m + jnp.log(den)
                acc = oh if acc is None else acc + oh
                lse = jnp.broadcast_to(lh, (blk, LANES)) if lse is None else jnp.where(low, lse, lh)
            o_ref[0, rows, sl] = acc.astype(o_ref.dtype)
            lse_ref[0, rows, sl] = lse


def _dilated_attention(aq, ak, av, dil, group):
    b, m, width = aq.shape
    width //= dil
    nb = m // ATTN_BLOCK
    blocks = group
    group = min(blocks, nb)
    classes = min(dil, blocks // group)
    tq = group * ATTN_BLOCK
    cur = pl.BlockSpec((1, tq, classes * width), lambda bi, r, n: (bi, n, r))
    prv = pl.BlockSpec((1, ATTN_BLOCK, classes * width), lambda bi, r, n: (bi, jnp.maximum(n * group - 1, 0), r))
    o, lse = pl.pallas_call(
        functools.partial(_attn_kernel, group=group, scale=ATTN_HEAD_DIM ** -0.5),
        grid=(b, dil // classes, nb // group),
        in_specs=[cur, cur, prv, cur, prv],
        out_specs=[cur, cur],
        out_shape=[jax.ShapeDtypeStruct((b, m, dil * width), BF16),
                   jax.ShapeDtypeStruct((b, m, dil * width), F32)],
        compiler_params=_cparams(("arbitrary", "arbitrary", "arbitrary")),
        name=f"dilated_attn_d{dil}",
    )(aq, ak, ak, av, av)
    return o, lse


def _split_bf16(v):
    hi = v.astype(BF16)
    return hi, (v - hi.astype(F32)).astype(BF16)


def _out_proj_kernel(oa_ref, *refs, dils):
    n_pat = len(dils)
    o_refs, l_refs = refs[:n_pat], refs[n_pat:2 * n_pat]
    (x_ref, gb_ref, hm_ref, wa_ref, wb_ref, g2_ref, rwh_ref, rwl_ref, rb_ref,
     x1_ref, h2_ref, lg_ref) = refs[2 * n_pat:2 * n_pat + 12]
    slabs = refs[2 * n_pat + 12:]
    ratio = min(d for d in dils if d > 1)
    outs = [_from_classes(o_refs[p], slabs[4 * p], slabs[4 * p + 1], dils[p], ratio) for p in range(n_pat)]
    lses = [_from_classes(l_refs[p], slabs[4 * p + 2], slabs[4 * p + 3], dils[p], ratio) for p in range(n_pat)]
    mx = functools.reduce(jnp.maximum, lses)
    es = [jnp.exp(l - mx) for l in lses]
    ob = sum(e * o for e, o in zip(es, outs)) / sum(es)
    sq_hi, sq_lo = _split_bf16(ob * ob)
    ms = (jnp.dot(sq_hi, hm_ref[...], preferred_element_type=F32)
          + jnp.dot(sq_lo, hm_ref[...], preferred_element_type=F32))
    obn = (ob * lax.rsqrt(ms + NORM_EPS) * gb_ref[...]).astype(BF16)
    y = jnp.dot(oa_ref[...], wa_ref[...], preferred_element_type=F32)
    y = y + jnp.dot(obn, wb_ref[...], preferred_element_type=F32)
    x1 = x_ref[...] + y
    x1_ref[...] = x1
    h2 = (x1 * lax.rsqrt(jnp.mean(x1 * x1, axis=-1, keepdims=True) + NORM_EPS)) * g2_ref[...]
    h_hi, h_lo = _split_bf16(h2)
    h2_ref[...] = h_hi
    lg = (jnp.dot(h_hi, rwh_ref[...], preferred_element_type=F32)
          + jnp.dot(h_hi, rwl_ref[...], preferred_element_type=F32)
          + jnp.dot(h_lo, rwh_ref[...], preferred_element_type=F32)) + rb_ref[...]
    lg_ref[...] = lg.T


def _out_proj(oa, obs, lses, dils, x2, gain_b, head_mean, w_a, w_b, gain2, rw_hi, rw_lo, rb, tile_m):
    n, d = x2.shape
    wm = oa.shape[1]
    row = lambda w: pl.BlockSpec((tile_m, w), lambda i: (i, 0))
    cls = [pl.BlockSpec((tile_m // dil, dil * wm), lambda i: (i, 0)) for dil in dils]
    full = lambda a: pl.BlockSpec(a.shape, lambda i: (0, 0))
    consts = (gain_b, head_mean, w_a, w_b, gain2, rw_hi, rw_lo, rb)
    return pl.pallas_call(
        functools.partial(_out_proj_kernel, dils=dils),
        grid=(n // tile_m,),
        in_specs=[row(wm)] + cls + cls + [row(d)] + [full(a) for a in consts],
        out_specs=[row(d), row(d), pl.BlockSpec((LANES, tile_m), lambda i: (0, i))],
        out_shape=[jax.ShapeDtypeStruct((n, d), F32), jax.ShapeDtypeStruct((n, d), BF16),
                   jax.ShapeDtypeStruct((LANES, n), F32)],
        scratch_shapes=[pltpu.VMEM((wm // LANES, tile_m, LANES), F32) for _ in range(4 * len(dils))],
        compiler_params=_cparams(("arbitrary",)),
        name="out_proj",
    )(oa, *obs, *lses, x2, *consts)


def _route_kernel(lgt_ref, gate_ref, pos_ref, post_ref, start_ref, nch_ref, nfull_ref, off_ref, cnt_ref,
                  carry_ref):
    ne, tt = carry_ref.shape[0], lgt_ref.shape[1]

    @pl.when(pl.program_id(0) == 0)
    def _():
        carry_ref[...] = jnp.zeros_like(carry_ref)

    logits = lgt_ref[0:ne, :]
    erow = lax.broadcasted_iota(jnp.int32, (ne, tt), 0).astype(F32)
    vals, idxs = [], []
    for _ in range(TOP_K):
        m = jnp.max(logits, axis=0, keepdims=True)
        ix = jnp.min(jnp.where(logits == m, erow, float(ne)), axis=0, keepdims=True)
        vals.append(m)
        idxs.append(ix)
        logits = jnp.where(erow == ix, -jnp.inf, logits)
    exps = [jnp.exp(v - vals[0]) for v in vals]
    den = exps[0] + exps[1] + exps[2] + exps[3]
    chosen = jnp.zeros((ne, tt), F32)
    for ix in idxs:
        chosen = chosen + jnp.where(erow == ix, 1.0, 0.0)
    s = lax.broadcasted_iota(jnp.int32, (tt, tt), 0)
    t = lax.broadcasted_iota(jnp.int32, (tt, tt), 1)
    before = jnp.dot(chosen.astype(BF16), jnp.where(s < t, 1.0, 0.0).astype(BF16), preferred_element_type=F32)
    count = jnp.broadcast_to(jnp.sum(chosen, axis=1, keepdims=True), (ne, LANES))
    carry = carry_ref[...]
    head = carry - MOE_CHUNK * jnp.floor(carry * (1.0 / MOE_CHUNK))
    nfull = jnp.floor((head + count) * (1.0 / MOE_CHUNK))
    nch = jnp.floor((head + count + (MOE_CHUNK - 1)) * (1.0 / MOE_CHUNK))
    er = lax.broadcasted_iota(jnp.int32, (ne, ne), 0)
    ec = lax.broadcasted_iota(jnp.int32, (ne, ne), 1)
    off = MOE_CHUNK * jnp.dot(jnp.where(ec < er, 1.0, 0.0).astype(BF16), nch.astype(BF16),
                              preferred_element_type=F32)
    slot = before + jnp.concatenate([off + head] * (tt // LANES), axis=1)
    krow = lax.broadcasted_iota(jnp.int32, (LANES, tt), 0)
    gate_t = jnp.zeros((LANES, tt), F32)
    pos_t = jnp.zeros((LANES, tt), F32)
    for k in range(TOP_K):
        pk = jnp.sum(jnp.where(erow == idxs[k], slot, 0.0), axis=0, keepdims=True)
        gate_t = jnp.where(krow == k, exps[k] / den, gate_t)
        pos_t = jnp.where(krow == k, pk, pos_t)
    gate_ref[...] = gate_t.T
    pos_ref[...] = pos_t.T.astype(jnp.int32)
    post_ref[...] = pos_t[0:SUBLANES].astype(jnp.int32)
    start_ref[0] = (carry - head).astype(jnp.int32)
    nch_ref[0] = nch.astype(jnp.int32)
    nfull_ref[0] = nfull.astype(jnp.int32)
    off_ref[0] = off.astype(jnp.int32)
    carry_ref[...] = carry + count
    cnt_ref[...] = (carry + count).astype(jnp.int32)


def _route(logits_t, tile_t, n_experts):
    n = logits_t.shape[1]
    nt = n // tile_t
    ne = pl.cdiv(n_experts, SUBLANES) * SUBLANES
    row = pl.BlockSpec((tile_t, LANES), lambda i: (i, 0))
    tab = pl.BlockSpec((1, ne, LANES), lambda i: (i, 0, 0))
    tab_shape = jax.ShapeDtypeStruct((nt, ne, LANES), jnp.int32)
    return pl.pallas_call(
        _route_kernel,
        grid=(nt,),
        in_specs=[pl.BlockSpec((LANES, tile_t), lambda i: (0, i))],
        out_specs=[row, row, pl.BlockSpec((SUBLANES, tile_t), lambda i: (i, 0)), tab, tab, tab, tab,
                   pl.BlockSpec((ne, LANES), lambda i: (0, 0))],
        out_shape=[jax.ShapeDtypeStruct((n, LANES), F32), jax.ShapeDtypeStruct((n, LANES), jnp.int32),
                   jax.ShapeDtypeStruct((nt * SUBLANES, tile_t), jnp.int32), tab_shape, tab_shape, tab_shape,
                   tab_shape, jax.ShapeDtypeStruct((ne, LANES), jnp.int32)],
        scratch_shapes=[pltpu.VMEM((ne, LANES), F32)],
        compiler_params=_cparams(("arbitrary",)),
        name="route",
    )(logits_t)


def _for_each_chunk(tab_ref, n, fn):
    def one(c):
        fn(pl.multiple_of(c * MOE_CHUNK, MOE_CHUNK), pl.multiple_of(tab_ref[0, 0, c], MOE_CHUNK))

    def four(q, carry):
        for u in range(4):
            one(q * 4 + u)
        return carry

    def single(c, carry):
        one(c)
        return carry

    lax.fori_loop(0, lax.shift_right_logical(n, 2), four, 0)
    lax.fori_loop(n & ~3, n, single, 0)


def _pad_sizes():
    sizes, s = [], MOE_ROWS // 2
    while s >= MOE_CHUNK:
        sizes.append(s)
        s //= 2
    return sizes


def _scatter_kernel(nch_ref, nfull_ref, off_ref, total_ref, padrow_ref, padlen_ref, nused_ref, post_ref, tab_ref,
                    tabp_ref, h_ref, xout_ref, stage_ref, open_ref, zero_ref, sems, zsem, *, n_experts):
    rs, tt = stage_ref.shape[1], h_ref.shape[0]
    i, nt = pl.program_id(0), pl.num_programs(0)
    slot = i % 2
    base = i * n_experts
    stage = stage_ref.at[slot]

    @pl.when(i == 0)
    def _():
        open_ref[...] = jnp.zeros_like(open_ref)
        zero_ref[...] = jnp.zeros_like(zero_ref)

    srow = lax.broadcasted_iota(jnp.int32, (rs, tt), 0)
    post = post_ref[...]
    hit = srow == post[0:1]
    for k in range(1, TOP_K):
        hit = hit | (srow == post[k:k + 1])
    sel = jnp.where(hit, 1.0, 0.0).astype(BF16)
    stage[...] = jnp.dot(sel, h_ref[...], preferred_element_type=F32)

    def add_open(e, carry):
        rows = pl.ds(pl.multiple_of(off_ref[base + e], MOE_CHUNK), MOE_CHUNK)
        stage[rows, :] = stage[rows, :] + open_ref[e]
        return carry

    def save_open(e, carry):
        nfull = nfull_ref[base + e]
        rows = pl.ds(pl.multiple_of(off_ref[base + e] + nfull * MOE_CHUNK, MOE_CHUNK), MOE_CHUNK)
        still_open = nch_ref[base + e] > nfull
        open_ref[e] = jnp.where(still_open, stage[rows, :], 0.0)
        return carry

    def chunk_copy(sl, srow0, xrow0):
        return pltpu.make_async_copy(stage_ref.at[sl, pl.ds(srow0, MOE_CHUNK)],
                                     xout_ref.at[pl.ds(xrow0, MOE_CHUNK)], sems.at[sl])

    def pad_copies(fn):
        def zeros_to(start, size):
            fn(pltpu.make_async_copy(zero_ref.at[pl.ds(0, size)], xout_ref.at[pl.ds(start, size)], zsem))

        def expert(e, carry):
            row, length = pl.multiple_of(padrow_ref[e], MOE_CHUNK), padlen_ref[e]
            for size in _pad_sizes():
                @pl.when((length & size) != 0)
                def _():
                    zeros_to(pl.multiple_of(row + (length & ~(2 * size - 1)), MOE_CHUNK), size)
            return carry

        def unused_half_block(c, carry):
            zeros_to(pl.multiple_of(c * (MOE_ROWS // 2), MOE_ROWS // 2), MOE_ROWS // 2)
            return carry

        lax.fori_loop(0, n_experts, expert, 0)
        lax.fori_loop(2 * nused_ref[0], xout_ref.shape[0] // (MOE_ROWS // 2), unused_half_block, 0)

    lax.fori_loop(0, n_experts, add_open, 0, unroll=8)

    @pl.when(i > 0)
    def _():
        _for_each_chunk(tabp_ref, total_ref[i - 1], lambda s0, x0: chunk_copy(1 - slot, s0, x0).wait())

    _for_each_chunk(tab_ref, total_ref[i], lambda s0, x0: chunk_copy(slot, s0, x0).start())
    lax.fori_loop(0, n_experts, save_open, 0, unroll=8)

    @pl.when(i == nt - 1)
    def _():
        pad_copies(lambda cp: cp.start())
        _for_each_chunk(tab_ref, total_ref[i], lambda s0, x0: chunk_copy(slot, s0, x0).wait())
        pad_copies(lambda cp: cp.wait())


def _stage_rows(tile_t, n_experts):
    rows = tile_t * TOP_K + 2 * n_experts * (MOE_CHUNK - 1) + MOE_CHUNK
    return pl.cdiv(rows, LANES) * LANES


def _chunk_table_spec(table, shift):
    nt, _, width = table.shape
    return pl.BlockSpec((1, 1, width), lambda i, *_: (jnp.clip(i + shift, 0, nt - 1), 0, 0), memory_space=pltpu.SMEM)


def _moe_scatter(nch, nfull, off, total, padrow, padlen, n_used, post, table, h2, n_rows, tile_t, n_experts):
    n, d = h2.shape
    tab = lambda shift: _chunk_table_spec(table, shift)
    return pl.pallas_call(
        functools.partial(_scatter_kernel, n_experts=n_experts),
        grid_spec=pltpu.PrefetchScalarGridSpec(
            num_scalar_prefetch=7,
            grid=(n // tile_t,),
            in_specs=[pl.BlockSpec((SUBLANES, tile_t), lambda i, *_: (i, 0)), tab(0), tab(-1),
                      pl.BlockSpec((tile_t, d), lambda i, *_: (i, 0))],
            out_specs=pl.BlockSpec(memory_space=pl.ANY),
            scratch_shapes=[pltpu.VMEM((2, _stage_rows(tile_t, n_experts), d), F32),
                            pltpu.VMEM((n_experts, MOE_CHUNK, d), F32),
                            pltpu.VMEM((MOE_ROWS // 2, d), F32),
                            pltpu.SemaphoreType.DMA((2,)), pltpu.SemaphoreType.DMA]),
        out_shape=jax.ShapeDtypeStruct((n_rows, d), F32),
        compiler_params=_cparams(("arbitrary",)),
        name="moe_scatter",
    )(nch, nfull, off, total, padrow, padlen, n_used, post, table, table, h2)


def _expert_kernel(be_ref, nu_ref, slot_ref, next_ref, x_ref, bg_ref, bl_ref, bd_ref, wu_hbm, wd_hbm, y_ref,
                   wu_buf, wd_buf, wg_s, wl_s, wd_s, sems):
    i = pl.program_id(0)
    used = i < nu_ref[0]
    e = be_ref[i]
    new_expert = (i == 0) | (e != be_ref[jnp.maximum(i - 1, 0)])
    slot = slot_ref[e]

    def weight_copies(expert, sl):
        return (pltpu.make_async_copy(wu_hbm.at[expert], wu_buf.at[sl], sems.at[0, sl]),
                pltpu.make_async_copy(wd_hbm.at[expert], wd_buf.at[sl], sems.at[1, sl]))

    @pl.when(used & (i == 0))
    def _():
        for cp in weight_copies(e, slot):
            cp.start()

    @pl.when(used & new_expert)
    def _():
        for cp in weight_copies(e, slot):
            cp.wait()
        nxt = next_ref[e]

        @pl.when(nxt >= 0)
        def _():
            for cp in weight_copies(nxt, 1 - slot):
                cp.start()

        r = lax.broadcasted_iota(jnp.int32, (2 * LANES, 2 * LANES), 0)
        c = lax.broadcasted_iota(jnp.int32, (2 * LANES, 2 * LANES), 1)
        pick = jnp.where(r == jnp.where(c < LANES, 2 * c, 2 * (c - LANES) + 1), 1.0, 0.0).astype(BF16)
        for cb in range(wg_s.shape[1] // LANES):
            grp = wu_buf[slot, :, cb * 2 * LANES:(cb + 1) * 2 * LANES].astype(BF16)
            out = slice(cb * LANES, (cb + 1) * LANES)
            both = jnp.dot(grp, pick, preferred_element_type=F32).astype(BF16)
            wg_s[:, out] = both[:, :LANES]
            wl_s[:, out] = both[:, LANES:]
        wd_s[...] = wd_buf[slot].astype(BF16)

    @pl.when(used)
    def _():
        x = x_ref[...].astype(BF16)
        glu = jnp.dot(x, wg_s[...], preferred_element_type=F32) + bg_ref[0]
        lin = jnp.dot(x, wl_s[...], preferred_element_type=F32) + bl_ref[0]
        glu = jnp.minimum(glu, SWIGLU_LIMIT)
        lin = jnp.clip(lin, -SWIGLU_LIMIT, SWIGLU_LIMIT)
        h = glu * jax.nn.sigmoid(SWIGLU_ALPHA * glu) * (lin + 1.0)
        y_ref[...] = jnp.dot(h.astype(BF16), wd_s[...], preferred_element_type=F32) + bd_ref[0]

    @pl.when(jnp.logical_not(used))
    def _():
        y_ref[...] = jnp.zeros_like(y_ref)


def _moe_experts(block_e, n_used, buf_slot, next_expert, xbuf, w_up, bg, bl, w_down, bd):
    n_rows, d = xbuf.shape
    de = w_down.shape[1]
    nblk = n_rows // MOE_ROWS
    rows = pl.BlockSpec((MOE_ROWS, d), lambda i, be, nu, *_: (i, 0))
    rows_in = pl.BlockSpec((MOE_ROWS, d), lambda i, be, nu, *_: (jnp.minimum(i, jnp.maximum(nu[0] - 1, 0)), 0))
    bias = lambda a: pl.BlockSpec((1,) + a.shape[1:], lambda i, be, *_: (be[i], 0, 0))
    hbm = pl.BlockSpec(memory_space=pl.ANY)
    return pl.pallas_call(
        _expert_kernel,
        grid_spec=pltpu.PrefetchScalarGridSpec(
            num_scalar_prefetch=4,
            grid=(nblk,),
            in_specs=[rows_in, bias(bg), bias(bl), bias(bd), hbm, hbm],
            out_specs=rows,
            scratch_shapes=[pltpu.VMEM((2,) + w_up.shape[1:], F32), pltpu.VMEM((2,) + w_down.shape[1:], F32),
                            pltpu.VMEM((d, de), BF16), pltpu.VMEM((d, de), BF16), pltpu.VMEM((de, d), BF16),
                            pltpu.SemaphoreType.DMA((2, 2))]),
        out_shape=jax.ShapeDtypeStruct((n_rows, d), F32),
        compiler_params=_cparams(("arbitrary",)),
        name="moe_experts",
    )(block_e, n_used, buf_slot, next_expert, xbuf, bg, bl, bd, w_up, w_down)


def _combine_kernel(total_ref, pos_ref, gate_ref, x1_ref, g_ref, tab_ref, tabn_ref, y_ref, o_ref, stage_ref, sems):
    rs, tt = stage_ref.shape[1], x1_ref.shape[0]
    i, nt = pl.program_id(0), pl.num_programs(0)
    slot = i % 2

    def chunk_copy(sl, srow0, yrow0):
        return pltpu.make_async_copy(y_ref.at[pl.ds(yrow0, MOE_CHUNK)],
                                     stage_ref.at[sl, pl.ds(srow0, MOE_CHUNK)], sems.at[sl])

    @pl.when(i == 0)
    def _():
        stage_ref[...] = jnp.zeros_like(stage_ref)
        _for_each_chunk(tab_ref, total_ref[i], lambda s0, y0: chunk_copy(slot, s0, y0).start())

    @pl.when(i + 1 < nt)
    def _():
        _for_each_chunk(tabn_ref, total_ref[i + 1], lambda s0, y0: chunk_copy(1 - slot, s0, y0).start())

    scol = lax.broadcasted_iota(jnp.int32, (tt, rs), 1)
    pos, gates = pos_ref[...], gate_ref[...]
    weights = jnp.zeros((tt, rs), F32)
    for k in range(TOP_K):
        weights = jnp.where(scol == pos[:, k:k + 1], gates[:, k:k + 1], weights)
    _for_each_chunk(tab_ref, total_ref[i], lambda s0, y0: chunk_copy(slot, s0, y0).wait())
    acc = x1_ref[...] + jnp.dot(weights.astype(BF16), stage_ref[slot].astype(BF16), preferred_element_type=F32)
    o_ref[...] = (acc * lax.rsqrt(jnp.mean(acc * acc, axis=-1, keepdims=True) + NORM_EPS)) * g_ref[...]


def _moe_combine(total, pos, gates, x1, gain, table, ybuf, tile_t, n_experts):
    n, d = x1.shape
    return pl.pallas_call(
        _combine_kernel,
        grid_spec=pltpu.PrefetchScalarGridSpec(
            num_scalar_prefetch=1,
            grid=(n // tile_t,),
            in_specs=[pl.BlockSpec((tile_t, LANES), lambda i, *_: (i, 0)),
                      pl.BlockSpec((tile_t, LANES), lambda i, *_: (i, 0)),
                      pl.BlockSpec((tile_t, d), lambda i, *_: (i, 0)),
                      pl.BlockSpec((1, d), lambda i, *_: (0, 0)),
                      _chunk_table_spec(table, 0), _chunk_table_spec(table, 1),
                      pl.BlockSpec(memory_space=pl.ANY)],
            out_specs=pl.BlockSpec((tile_t, d), lambda i, *_: (i, 0)),
            scratch_shapes=[pltpu.VMEM((2, _stage_rows(tile_t, n_experts), d), F32),
                            pltpu.SemaphoreType.DMA((2,))]),
        out_shape=jax.ShapeDtypeStruct((n, d), F32),
        compiler_params=_cparams(("arbitrary",)),
        name="moe_combine",
    )(total, pos, gates, x1, gain, table, table, ybuf)


def kernel(x, norm1_g, w_in, hgrn_lb_logits, hgrn_norm_g, attn_norm_g, w_out, norm2_g, router_w, router_b,
           w_up, b_up, w_down, b_down, final_norm_g):
    b, s, d = x.shape
    n = b * s
    depth = w_in.shape[0]
    n_experts = router_w.shape[-1]
    d_mix = w_out.shape[1]
    d_hgrn = d_mix // 2
    d_attn = d_mix - d_hgrn
    assert w_in.shape[-1] == 4 * d_hgrn + 3 * d_attn and d_hgrn == d_attn
    assert all(s % win == 0 and win // dil == ATTN_BLOCK for win, dil in DILATED_PATTERNS)
    assert s % HGRN_ROWS == 0 and n % PROJ_ROWS == 0 and n % ROUTE_TOKENS == 0 and n_experts <= LANES
    assert depth == 1, "the final rmsnorm is fused into the single layer's MoE combine"

    lb_all = jnp.cumsum(jax.nn.softmax(hgrn_lb_logits.astype(F32), axis=0), axis=0)
    lane = jnp.arange(d_attn)
    head_mean = jnp.where((lane[:, None] // ATTN_HEAD_DIM) == (lane[None, :] // ATTN_HEAD_DIM),
                          1.0 / ATTN_HEAD_DIM, 0.0).astype(BF16)
    x2 = x.reshape(n, d)
    l = 0
    dils = tuple(dil for _, dil in DILATED_PATTERNS)
    n_pat = len(dils)
    hq, hf, hi, hg, *attn = _in_proj(
        x2, norm1_g[l].reshape(1, d), w_in[l].astype(BF16),
        (BF16, F32, BF16, BF16), 3, dils, tile_m=PROJ_ROWS)
    to3 = lambda t: t.reshape(b, s, -1)
    o_a = _hgrn2(to3(hq), to3(hf), to3(hi), to3(hg), lb_all[l].reshape(1, d_hgrn),
                 hgrn_norm_g[l].reshape(1, d_hgrn), tile_s=HGRN_ROWS)
    obs, lses = [], []
    for p, dil in enumerate(dils):
        aq, ak, av = (attn[a * n_pat + p].reshape(b, s // dil, dil * d_attn) for a in range(3))
        o_p, lse_p = _dilated_attention(aq, ak, av, dil, group=ATTN_BLOCKS)
        obs.append(o_p.reshape(n // dil, dil * d_attn))
        lses.append(lse_p.reshape(n // dil, dil * d_attn))
    rw = jnp.zeros((d, LANES), F32).at[:, :n_experts].set(router_w[l])
    rw_hi = rw.astype(BF16)
    rw_lo = (rw - rw_hi.astype(F32)).astype(BF16)
    rb = jnp.full((1, LANES), -jnp.inf, F32).at[0, :n_experts].set(router_b[l])
    w_o = w_out[l].astype(BF16)
    x1, h2, logits_t = _out_proj(o_a.reshape(n, d_hgrn), obs, lses, dils, x2, attn_norm_g[l].reshape(1, d_attn),
                                 head_mean, w_o[:d_hgrn], w_o[d_hgrn:], norm2_g[l].reshape(1, d),
                                 rw_hi, rw_lo, rb, tile_m=PROJ_ROWS)
    tile_t = ROUTE_TOKENS
    gates, pos, post, start, nch, nfull, off, counts = _route(logits_t, tile_t, n_experts)

    counts = counts[:n_experts, 0]
    padded = (counts + MOE_ROWS - 1) // MOE_ROWS * MOE_ROWS
    pad_end = jnp.cumsum(padded)
    pad_start = pad_end - padded
    n_rows = n * TOP_K + n_experts * MOE_ROWS
    nblk = n_rows // MOE_ROWS
    block_row = jnp.arange(nblk, dtype=jnp.int32) * MOE_ROWS
    block_e = jnp.minimum(jnp.sum(pad_end[None, :] <= block_row[:, None], axis=1), n_experts - 1).astype(jnp.int32)
    n_used = (pad_end[-1:] // MOE_ROWS).astype(jnp.int32)
    has_rows = padded > 0
    eid = jnp.arange(n_experts, dtype=jnp.int32)
    buf_slot = ((jnp.cumsum(has_rows) - 1) % 2).astype(jnp.int32)
    following = jnp.min(jnp.where(has_rows[None, :] & (eid[None, :] > eid[:, None]), eid[None, :], n_experts), axis=1)
    next_expert = jnp.where(following < n_experts, following, -1).astype(jnp.int32)
    per_expert = lambda t: t[:, :n_experts, 0]
    seg = per_expert(start) + pad_start[None, :].astype(jnp.int32)
    nch, nfull, off = per_expert(nch), per_expert(nfull), per_expert(off)
    total = jnp.sum(nch, axis=1).astype(jnp.int32)
    first = off // MOE_CHUNK
    max_chunks = pl.cdiv(_stage_rows(tile_t, n_experts) // MOE_CHUNK, LANES) * LANES
    chunk = jnp.arange(max_chunks, dtype=jnp.int32)[None, :, None]
    owned = (first[:, None, :] <= chunk) & (chunk < (first + nch)[:, None, :])
    table = jnp.sum(jnp.where(owned, (seg - first * MOE_CHUNK)[:, None, :], 0), axis=-1) + chunk[:, :, 0] * MOE_CHUNK
    table = table.astype(jnp.int32)[:, None, :]
    filled = (counts + MOE_CHUNK - 1) // MOE_CHUNK * MOE_CHUNK
    padrow = (pad_start + filled).astype(jnp.int32)
    padlen = (padded - filled).astype(jnp.int32)

    xbuf = _moe_scatter(nch.reshape(-1), nfull.reshape(-1), off.reshape(-1), total, padrow, padlen, n_used, post,
                        table, h2, n_rows, tile_t, n_experts)
    ybuf = _moe_experts(block_e, n_used, buf_slot, next_expert, xbuf, w_up[l], b_up[l][:, None, 0::2],
                        b_up[l][:, None, 1::2], w_down[l], b_down[l][:, None, :])
    out = _moe_combine(total, pos, gates, x1, final_norm_g.reshape(1, d), table, ybuf, tile_t, n_experts)
    return out.reshape(b, s, d)
```

```python
import functools

import jax
import jax.numpy as jnp
from jax import lax
from jax.experimental import pallas as pl
from jax.experimental.pallas import tpu as pltpu

F32 = jnp.float32
BF16 = jnp.bfloat16

NORM_EPS = 1e-6
HGRN_HEAD_DIM = 128
ATTN_HEAD_DIM = 64
DILATED_PATTERNS = ((128, 1), (512, 4), (2048, 16))
TOP_K = 4
SWIGLU_ALPHA = 1.702
SWIGLU_LIMIT = 7.0

LANES = 128
SUBLANES = 8
HGRN_CHUNK = 64
ATTN_BLOCK = 128
MOE_ROWS = 512
MOE_CHUNK = SUBLANES
V7X_VMEM_BYTES = 64 * 1024 * 1024
VMEM_LIMIT = V7X_VMEM_BYTES - 8 * 1024 * 1024

PROJ_ROWS = 512
HGRN_ROWS = 1024
ATTN_BLOCKS = 4
ROUTE_TOKENS = 256

_NT = (((1,), (1,)), ((), ()))
_TN = (((0,), (0,)), ((), ()))


def _cparams(sem):
    return pltpu.CompilerParams(dimension_semantics=sem, vmem_limit_bytes=VMEM_LIMIT)


def _to_classes(res, o_refs, dils, slab_refs):
    tm, width = res.shape
    n_slab = width // LANES
    for sb in range(n_slab):
        slab_refs[0][sb] = res[:, sb * LANES:(sb + 1) * LANES]
    prev = 1
    for p, (o_ref, dil) in enumerate(zip(o_refs, dils)):
        if dil == 1:
            o_ref[...] = res.astype(o_ref.dtype)
            continue
        ratio, src, last = dil // prev, slab_refs[0] if prev == 1 else slab_refs[1], p == len(dils) - 1
        assert dil % prev == 0 and (prev == 1 or last), "one intermediate slab: at most two strided levels"
        rows = tm // dil
        for rp in range(prev):
            for r2 in range(ratio):
                r = r2 * prev + rp
                for sb in range(n_slab):
                    part = src[sb, pl.ds(rp * (tm // prev) + r2, rows, stride=ratio), :]
                    col = r * width + sb * LANES
                    o_ref[:, col:col + LANES] = part.astype(o_ref.dtype)
                    if not last:
                        slab_refs[1][sb, r * rows:(r + 1) * rows, :] = part
        prev = dil


def _from_classes(ref, slab_ref, mid_ref, dil, ratio):
    if dil == 1:
        return ref[...].astype(F32)
    rows = ref.shape[0]
    tm, width = rows * dil, ref.shape[1] // dil
    n_slab = width // LANES
    assert dil in (ratio, ratio * ratio)
    prev = dil // ratio
    for rp in range(prev):
        for r2 in range(ratio):
            r = r2 * prev + rp
            for sb in range(n_slab):
                col = r * width + sb * LANES
                part = ref[:, col:col + LANES].astype(F32)
                if prev == 1:
                    slab_ref[sb, pl.ds(r2, rows, stride=ratio), :] = part
                else:
                    mid_ref[sb, pl.ds(rp * (tm // prev) + r2, rows, stride=ratio), :] = part
    if prev > 1:
        for rp in range(prev):
            for sb in range(n_slab):
                slab_ref[sb, pl.ds(rp, tm // prev, stride=prev), :] = mid_ref[sb, rp * (tm // prev):(rp + 1) * (tm // prev), :]
    return jnp.concatenate([slab_ref[sb] for sb in range(n_slab)], axis=-1)


def _in_proj_kernel(x_ref, g_ref, w_ref, *refs, plain_dtypes, n_attn, dils):
    out_refs, slab_refs = refs[:-2], refs[-2:]
    x = x_ref[...]
    r = lax.rsqrt(jnp.mean(x * x, axis=-1, keepdims=True) + NORM_EPS)
    h = ((x * r) * g_ref[...]).astype(BF16)
    width = out_refs[0].shape[-1]
    n_plain = len(plain_dtypes)
    for j in list(range(n_plain, n_plain + n_attn)) + list(range(n_plain)):
        res = jnp.dot(h, w_ref[:, j * width:(j + 1) * width], preferred_element_type=F32)
        if j < n_plain:
            out_refs[j][...] = res.astype(out_refs[j].dtype)
        else:
            first = n_plain + (j - n_plain) * len(dils)
            _to_classes(res, out_refs[first:first + len(dils)], dils, slab_refs)


def _in_proj(x2, gain, w_bf16, plain_dtypes, n_attn, dils, tile_m):
    n, d = x2.shape
    width = w_bf16.shape[1] // (len(plain_dtypes) + n_attn)
    out_specs = [pl.BlockSpec((tile_m, width), lambda i: (i, 0)) for _ in plain_dtypes]
    out_shape = [jax.ShapeDtypeStruct((n, width), dt) for dt in plain_dtypes]
    for _ in range(n_attn):
        for dil in dils:
            out_specs.append(pl.BlockSpec((tile_m // dil, dil * width), lambda i: (i, 0)))
            out_shape.append(jax.ShapeDtypeStruct((n // dil, dil * width), BF16))
    return pl.pallas_call(
        functools.partial(_in_proj_kernel, plain_dtypes=plain_dtypes, n_attn=n_attn, dils=dils),
        grid=(n // tile_m,),
        in_specs=[pl.BlockSpec((tile_m, d), lambda i: (i, 0)),
                  pl.BlockSpec((1, d), lambda i: (0, 0)),
                  pl.BlockSpec(w_bf16.shape, lambda i: (0, 0))],
        out_specs=out_specs,
        out_shape=out_shape,
        scratch_shapes=[pltpu.VMEM((width // LANES, tile_m, LANES), F32) for _ in range(2)],
        compiler_params=_cparams(("arbitrary",)),
        name="in_proj",
    )(x2, gain, w_bf16)


def _hgrn_kernel(q_ref, f_ref, i_ref, g_ref, lb_ref, gain_ref, o_ref, state_ref, *, n_heads):
    hd = HGRN_HEAD_DIM
    width = n_heads * hd
    nblk = HGRN_CHUNK // SUBLANES

    @pl.when(pl.program_id(1) == 0)
    def _():
        state_ref[...] = jnp.zeros_like(state_ref)

    lb = lb_ref[...]
    gain = gain_ref[...]
    row8 = lax.broadcasted_iota(jnp.int32, (SUBLANES, width), 0)
    rowc = lax.broadcasted_iota(jnp.int32, (SUBLANES, HGRN_CHUNK), 0)
    colc = lax.broadcasted_iota(jnp.int32, (SUBLANES, HGRN_CHUNK), 1)
    r64 = lax.broadcasted_iota(jnp.int32, (HGRN_CHUNK, HGRN_CHUNK), 0)
    c64 = lax.broadcasted_iota(jnp.int32, (HGRN_CHUNK, HGRN_CHUNK), 1)
    same32 = (r64 // 32) == (c64 // 32)
    same16 = (r64 // 16) == (c64 // 16)

    def scan8(x):
        for s in (1, 2, 4):
            x = x + jnp.where(row8 >= s, pltpu.roll(x, s, axis=0), 0.0)
        return x

    def cat(blocks):
        return jnp.concatenate(blocks, axis=0)

    def chunk(ci, carry):
        r0 = pl.multiple_of(ci * HGRN_CHUNK, HGRN_CHUNK)
        rows = pl.ds(r0, HGRN_CHUNK)
        q = q_ref[0, rows, :].astype(F32)
        v = i_ref[0, rows, :]
        f = lb + (1.0 - lb) * jax.nn.sigmoid(f_ref[0, rows, :])
        logf = jnp.log2(f)
        kk = 1.0 - f

        qb = [q[SUBLANES * i:SUBLANES * (i + 1)] for i in range(nblk)]
        kb = [kk[SUBLANES * i:SUBLANES * (i + 1)] for i in range(nblk)]
        b8 = [scan8(logf[SUBLANES * i:SUBLANES * (i + 1)]) for i in range(nblk)]
        t8 = [jnp.broadcast_to(x[SUBLANES - 1:SUBLANES], x.shape) for x in b8]
        b16 = [b8[i] + t8[i - 1] if i % 2 else b8[i] for i in range(nblk)]
        t16 = [t8[i - i % 2] + t8[i - i % 2 + 1] for i in range(nblk)]
        b32 = [b16[i] + t16[i - 2] if (i // 2) % 2 else b16[i] for i in range(nblk)]
        t32 = [t16[i - i % 4] + t16[i - i % 4 + 2] for i in range(nblk)]
        b64 = [b32[i] + t32[0] if i >= 4 else b32[i] for i in range(nblk)]
        t64 = t32[0] + t32[4]

        zero = jnp.zeros_like(qb[0])

        def q_side(bl, span):
            return cat([qb[i] * jnp.exp2(bl[i]) if (i * SUBLANES // span) % 2 else zero for i in range(nblk)])

        def k_side(bl, tl, span):
            return cat([zero if (i * SUBLANES // span) % 2 else kb[i] * jnp.exp2(tl[i] - bl[i]) for i in range(nblk)])

        q64 = cat([qb[i] * jnp.exp2(b64[i]) for i in range(nblk)]).astype(BF16)
        k64 = cat([kb[i] * jnp.exp2(t64 - b64[i]) for i in range(nblk)]).astype(BF16)
        q32, k32 = q_side(b32, 32).astype(BF16), k_side(b32, t32, 32).astype(BF16)
        q16, k16 = q_side(b16, 16).astype(BF16), k_side(b16, t16, 16).astype(BF16)
        q8, k8 = q_side(b8, 8).astype(BF16), k_side(b8, t8, 8).astype(BF16)
        decay = jnp.exp2(t64[0:1])

        diag = [[jnp.zeros((SUBLANES, HGRN_CHUNK), F32) for _ in range(nblk)] for _ in range(n_heads)]
        for j in range(SUBLANES):
            for i in range(nblk):
                if j == 0:
                    p = qb[i] * kb[i]
                else:
                    p = qb[i] * pltpu.roll(kb[i], j, axis=0) * jnp.exp2(b8[i] - pltpu.roll(b8[i], j, axis=0))
                hit = (colc == rowc + (SUBLANES * i - j)) & (rowc >= j)
                for h in range(n_heads):
                    w = jnp.sum(p[:, h * hd:(h + 1) * hd], axis=-1, keepdims=True)
                    diag[h][i] = jnp.where(hit, w, diag[h][i])

        outs = []
        for h in range(n_heads):
            sl = slice(h * hd, (h + 1) * hd)
            a = cat(diag[h])
            a = a + lax.dot_general(q32[:, sl], k32[:, sl], _NT, preferred_element_type=F32)
            a = a + jnp.where(same32, lax.dot_general(q16[:, sl], k16[:, sl], _NT, preferred_element_type=F32), 0.0)
            a = a + jnp.where(same16, lax.dot_general(q8[:, sl], k8[:, sl], _NT, preferred_element_type=F32), 0.0)
            st = state_ref[h]
            o = lax.dot_general(q64[:, sl], st.astype(BF16), _NT, preferred_element_type=F32)
            o = o + jnp.dot(a.astype(BF16), v[:, sl], preferred_element_type=F32)
            state_ref[h] = st * decay[:, sl] + lax.dot_general(v[:, sl], k64[:, sl], _TN,
                                                              preferred_element_type=F32)
            o = o * lax.rsqrt(jnp.mean(o * o, axis=-1, keepdims=True) + NORM_EPS)
            outs.append(o)
        o = jnp.concatenate(outs, axis=-1) * gain
        o_ref[0, rows, :] = (o * jax.nn.silu(g_ref[0, rows, :].astype(F32))).astype(o_ref.dtype)
        return carry

    lax.fori_loop(0, q_ref.shape[1] // HGRN_CHUNK, chunk, 0, unroll=4)


def _hgrn2(hq, hf, hi, hg, lb, gain, tile_s):
    b, s, width = hq.shape
    n_heads = width // HGRN_HEAD_DIM
    blk = pl.BlockSpec((1, tile_s, width), lambda bi, si: (bi, si, 0))
    vec = pl.BlockSpec((1, width), lambda bi, si: (0, 0))
    return pl.pallas_call(
        functools.partial(_hgrn_kernel, n_heads=n_heads),
        grid=(b, s // tile_s),
        in_specs=[blk, blk, blk, blk, vec, vec],
        out_specs=blk,
        out_shape=jax.ShapeDtypeStruct((b, s, width), BF16),
        scratch_shapes=[pltpu.VMEM((n_heads, HGRN_HEAD_DIM, HGRN_HEAD_DIM), F32)],
        compiler_params=_cparams(("arbitrary", "arbitrary")),
        name="hgrn2",
    )(hq, hf, hi, hg, lb, gain)


def _attn_kernel(qc_ref, kc_ref, kp_ref, vc_ref, vp_ref, o_ref, lse_ref, *, group, scale):
    blk = ATTN_BLOCK
    n_pairs = qc_ref.shape[-1] // LANES
    lane = lax.broadcasted_iota(jnp.int32, (1, LANES), 1)
    low = lane < ATTN_HEAD_DIM
    qi = lax.broadcasted_iota(jnp.int32, (blk, 2 * blk), 0)
    ki = lax.broadcasted_iota(jnp.int32, (blk, 2 * blk), 1)
    band = ((ki >= blk) & (ki - blk <= qi)) | ((ki < blk) & (ki >= qi))
    first_key = jnp.where(pl.program_id(2) == 0, blk, 0)
    band_first = band & (ki >= first_key)

    for g in range(group):
        rows = slice(g * blk, (g + 1) * blk)
        prev = slice((g - 1) * blk, g * blk)
        mask = band_first if g == 0 else band
        for pr in range(n_pairs):
            sl = slice(pr * LANES, (pr + 1) * LANES)
            q = qc_ref[0, rows, sl]
            if g == 0:
                k2 = jnp.concatenate([kp_ref[0, :, sl], kc_ref[0, rows, sl]], axis=0)
                v2 = jnp.concatenate([vp_ref[0, :, sl], vc_ref[0, rows, sl]], axis=0)
            else:
                k2 = jnp.concatenate([kc_ref[0, prev, sl], kc_ref[0, rows, sl]], axis=0)
                v2 = jnp.concatenate([vc_ref[0, prev, sl], vc_ref[0, rows, sl]], axis=0)
            acc = None
            lse = None
            for half in (low, jnp.logical_not(low)):
                qh = jnp.where(half, q, jnp.zeros_like(q))
                vh = jnp.where(half, v2, jnp.zeros_like(v2))
                s = lax.dot_general(qh, k2, _NT, preferred_element_type=F32) * scale
                s = jnp.where(mask, s, -jnp.inf)
                m = jnp.max(s, axis=-1, keepdims=True)
                p = jnp.exp(s - m)
                den = jnp.sum(p, axis=-1, keepdims=True)
                oh = jnp.dot(p.astype(BF16), vh, preferred_element_type=F32) / den
                lh = m + jnp.log(den)
                acc = oh if acc is None else acc + oh
                lse = jnp.broadcast_to(lh, (blk, LANES)) if lse is None else jnp.where(low, lse, lh)
            o_ref[0, rows, sl] = acc.astype(o_ref.dtype)
            lse_ref[0, rows, sl] = lse


def _dilated_attention(aq, ak, av, dil, group):
    b, m, width = aq.shape
    width //= dil
    nb = m // ATTN_BLOCK
    blocks = group
    group = min(blocks, nb)
    classes = min(dil, blocks // group)
    tq = group * ATTN_BLOCK
    cur = pl.BlockSpec((1, tq, classes * width), lambda bi, r, n: (bi, n, r))
    prv = pl.BlockSpec((1, ATTN_BLOCK, classes * width), lambda bi, r, n: (bi, jnp.maximum(n * group - 1, 0), r))
    o, lse = pl.pallas_call(
        functools.partial(_attn_kernel, group=group, scale=ATTN_HEAD_DIM ** -0.5),
        grid=(b, dil // classes, nb // group),
        in_specs=[cur, cur, prv, cur, prv],
        out_specs=[cur, cur],
        out_shape=[jax.ShapeDtypeStruct((b, m, dil * width), BF16),
                   jax.ShapeDtypeStruct((b, m, dil * width), F32)],
        compiler_params=_cparams(("arbitrary", "arbitrary", "arbitrary")),
        name=f"dilated_attn_d{dil}",
    )(aq, ak, ak, av, av)
    return o, lse


def _split_bf16(v):
    hi = v.astype(BF16)
    return hi, (v - hi.astype(F32)).astype(BF16)


def _out_proj_kernel(oa_ref, *refs, dils, tile_t):
    n_pat = len(dils)
    o_refs, l_refs = refs[:n_pat], refs[n_pat:2 * n_pat]
    (x_ref, gb_ref, hm_ref, wa_ref, wb_ref, g2_ref, rwh_ref, rwl_ref, rb_ref,
     x1_ref, h2_ref) = refs[2 * n_pat:2 * n_pat + 11]
    route_refs = refs[2 * n_pat + 11:2 * n_pat + 19]
    carry_ref = refs[2 * n_pat + 19]
    slabs = refs[2 * n_pat + 20:]
    ratio = min(d for d in dils if d > 1)

    @pl.when(pl.program_id(0) == 0)
    def _():
        carry_ref[...] = jnp.zeros_like(carry_ref)

    outs = [_from_classes(o_refs[p], slabs[4 * p], slabs[4 * p + 1], dils[p], ratio) for p in range(n_pat)]
    lses = [_from_classes(l_refs[p], slabs[4 * p + 2], slabs[4 * p + 3], dils[p], ratio) for p in range(n_pat)]
    mx = functools.reduce(jnp.maximum, lses)
    es = [jnp.exp(l - mx) for l in lses]
    ob = sum(e * o for e, o in zip(es, outs)) / sum(es)
    sq_hi, sq_lo = _split_bf16(ob * ob)
    ms = (jnp.dot(sq_hi, hm_ref[...], preferred_element_type=F32)
          + jnp.dot(sq_lo, hm_ref[...], preferred_element_type=F32))
    obn = (ob * lax.rsqrt(ms + NORM_EPS) * gb_ref[...]).astype(BF16)
    y = jnp.dot(oa_ref[...], wa_ref[...], preferred_element_type=F32)
    y = y + jnp.dot(obn, wb_ref[...], preferred_element_type=F32)
    x1 = x_ref[...] + y
    x1_ref[...] = x1
    h2 = (x1 * lax.rsqrt(jnp.mean(x1 * x1, axis=-1, keepdims=True) + NORM_EPS)) * g2_ref[...]
    h_hi, h_lo = _split_bf16(h2)
    h2_ref[...] = h_hi
    lg = (jnp.dot(h_hi, rwh_ref[...], preferred_element_type=F32)
          + jnp.dot(h_hi, rwl_ref[...], preferred_element_type=F32)
          + jnp.dot(h_lo, rwh_ref[...], preferred_element_type=F32)) + rb_ref[...]
    for u in range(lg.shape[0] // tile_t):
        _route_tile(lg[u * tile_t:(u + 1) * tile_t].T, u, *route_refs, carry_ref)


def _out_proj(oa, obs, lses, dils, x2, gain_b, head_mean, w_a, w_b, gain2, rw_hi, rw_lo, rb, tile_m, tile_t,
              n_experts):
    n, d = x2.shape
    wm = oa.shape[1]
    nt, per_step = n // tile_t, tile_m // tile_t
    ne = pl.cdiv(n_experts, SUBLANES) * SUBLANES
    row = lambda w: pl.BlockSpec((tile_m, w), lambda i: (i, 0))
    cls = [pl.BlockSpec((tile_m // dil, dil * wm), lambda i: (i, 0)) for dil in dils]
    full = lambda a: pl.BlockSpec(a.shape, lambda i: (0, 0))
    tab = pl.BlockSpec((per_step, ne, LANES), lambda i: (i, 0, 0))
    tab_shape = jax.ShapeDtypeStruct((nt, ne, LANES), jnp.int32)
    consts = (gain_b, head_mean, w_a, w_b, gain2, rw_hi, rw_lo, rb)
    return pl.pallas_call(
        functools.partial(_out_proj_kernel, dils=dils, tile_t=tile_t),
        grid=(n // tile_m,),
        in_specs=[row(wm)] + cls + cls + [row(d)] + [full(a) for a in consts],
        out_specs=[row(d), row(d), row(LANES), row(LANES),
                   pl.BlockSpec((per_step * SUBLANES, tile_t), lambda i: (i, 0)), tab, tab, tab, tab,
                   pl.BlockSpec((ne, LANES), lambda i: (0, 0))],
        out_shape=[jax.ShapeDtypeStruct((n, d), F32), jax.ShapeDtypeStruct((n, d), BF16),
                   jax.ShapeDtypeStruct((n, LANES), F32), jax.ShapeDtypeStruct((n, LANES), jnp.int32),
                   jax.ShapeDtypeStruct((nt * SUBLANES, tile_t), jnp.int32), tab_shape, tab_shape, tab_shape,
                   tab_shape, jax.ShapeDtypeStruct((ne, LANES), jnp.int32)],
        scratch_shapes=[pltpu.VMEM((ne, LANES), F32)]
        + [pltpu.VMEM((wm // LANES, tile_m, LANES), F32) for _ in range(4 * len(dils))],
        compiler_params=_cparams(("arbitrary",)),
        name="out_proj",
    )(oa, *obs, *lses, x2, *consts)


def _route_tile(logits_t, u, gate_ref, pos_ref, post_ref, start_ref, nch_ref, nfull_ref, off_ref, cnt_ref,
                carry_ref):
    ne, tt = carry_ref.shape[0], logits_t.shape[1]

    logits = logits_t[0:ne, :]
    erow = lax.broadcasted_iota(jnp.int32, (ne, tt), 0).astype(F32)
    vals, idxs = [], []
    for _ in range(TOP_K):
        m = jnp.max(logits, axis=0, keepdims=True)
        ix = jnp.min(jnp.where(logits == m, erow, float(ne)), axis=0, keepdims=True)
        vals.append(m)
        idxs.append(ix)
        logits = jnp.where(erow == ix, -jnp.inf, logits)
    exps = [jnp.exp(v - vals[0]) for v in vals]
    den = exps[0] + exps[1] + exps[2] + exps[3]
    chosen = jnp.zeros((ne, tt), F32)
    for ix in idxs:
        chosen = chosen + jnp.where(erow == ix, 1.0, 0.0)
    s = lax.broadcasted_iota(jnp.int32, (tt, tt), 0)
    t = lax.broadcasted_iota(jnp.int32, (tt, tt), 1)
    before = jnp.dot(chosen.astype(BF16), jnp.where(s < t, 1.0, 0.0).astype(BF16), preferred_element_type=F32)
    count = jnp.broadcast_to(jnp.sum(chosen, axis=1, keepdims=True), (ne, LANES))
    carry = carry_ref[...]
    head = carry - MOE_CHUNK * jnp.floor(carry * (1.0 / MOE_CHUNK))
    nfull = jnp.floor((head + count) * (1.0 / MOE_CHUNK))
    nch = jnp.floor((head + count + (MOE_CHUNK - 1)) * (1.0 / MOE_CHUNK))
    er = lax.broadcasted_iota(jnp.int32, (ne, ne), 0)
    ec = lax.broadcasted_iota(jnp.int32, (ne, ne), 1)
    off = MOE_CHUNK * jnp.dot(jnp.where(ec < er, 1.0, 0.0).astype(BF16), nch.astype(BF16),
                              preferred_element_type=F32)
    slot = before + jnp.concatenate([off + head] * (tt // LANES), axis=1)
    krow = lax.broadcasted_iota(jnp.int32, (LANES, tt), 0)
    gate_t = jnp.zeros((LANES, tt), F32)
    pos_t = jnp.zeros((LANES, tt), F32)
    for k in range(TOP_K):
        pk = jnp.sum(jnp.where(erow == idxs[k], slot, 0.0), axis=0, keepdims=True)
        gate_t = jnp.where(krow == k, exps[k] / den, gate_t)
        pos_t = jnp.where(krow == k, pk, pos_t)
    gate_ref[u * tt:(u + 1) * tt, :] = gate_t.T
    pos_ref[u * tt:(u + 1) * tt, :] = pos_t.T.astype(jnp.int32)
    post_ref[u * SUBLANES:(u + 1) * SUBLANES, :] = pos_t[0:SUBLANES].astype(jnp.int32)
    start_ref[u] = (carry - head).astype(jnp.int32)
    nch_ref[u] = nch.astype(jnp.int32)
    nfull_ref[u] = nfull.astype(jnp.int32)
    off_ref[u] = off.astype(jnp.int32)
    carry_ref[...] = carry + count
    cnt_ref[...] = (carry + count).astype(jnp.int32)


def _for_each_chunk(tab_ref, n, fn):
    def one(c):
        fn(pl.multiple_of(c * MOE_CHUNK, MOE_CHUNK), pl.multiple_of(tab_ref[0, 0, c], MOE_CHUNK))

    def four(q, carry):
        for u in range(4):
            one(q * 4 + u)
        return carry

    def single(c, carry):
        one(c)
        return carry

    lax.fori_loop(0, lax.shift_right_logical(n, 2), four, 0)
    lax.fori_loop(n & ~3, n, single, 0)


def _pad_sizes():
    sizes, s = [], MOE_ROWS // 2
    while s >= MOE_CHUNK:
        sizes.append(s)
        s //= 2
    return sizes


def _scatter_kernel(nch_ref, nfull_ref, off_ref, total_ref, padrow_ref, padlen_ref, nused_ref, post_ref, tab_ref,
                    tabp_ref, h_ref, xout_ref, stage_ref, open_ref, zero_ref, sems, zsem, *, n_experts):
    rs, tt = stage_ref.shape[1], h_ref.shape[0]
    i, nt = pl.program_id(0), pl.num_programs(0)
    slot = i % 2
    base = i * n_experts
    stage = stage_ref.at[slot]

    @pl.when(i == 0)
    def _():
        open_ref[...] = jnp.zeros_like(open_ref)
        zero_ref[...] = jnp.zeros_like(zero_ref)

    srow = lax.broadcasted_iota(jnp.int32, (rs, tt), 0)
    post = post_ref[...]
    hit = srow == post[0:1]
    for k in range(1, TOP_K):
        hit = hit | (srow == post[k:k + 1])
    sel = jnp.where(hit, 1.0, 0.0).astype(BF16)
    stage[...] = jnp.dot(sel, h_ref[...], preferred_element_type=F32)

    def add_open(e, carry):
        rows = pl.ds(pl.multiple_of(off_ref[base + e], MOE_CHUNK), MOE_CHUNK)
        stage[rows, :] = stage[rows, :] + open_ref[e]
        return carry

    def save_open(e, carry):
        nfull = nfull_ref[base + e]
        rows = pl.ds(pl.multiple_of(off_ref[base + e] + nfull * MOE_CHUNK, MOE_CHUNK), MOE_CHUNK)
        still_open = nch_ref[base + e] > nfull
        open_ref[e] = jnp.where(still_open, stage[rows, :], 0.0)
        return carry

    def chunk_copy(sl, srow0, xrow0):
        return pltpu.make_async_copy(stage_ref.at[sl, pl.ds(srow0, MOE_CHUNK)],
                                     xout_ref.at[pl.ds(xrow0, MOE_CHUNK)], sems.at[sl])

    def pad_copies(fn):
        def zeros_to(start, size):
            fn(pltpu.make_async_copy(zero_ref.at[pl.ds(0, size)], xout_ref.at[pl.ds(start, size)], zsem))

        def expert(e, carry):
            row, length = pl.multiple_of(padrow_ref[e], MOE_CHUNK), padlen_ref[e]
            for size in _pad_sizes():
                @pl.when((length & size) != 0)
                def _():
                    zeros_to(pl.multiple_of(row + (length & ~(2 * size - 1)), MOE_CHUNK), size)
            return carry

        def unused_half_block(c, carry):
            zeros_to(pl.multiple_of(c * (MOE_ROWS // 2), MOE_ROWS // 2), MOE_ROWS // 2)
            return carry

        lax.fori_loop(0, n_experts, expert, 0)
        lax.fori_loop(2 * nused_ref[0], xout_ref.shape[0] // (MOE_ROWS // 2), unused_half_block, 0)

    lax.fori_loop(0, n_experts, add_open, 0, unroll=8)

    @pl.when(i > 0)
    def _():
        _for_each_chunk(tabp_ref, total_ref[i - 1], lambda s0, x0: chunk_copy(1 - slot, s0, x0).wait())

    _for_each_chunk(tab_ref, total_ref[i], lambda s0, x0: chunk_copy(slot, s0, x0).start())
    lax.fori_loop(0, n_experts, save_open, 0, unroll=8)

    @pl.when(i == nt - 1)
    def _():
        pad_copies(lambda cp: cp.start())
        _for_each_chunk(tab_ref, total_ref[i], lambda s0, x0: chunk_copy(slot, s0, x0).wait())
        pad_copies(lambda cp: cp.wait())


def _stage_rows(tile_t, n_experts):
    rows = tile_t * TOP_K + 2 * n_experts * (MOE_CHUNK - 1) + MOE_CHUNK
    return pl.cdiv(rows, LANES) * LANES


def _chunk_table_spec(table, shift):
    nt, _, width = table.shape
    return pl.BlockSpec((1, 1, width), lambda i, *_: (jnp.clip(i + shift, 0, nt - 1), 0, 0), memory_space=pltpu.SMEM)


def _moe_scatter(nch, nfull, off, total, padrow, padlen, n_used, post, table, h2, n_rows, tile_t, n_experts):
    n, d = h2.shape
    tab = lambda shift: _chunk_table_spec(table, shift)
    return pl.pallas_call(
        functools.partial(_scatter_kernel, n_experts=n_experts),
        grid_spec=pltpu.PrefetchScalarGridSpec(
            num_scalar_prefetch=7,
            grid=(n // tile_t,),
            in_specs=[pl.BlockSpec((SUBLANES, tile_t), lambda i, *_: (i, 0)), tab(0), tab(-1),
                      pl.BlockSpec((tile_t, d), lambda i, *_: (i, 0))],
            out_specs=pl.BlockSpec(memory_space=pl.ANY),
            scratch_shapes=[pltpu.VMEM((2, _stage_rows(tile_t, n_experts), d), F32),
                            pltpu.VMEM((n_experts, MOE_CHUNK, d), F32),
                            pltpu.VMEM((MOE_ROWS // 2, d), F32),
                            pltpu.SemaphoreType.DMA((2,)), pltpu.SemaphoreType.DMA]),
        out_shape=jax.ShapeDtypeStruct((n_rows, d), F32),
        compiler_params=_cparams(("arbitrary",)),
        name="moe_scatter",
    )(nch, nfull, off, total, padrow, padlen, n_used, post, table, table, h2)


def _expert_kernel(be_ref, nu_ref, valid_ref, slot_ref, next_ref, x_ref, bg_ref, bl_ref, bd_ref, wu_hbm, wd_hbm,
                   y_ref, wu_buf, wd_buf, wg_s, wl_s, wd_s, sems):
    i = pl.program_id(0)
    used = i < nu_ref[0]
    e = be_ref[i]
    new_expert = (i == 0) | (e != be_ref[jnp.maximum(i - 1, 0)])
    slot = slot_ref[e]

    def weight_copies(expert, sl):
        return (pltpu.make_async_copy(wu_hbm.at[expert], wu_buf.at[sl], sems.at[0, sl]),
                pltpu.make_async_copy(wd_hbm.at[expert], wd_buf.at[sl], sems.at[1, sl]))

    @pl.when(used & (i == 0))
    def _():
        for cp in weight_copies(e, slot):
            cp.start()

    @pl.when(used & new_expert)
    def _():
        for cp in weight_copies(e, slot):
            cp.wait()
        nxt = next_ref[e]

        @pl.when(nxt >= 0)
        def _():
            for cp in weight_copies(nxt, 1 - slot):
                cp.start()

        r = lax.broadcasted_iota(jnp.int32, (2 * LANES, 2 * LANES), 0)
        c = lax.broadcasted_iota(jnp.int32, (2 * LANES, 2 * LANES), 1)
        pick = jnp.where(r == jnp.where(c < LANES, 2 * c, 2 * (c - LANES) + 1), 1.0, 0.0).astype(BF16)
        for cb in range(wg_s.shape[1] // LANES):
            grp = wu_buf[slot, :, cb * 2 * LANES:(cb + 1) * 2 * LANES].astype(BF16)
            out = slice(cb * LANES, (cb + 1) * LANES)
            both = jnp.dot(grp, pick, preferred_element_type=F32).astype(BF16)
            wg_s[:, out] = both[:, :LANES]
            wl_s[:, out] = both[:, LANES:]
        wd_s[...] = wd_buf[slot].astype(BF16)

    def ffn(rows):
        x = x_ref[rows, :].astype(BF16)
        glu = jnp.dot(x, wg_s[...], preferred_element_type=F32) + bg_ref[0]
        lin = jnp.dot(x, wl_s[...], preferred_element_type=F32) + bl_ref[0]
        glu = jnp.minimum(glu, SWIGLU_LIMIT)
        lin = jnp.clip(lin, -SWIGLU_LIMIT, SWIGLU_LIMIT)
        h = glu * jax.nn.sigmoid(SWIGLU_ALPHA * glu) * (lin + 1.0)
        y_ref[rows, :] = jnp.dot(h.astype(BF16), wd_s[...], preferred_element_type=F32) + bd_ref[0]

    half = MOE_ROWS // 2
    full = used & (valid_ref[i] > half)

    @pl.when(full)
    def _():
        ffn(slice(0, MOE_ROWS))

    @pl.when(used & jnp.logical_not(full))
    def _():
        ffn(slice(0, half))
        y_ref[half:, :] = jnp.zeros((MOE_ROWS - half, y_ref.shape[1]), y_ref.dtype)

    @pl.when(jnp.logical_not(used))
    def _():
        y_ref[...] = jnp.zeros_like(y_ref)


def _moe_experts(block_e, n_used, valid, buf_slot, next_expert, xbuf, w_up, bg, bl, w_down, bd):
    n_rows, d = xbuf.shape
    de = w_down.shape[1]
    nblk = n_rows // MOE_ROWS
    rows = pl.BlockSpec((MOE_ROWS, d), lambda i, be, nu, *_: (i, 0))
    rows_in = pl.BlockSpec((MOE_ROWS, d), lambda i, be, nu, *_: (jnp.minimum(i, jnp.maximum(nu[0] - 1, 0)), 0))
    bias = lambda a: pl.BlockSpec((1,) + a.shape[1:], lambda i, be, *_: (be[i], 0, 0))
    hbm = pl.BlockSpec(memory_space=pl.ANY)
    return pl.pallas_call(
        _expert_kernel,
        grid_spec=pltpu.PrefetchScalarGridSpec(
            num_scalar_prefetch=5,
            grid=(nblk,),
            in_specs=[rows_in, bias(bg), bias(bl), bias(bd), hbm, hbm],
            out_specs=rows,
            scratch_shapes=[pltpu.VMEM((2,) + w_up.shape[1:], F32), pltpu.VMEM((2,) + w_down.shape[1:], F32),
                            pltpu.VMEM((d, de), BF16), pltpu.VMEM((d, de), BF16), pltpu.VMEM((de, d), BF16),
                            pltpu.SemaphoreType.DMA((2, 2))]),
        out_shape=jax.ShapeDtypeStruct((n_rows, d), F32),
        compiler_params=_cparams(("arbitrary",)),
        name="moe_experts",
    )(block_e, n_used, valid, buf_slot, next_expert, xbuf, bg, bl, bd, w_up, w_down)


def _combine_kernel(total_ref, pos_ref, gate_ref, x1_ref, g_ref, tab_ref, tabn_ref, y_ref, o_ref, stage_ref, sems):
    rs, tt = stage_ref.shape[1], x1_ref.shape[0]
    i, nt = pl.program_id(0), pl.num_programs(0)
    slot = i % 2

    def chunk_copy(sl, srow0, yrow0):
        return pltpu.make_async_copy(y_ref.at[pl.ds(yrow0, MOE_CHUNK)],
                                     stage_ref.at[sl, pl.ds(srow0, MOE_CHUNK)], sems.at[sl])

    @pl.when(i == 0)
    def _():
        stage_ref[...] = jnp.zeros_like(stage_ref)
        _for_each_chunk(tab_ref, total_ref[i], lambda s0, y0: chunk_copy(slot, s0, y0).start())

    @pl.when(i + 1 < nt)
    def _():
        _for_each_chunk(tabn_ref, total_ref[i + 1], lambda s0, y0: chunk_copy(1 - slot, s0, y0).start())

    scol = lax.broadcasted_iota(jnp.int32, (tt, rs), 1)
    pos, gates = pos_ref[...], gate_ref[...]
    weights = jnp.zeros((tt, rs), F32)
    for k in range(TOP_K):
        weights = jnp.where(scol == pos[:, k:k + 1], gates[:, k:k + 1], weights)
    _for_each_chunk(tab_ref, total_ref[i], lambda s0, y0: chunk_copy(slot, s0, y0).wait())
    acc = x1_ref[...] + jnp.dot(weights.astype(BF16), stage_ref[slot].astype(BF16), preferred_element_type=F32)
    o_ref[...] = (acc * lax.rsqrt(jnp.mean(acc * acc, axis=-1, keepdims=True) + NORM_EPS)) * g_ref[...]


def _moe_combine(total, pos, gates, x1, gain, table, ybuf, tile_t, n_experts):
    n, d = x1.shape
    return pl.pallas_call(
        _combine_kernel,
        grid_spec=pltpu.PrefetchScalarGridSpec(
            num_scalar_prefetch=1,
            grid=(n // tile_t,),
            in_specs=[pl.BlockSpec((tile_t, LANES), lambda i, *_: (i, 0)),
                      pl.BlockSpec((tile_t, LANES), lambda i, *_: (i, 0)),
                      pl.BlockSpec((tile_t, d), lambda i, *_: (i, 0)),
                      pl.BlockSpec((1, d), lambda i, *_: (0, 0)),
                      _chunk_table_spec(table, 0), _chunk_table_spec(table, 1),
                      pl.BlockSpec(memory_space=pl.ANY)],
            out_specs=pl.BlockSpec((tile_t, d), lambda i, *_: (i, 0)),
            scratch_shapes=[pltpu.VMEM((2, _stage_rows(tile_t, n_experts), d), F32),
                            pltpu.SemaphoreType.DMA((2,))]),
        out_shape=jax.ShapeDtypeStruct((n, d), F32),
        compiler_params=_cparams(("arbitrary",)),
        name="moe_combine",
    )(total, pos, gates, x1, gain, table, table, ybuf)


def kernel(x, norm1_g, w_in, hgrn_lb_logits, hgrn_norm_g, attn_norm_g, w_out, norm2_g, router_w, router_b,
           w_up, b_up, w_down, b_down, final_norm_g):
    b, s, d = x.shape
    n = b * s
    depth = w_in.shape[0]
    n_experts = router_w.shape[-1]
    d_mix = w_out.shape[1]
    d_hgrn = d_mix // 2
    d_attn = d_mix - d_hgrn
    assert w_in.shape[-1] == 4 * d_hgrn + 3 * d_attn and d_hgrn == d_attn
    assert all(s % win == 0 and win // dil == ATTN_BLOCK for win, dil in DILATED_PATTERNS)
    assert s % HGRN_ROWS == 0 and n % PROJ_ROWS == 0 and PROJ_ROWS % ROUTE_TOKENS == 0 and n_experts <= LANES
    assert depth == 1, "the final rmsnorm is fused into the single layer's MoE combine"

    lb_all = jnp.cumsum(jax.nn.softmax(hgrn_lb_logits.astype(F32), axis=0), axis=0)
    lane = jnp.arange(d_attn)
    head_mean = jnp.where((lane[:, None] // ATTN_HEAD_DIM) == (lane[None, :] // ATTN_HEAD_DIM),
                          1.0 / ATTN_HEAD_DIM, 0.0).astype(BF16)
    x2 = x.reshape(n, d)
    l = 0
    dils = tuple(dil for _, dil in DILATED_PATTERNS)
    n_pat = len(dils)
    hq, hf, hi, hg, *attn = _in_proj(
        x2, norm1_g[l].reshape(1, d), w_in[l].astype(BF16),
        (BF16, F32, BF16, BF16), 3, dils, tile_m=PROJ_ROWS)
    to3 = lambda t: t.reshape(b, s, -1)
    o_a = _hgrn2(to3(hq), to3(hf), to3(hi), to3(hg), lb_all[l].reshape(1, d_hgrn),
                 hgrn_norm_g[l].reshape(1, d_hgrn), tile_s=HGRN_ROWS)
    obs, lses = [], []
    for p, dil in enumerate(dils):
        aq, ak, av = (attn[a * n_pat + p].reshape(b, s // dil, dil * d_attn) for a in range(3))
        o_p, lse_p = _dilated_attention(aq, ak, av, dil, group=ATTN_BLOCKS)
        obs.append(o_p.reshape(n // dil, dil * d_attn))
        lses.append(lse_p.reshape(n // dil, dil * d_attn))
    rw = jnp.zeros((d, LANES), F32).at[:, :n_experts].set(router_w[l])
    rw_hi = rw.astype(BF16)
    rw_lo = (rw - rw_hi.astype(F32)).astype(BF16)
    rb = jnp.full((1, LANES), -jnp.inf, F32).at[0, :n_experts].set(router_b[l])
    w_o = w_out[l].astype(BF16)
    tile_t = ROUTE_TOKENS
    x1, h2, gates, pos, post, start, nch, nfull, off, counts = _out_proj(
        o_a.reshape(n, d_hgrn), obs, lses, dils, x2, attn_norm_g[l].reshape(1, d_attn), head_mean,
        w_o[:d_hgrn], w_o[d_hgrn:], norm2_g[l].reshape(1, d), rw_hi, rw_lo, rb, PROJ_ROWS, tile_t, n_experts)

    counts = counts[:n_experts, 0]
    padded = (counts + MOE_ROWS - 1) // MOE_ROWS * MOE_ROWS
    pad_end = jnp.cumsum(padded)
    pad_start = pad_end - padded
    n_rows = n * TOP_K + n_experts * MOE_ROWS
    nblk = n_rows // MOE_ROWS
    block_row = jnp.arange(nblk, dtype=jnp.int32) * MOE_ROWS
    block_e = jnp.minimum(jnp.sum(pad_end[None, :] <= block_row[:, None], axis=1), n_experts - 1).astype(jnp.int32)
    n_used = (pad_end[-1:] // MOE_ROWS).astype(jnp.int32)
    eid = jnp.arange(n_experts, dtype=jnp.int32)
    group_end = jnp.sum(jnp.where(eid[None, :] == block_e[:, None], (pad_start + counts)[None, :], 0), axis=1)
    valid = jnp.clip(group_end - block_row, 0, MOE_ROWS).astype(jnp.int32)
    has_rows = padded > 0
    buf_slot = ((jnp.cumsum(has_rows) - 1) % 2).astype(jnp.int32)
    following = jnp.min(jnp.where(has_rows[None, :] & (eid[None, :] > eid[:, None]), eid[None, :], n_experts), axis=1)
    next_expert = jnp.where(following < n_experts, following, -1).astype(jnp.int32)
    per_expert = lambda t: t[:, :n_experts, 0]
    seg = per_expert(start) + pad_start[None, :].astype(jnp.int32)
    nch, nfull, off = per_expert(nch), per_expert(nfull), per_expert(off)
    total = jnp.sum(nch, axis=1).astype(jnp.int32)
    first = off // MOE_CHUNK
    max_chunks = pl.cdiv(_stage_rows(tile_t, n_experts) // MOE_CHUNK, LANES) * LANES
    chunk = jnp.arange(max_chunks, dtype=jnp.int32)[None, :, None]
    owned = (first[:, None, :] <= chunk) & (chunk < (first + nch)[:, None, :])
    table = jnp.sum(jnp.where(owned, (seg - first * MOE_CHUNK)[:, None, :], 0), axis=-1) + chunk[:, :, 0] * MOE_CHUNK
    table = table.astype(jnp.int32)[:, None, :]
    filled = (counts + MOE_CHUNK - 1) // MOE_CHUNK * MOE_CHUNK
    padrow = (pad_start + filled).astype(jnp.int32)
    padlen = (padded - filled).astype(jnp.int32)

    xbuf = _moe_scatter(nch.reshape(-1), nfull.reshape(-1), off.reshape(-1), total, padrow, padlen, n_used, post,
                        table, h2, n_rows, tile_t, n_experts)
    ybuf = _moe_experts(block_e, n_used, valid, buf_slot, next_expert, xbuf, w_up[l], b_up[l][:, None, 0::2],
                        b_up[l][:, None, 1::2], w_down[l], b_down[l][:, None, :])
    out = _moe_combine(total, pos, gates, x1, final_norm_g.reshape(1, d), table, ybuf, tile_t, n_experts)
    return out.reshape(b, s, d)
```

```python
import functools

import jax
import jax.numpy as jnp
from jax import lax
from jax.experimental import pallas as pl
from jax.experimental.pallas import tpu as pltpu

F32 = jnp.float32
BF16 = jnp.bfloat16

NORM_EPS = 1e-6
HGRN_HEAD_DIM = 128
ATTN_HEAD_DIM = 64
DILATED_PATTERNS = ((128, 1), (512, 4), (2048, 16))
TOP_K = 4
SWIGLU_ALPHA = 1.702
SWIGLU_LIMIT = 7.0

LANES = 128
SUBLANES = 8
HGRN_CHUNK = 64
ATTN_BLOCK = 128
MOE_ROWS = 512
MOE_CHUNK = SUBLANES
V7X_VMEM_BYTES = 64 * 1024 * 1024
VMEM_LIMIT = V7X_VMEM_BYTES - 8 * 1024 * 1024

PROJ_ROWS = 512
HGRN_ROWS = 1024
ATTN_BLOCKS = 4
ROUTE_TOKENS = 256

_NT = (((1,), (1,)), ((), ()))
_TN = (((0,), (0,)), ((), ()))


def _cparams(sem):
    return pltpu.CompilerParams(dimension_semantics=sem, vmem_limit_bytes=VMEM_LIMIT)


def _to_classes(res, o_refs, dils, slab_refs):
    tm, width = res.shape
    n_slab = width // LANES
    for sb in range(n_slab):
        slab_refs[0][sb] = res[:, sb * LANES:(sb + 1) * LANES]
    prev = 1
    for p, (o_ref, dil) in enumerate(zip(o_refs, dils)):
        if dil == 1:
            o_ref[...] = res.astype(o_ref.dtype)
            continue
        ratio, src, last = dil // prev, slab_refs[0] if prev == 1 else slab_refs[1], p == len(dils) - 1
        assert dil % prev == 0 and (prev == 1 or last), "one intermediate slab: at most two strided levels"
        rows = tm // dil
        for rp in range(prev):
            for r2 in range(ratio):
                r = r2 * prev + rp
                for sb in range(n_slab):
                    part = src[sb, pl.ds(rp * (tm // prev) + r2, rows, stride=ratio), :]
                    col = r * width + sb * LANES
                    o_ref[:, col:col + LANES] = part.astype(o_ref.dtype)
                    if not last:
                        slab_refs[1][sb, r * rows:(r + 1) * rows, :] = part
        prev = dil


def _from_classes(ref, slab_ref, mid_ref, dil, ratio):
    if dil == 1:
        return ref[...].astype(F32)
    rows = ref.shape[0]
    tm, width = rows * dil, ref.shape[1] // dil
    n_slab = width // LANES
    assert dil in (ratio, ratio * ratio)
    prev = dil // ratio
    for rp in range(prev):
        for r2 in range(ratio):
            r = r2 * prev + rp
            for sb in range(n_slab):
                col = r * width + sb * LANES
                part = ref[:, col:col + LANES].astype(F32)
                if prev == 1:
                    slab_ref[sb, pl.ds(r2, rows, stride=ratio), :] = part
                else:
                    mid_ref[sb, pl.ds(rp * (tm // prev) + r2, rows, stride=ratio), :] = part
    if prev > 1:
        for rp in range(prev):
            for sb in range(n_slab):
                slab_ref[sb, pl.ds(rp, tm // prev, stride=prev), :] = mid_ref[sb, rp * (tm // prev):(rp + 1) * (tm // prev), :]
    return jnp.concatenate([slab_ref[sb] for sb in range(n_slab)], axis=-1)


def _in_proj_kernel(x_ref, g_ref, w_ref, *refs, plain_dtypes, n_attn, dils):
    out_refs, slab_refs = refs[:-2], refs[-2:]
    x = x_ref[...]
    r = lax.rsqrt(jnp.mean(x * x, axis=-1, keepdims=True) + NORM_EPS)
    h = ((x * r) * g_ref[...]).astype(BF16)
    width = out_refs[0].shape[-1]
    n_plain = len(plain_dtypes)
    for j in list(range(n_plain, n_plain + n_attn)) + list(range(n_plain)):
        res = jnp.dot(h, w_ref[:, j * width:(j + 1) * width], preferred_element_type=F32)
        if j < n_plain:
            out_refs[j][...] = res.astype(out_refs[j].dtype)
        else:
            first = n_plain + (j - n_plain) * len(dils)
            _to_classes(res, out_refs[first:first + len(dils)], dils, slab_refs)


def _in_proj(x2, gain, w_bf16, plain_dtypes, n_attn, dils, tile_m):
    n, d = x2.shape
    width = w_bf16.shape[1] // (len(plain_dtypes) + n_attn)
    out_specs = [pl.BlockSpec((tile_m, width), lambda i: (i, 0)) for _ in plain_dtypes]
    out_shape = [jax.ShapeDtypeStruct((n, width), dt) for dt in plain_dtypes]
    for _ in range(n_attn):
        for dil in dils:
            out_specs.append(pl.BlockSpec((tile_m // dil, dil * width), lambda i: (i, 0)))
            out_shape.append(jax.ShapeDtypeStruct((n // dil, dil * width), BF16))
    return pl.pallas_call(
        functools.partial(_in_proj_kernel, plain_dtypes=plain_dtypes, n_attn=n_attn, dils=dils),
        grid=(n // tile_m,),
        in_specs=[pl.BlockSpec((tile_m, d), lambda i: (i, 0)),
                  pl.BlockSpec((1, d), lambda i: (0, 0)),
                  pl.BlockSpec(w_bf16.shape, lambda i: (0, 0))],
        out_specs=out_specs,
        out_shape=out_shape,
        scratch_shapes=[pltpu.VMEM((width // LANES, tile_m, LANES), F32) for _ in range(2)],
        compiler_params=_cparams(("arbitrary",)),
        name="in_proj",
    )(x2, gain, w_bf16)


def _hgrn_kernel(q_ref, f_ref, i_ref, g_ref, lb_ref, gain_ref, o_ref, state_ref, *, n_heads):
    hd = HGRN_HEAD_DIM
    width = n_heads * hd
    nblk = HGRN_CHUNK // SUBLANES

    @pl.when(pl.program_id(1) == 0)
    def _():
        state_ref[...] = jnp.zeros_like(state_ref)

    lb = lb_ref[...]
    gain = gain_ref[...]
    row8 = lax.broadcasted_iota(jnp.int32, (SUBLANES, width), 0)
    rowc = lax.broadcasted_iota(jnp.int32, (SUBLANES, HGRN_CHUNK), 0)
    colc = lax.broadcasted_iota(jnp.int32, (SUBLANES, HGRN_CHUNK), 1)
    r64 = lax.broadcasted_iota(jnp.int32, (HGRN_CHUNK, HGRN_CHUNK), 0)
    c64 = lax.broadcasted_iota(jnp.int32, (HGRN_CHUNK, HGRN_CHUNK), 1)
    same32 = (r64 // 32) == (c64 // 32)
    same16 = (r64 // 16) == (c64 // 16)

    def scan8(x):
        for s in (1, 2, 4):
            x = x + jnp.where(row8 >= s, pltpu.roll(x, s, axis=0), 0.0)
        return x

    def cat(blocks):
        return jnp.concatenate(blocks, axis=0)

    def chunk(ci, carry):
        r0 = pl.multiple_of(ci * HGRN_CHUNK, HGRN_CHUNK)
        rows = pl.ds(r0, HGRN_CHUNK)
        q = q_ref[0, rows, :].astype(F32)
        v = i_ref[0, rows, :]
        f = lb + (1.0 - lb) * jax.nn.sigmoid(f_ref[0, rows, :])
        logf = jnp.log2(f)
        kk = 1.0 - f

        qb = [q[SUBLANES * i:SUBLANES * (i + 1)] for i in range(nblk)]
        kb = [kk[SUBLANES * i:SUBLANES * (i + 1)] for i in range(nblk)]
        b8 = [scan8(logf[SUBLANES * i:SUBLANES * (i + 1)]) for i in range(nblk)]
        t8 = [jnp.broadcast_to(x[SUBLANES - 1:SUBLANES], x.shape) for x in b8]
        b16 = [b8[i] + t8[i - 1] if i % 2 else b8[i] for i in range(nblk)]
        t16 = [t8[i - i % 2] + t8[i - i % 2 + 1] for i in range(nblk)]
        b32 = [b16[i] + t16[i - 2] if (i // 2) % 2 else b16[i] for i in range(nblk)]
        t32 = [t16[i - i % 4] + t16[i - i % 4 + 2] for i in range(nblk)]
        b64 = [b32[i] + t32[0] if i >= 4 else b32[i] for i in range(nblk)]
        t64 = t32[0] + t32[4]

        zero = jnp.zeros_like(qb[0])

        def q_side(bl, span):
            return cat([qb[i] * jnp.exp2(bl[i]) if (i * SUBLANES // span) % 2 else zero for i in range(nblk)])

        def k_side(bl, tl, span):
            return cat([zero if (i * SUBLANES // span) % 2 else kb[i] * jnp.exp2(tl[i] - bl[i]) for i in range(nblk)])

        q64 = cat([qb[i] * jnp.exp2(b64[i]) for i in range(nblk)]).astype(BF16)
        k64 = cat([kb[i] * jnp.exp2(t64 - b64[i]) for i in range(nblk)]).astype(BF16)
        q32, k32 = q_side(b32, 32).astype(BF16), k_side(b32, t32, 32).astype(BF16)
        q16, k16 = q_side(b16, 16).astype(BF16), k_side(b16, t16, 16).astype(BF16)
        q8, k8 = q_side(b8, 8).astype(BF16), k_side(b8, t8, 8).astype(BF16)
        decay = jnp.exp2(t64[0:1])

        diag = [[jnp.zeros((SUBLANES, HGRN_CHUNK), F32) for _ in range(nblk)] for _ in range(n_heads)]
        for j in range(SUBLANES):
            for i in range(nblk):
                if j == 0:
                    p = qb[i] * kb[i]
                else:
                    p = qb[i] * pltpu.roll(kb[i], j, axis=0) * jnp.exp2(b8[i] - pltpu.roll(b8[i], j, axis=0))
                hit = (colc == rowc + (SUBLANES * i - j)) & (rowc >= j)
                for h in range(n_heads):
                    w = jnp.sum(p[:, h * hd:(h + 1) * hd], axis=-1, keepdims=True)
                    diag[h][i] = jnp.where(hit, w, diag[h][i])

        outs = []
        for h in range(n_heads):
            sl = slice(h * hd, (h + 1) * hd)
            a = cat(diag[h])
            a = a + lax.dot_general(q32[:, sl], k32[:, sl], _NT, preferred_element_type=F32)
            a = a + jnp.where(same32, lax.dot_general(q16[:, sl], k16[:, sl], _NT, preferred_element_type=F32), 0.0)
            a = a + jnp.where(same16, lax.dot_general(q8[:, sl], k8[:, sl], _NT, preferred_element_type=F32), 0.0)
            st = state_ref[h]
            o = lax.dot_general(q64[:, sl], st.astype(BF16), _NT, preferred_element_type=F32)
            o = o + jnp.dot(a.astype(BF16), v[:, sl], preferred_element_type=F32)
            state_ref[h] = st * decay[:, sl] + lax.dot_general(v[:, sl], k64[:, sl], _TN,
                                                              preferred_element_type=F32)
            o = o * lax.rsqrt(jnp.mean(o * o, axis=-1, keepdims=True) + NORM_EPS)
            outs.append(o)
        o = jnp.concatenate(outs, axis=-1) * gain
        o_ref[0, rows, :] = (o * jax.nn.silu(g_ref[0, rows, :].astype(F32))).astype(o_ref.dtype)
        return carry

    lax.fori_loop(0, q_ref.shape[1] // HGRN_CHUNK, chunk, 0, unroll=4)


def _hgrn2(hq, hf, hi, hg, lb, gain, tile_s):
    b, s, width = hq.shape
    n_heads = width // HGRN_HEAD_DIM
    blk = pl.BlockSpec((1, tile_s, width), lambda bi, si: (bi, si, 0))
    vec = pl.BlockSpec((1, width), lambda bi, si: (0, 0))
    return pl.pallas_call(
        functools.partial(_hgrn_kernel, n_heads=n_heads),
        grid=(b, s // tile_s),
        in_specs=[blk, blk, blk, blk, vec, vec],
        out_specs=blk,
        out_shape=jax.ShapeDtypeStruct((b, s, width), BF16),
        scratch_shapes=[pltpu.VMEM((n_heads, HGRN_HEAD_DIM, HGRN_HEAD_DIM), F32)],
        compiler_params=_cparams(("arbitrary", "arbitrary")),
        name="hgrn2",
    )(hq, hf, hi, hg, lb, gain)


def _attn_kernel(qc_ref, kc_ref, kp_ref, vc_ref, vp_ref, o_ref, lse_ref, *, group, scale):
    blk = ATTN_BLOCK
    n_pairs = qc_ref.shape[-1] // LANES
    lane = lax.broadcasted_iota(jnp.int32, (1, LANES), 1)
    low = lane < ATTN_HEAD_DIM
    qi = lax.broadcasted_iota(jnp.int32, (blk, 2 * blk), 0)
    ki = lax.broadcasted_iota(jnp.int32, (blk, 2 * blk), 1)
    band = ((ki >= blk) & (ki - blk <= qi)) | ((ki < blk) & (ki >= qi))
    first_key = jnp.where(pl.program_id(2) == 0, blk, 0)
    band_first = band & (ki >= first_key)

    for g in range(group):
        rows = slice(g * blk, (g + 1) * blk)
        prev = slice((g - 1) * blk, g * blk)
        mask = band_first if g == 0 else band
        for pr in range(n_pairs):
            sl = slice(pr * LANES, (pr + 1) * LANES)
            q = qc_ref[0, rows, sl]
            if g == 0:
                k2 = jnp.concatenate([kp_ref[0, :, sl], kc_ref[0, rows, sl]], axis=0)
                v2 = jnp.concatenate([vp_ref[0, :, sl], vc_ref[0, rows, sl]], axis=0)
            else:
                k2 = jnp.concatenate([kc_ref[0, prev, sl], kc_ref[0, rows, sl]], axis=0)
                v2 = jnp.concatenate([vc_ref[0, prev, sl], vc_ref[0, rows, sl]], axis=0)
            acc = None
            lse = None
            for half in (low, jnp.logical_not(low)):
                qh = jnp.where(half, q, jnp.zeros_like(q))
                vh = jnp.where(half, v2, jnp.zeros_like(v2))
                s = lax.dot_general(qh, k2, _NT, preferred_element_type=F32) * scale
                s = jnp.where(mask, s, -jnp.inf)
                m = jnp.max(s, axis=-1, keepdims=True)
                p = jnp.exp(s - m)
                den = jnp.sum(p, axis=-1, keepdims=True)
                oh = jnp.dot(p.astype(BF16), vh, preferred_element_type=F32) / den
                lh = m + jnp.log(den)
                acc = oh if acc is None else acc + oh
                lse = jnp.broadcast_to(lh, (blk, LANES)) if lse is None else jnp.where(low, lse, lh)
            o_ref[0, rows, sl] = acc.astype(o_ref.dtype)
            lse_ref[0, rows, sl] = lse


def _dilated_attention(aq, ak, av, dil, group):
    b, m, width = aq.shape
    width //= dil
    nb = m // ATTN_BLOCK
    blocks = group
    group = min(blocks, nb)
    classes = min(dil, blocks // group)
    tq = group * ATTN_BLOCK
    cur = pl.BlockSpec((1, tq, classes * width), lambda bi, r, n: (bi, n, r))
    prv = pl.BlockSpec((1, ATTN_BLOCK, classes * width), lambda bi, r, n: (bi, jnp.maximum(n * group - 1, 0), r))
    o, lse = pl.pallas_call(
        functools.partial(_attn_kernel, group=group, scale=ATTN_HEAD_DIM ** -0.5),
        grid=(b, dil // classes, nb // group),
        in_specs=[cur, cur, prv, cur, prv],
        out_specs=[cur, cur],
        out_shape=[jax.ShapeDtypeStruct((b, m, dil * width), BF16),
                   jax.ShapeDtypeStruct((b, m, dil * width), F32)],
        compiler_params=_cparams(("arbitrary", "arbitrary", "arbitrary")),
        name=f"dilated_attn_d{dil}",
    )(aq, ak, ak, av, av)
    return o, lse


def _split_bf16(v):
    hi = v.astype(BF16)
    return hi, (v - hi.astype(F32)).astype(BF16)


def _out_proj_kernel(oa_ref, *refs, dils, tile_t):
    n_pat = len(dils)
    o_refs, l_refs = refs[:n_pat], refs[n_pat:2 * n_pat]
    (x_ref, gb_ref, hm_ref, wa_ref, wb_ref, g2_ref, rwh_ref, rwl_ref, rb_ref,
     x1_ref, h2_ref) = refs[2 * n_pat:2 * n_pat + 11]
    route_refs = refs[2 * n_pat + 11:2 * n_pat + 19]
    carry_ref = refs[2 * n_pat + 19]
    slabs = refs[2 * n_pat + 20:]
    ratio = min(d for d in dils if d > 1)

    @pl.when(pl.program_id(0) == 0)
    def _():
        carry_ref[...] = jnp.zeros_like(carry_ref)

    outs = [_from_classes(o_refs[p], slabs[4 * p], slabs[4 * p + 1], dils[p], ratio) for p in range(n_pat)]
    lses = [_from_classes(l_refs[p], slabs[4 * p + 2], slabs[4 * p + 3], dils[p], ratio) for p in range(n_pat)]
    mx = functools.reduce(jnp.maximum, lses)
    es = [jnp.exp(l - mx) for l in lses]
    ob = sum(e * o for e, o in zip(es, outs)) / sum(es)
    sq_hi, sq_lo = _split_bf16(ob * ob)
    ms = (jnp.dot(sq_hi, hm_ref[...], preferred_element_type=F32)
          + jnp.dot(sq_lo, hm_ref[...], preferred_element_type=F32))
    obn = (ob * lax.rsqrt(ms + NORM_EPS) * gb_ref[...]).astype(BF16)
    y = jnp.dot(oa_ref[...], wa_ref[...], preferred_element_type=F32)
    y = y + jnp.dot(obn, wb_ref[...], preferred_element_type=F32)
    x1 = x_ref[...] + y
    x1_ref[...] = x1
    h2 = (x1 * lax.rsqrt(jnp.mean(x1 * x1, axis=-1, keepdims=True) + NORM_EPS)) * g2_ref[...]
    h_hi, h_lo = _split_bf16(h2)
    h2_ref[...] = h_hi
    lg = (jnp.dot(h_hi, rwh_ref[...], preferred_element_type=F32)
          + jnp.dot(h_hi, rwl_ref[...], preferred_element_type=F32)
          + jnp.dot(h_lo, rwh_ref[...], preferred_element_type=F32)) + rb_ref[...]
    for u in range(lg.shape[0] // tile_t):
        _route_tile(lg[u * tile_t:(u + 1) * tile_t].T, u, *route_refs, carry_ref)


def _out_proj(oa, obs, lses, dils, x2, gain_b, head_mean, w_a, w_b, gain2, rw_hi, rw_lo, rb, tile_m, tile_t,
              n_experts):
    n, d = x2.shape
    wm = oa.shape[1]
    nt, per_step = n // tile_t, tile_m // tile_t
    ne = pl.cdiv(n_experts, SUBLANES) * SUBLANES
    row = lambda w: pl.BlockSpec((tile_m, w), lambda i: (i, 0))
    cls = [pl.BlockSpec((tile_m // dil, dil * wm), lambda i: (i, 0)) for dil in dils]
    full = lambda a: pl.BlockSpec(a.shape, lambda i: (0, 0))
    tab = pl.BlockSpec((per_step, ne, LANES), lambda i: (i, 0, 0))
    tab_shape = jax.ShapeDtypeStruct((nt, ne, LANES), jnp.int32)
    consts = (gain_b, head_mean, w_a, w_b, gain2, rw_hi, rw_lo, rb)
    return pl.pallas_call(
        functools.partial(_out_proj_kernel, dils=dils, tile_t=tile_t),
        grid=(n // tile_m,),
        in_specs=[row(wm)] + cls + cls + [row(d)] + [full(a) for a in consts],
        out_specs=[row(d), row(d), row(LANES), row(LANES),
                   pl.BlockSpec((per_step * SUBLANES, tile_t), lambda i: (i, 0)), tab, tab, tab, tab,
                   pl.BlockSpec((ne, LANES), lambda i: (0, 0))],
        out_shape=[jax.ShapeDtypeStruct((n, d), F32), jax.ShapeDtypeStruct((n, d), BF16),
                   jax.ShapeDtypeStruct((n, LANES), F32), jax.ShapeDtypeStruct((n, LANES), jnp.int32),
                   jax.ShapeDtypeStruct((nt * SUBLANES, tile_t), jnp.int32), tab_shape, tab_shape, tab_shape,
                   tab_shape, jax.ShapeDtypeStruct((ne, LANES), jnp.int32)],
        scratch_shapes=[pltpu.VMEM((ne, LANES), F32)]
        + [pltpu.VMEM((wm // LANES, tile_m, LANES), F32) for _ in range(4 * len(dils))],
        compiler_params=_cparams(("arbitrary",)),
        name="out_proj",
    )(oa, *obs, *lses, x2, *consts)


def _route_tile(logits_t, u, gate_ref, pos_ref, post_ref, start_ref, nch_ref, nfull_ref, off_ref, cnt_ref,
                carry_ref):
    ne, tt = carry_ref.shape[0], logits_t.shape[1]

    logits = logits_t[0:ne, :]
    erow = lax.broadcasted_iota(jnp.int32, (ne, tt), 0).astype(F32)
    vals, idxs = [], []
    for _ in range(TOP_K):
        m = jnp.max(logits, axis=0, keepdims=True)
        ix = jnp.min(jnp.where(logits == m, erow, float(ne)), axis=0, keepdims=True)
        vals.append(m)
        idxs.append(ix)
        logits = jnp.where(erow == ix, -jnp.inf, logits)
    exps = [jnp.exp(v - vals[0]) for v in vals]
    den = exps[0] + exps[1] + exps[2] + exps[3]
    chosen = jnp.zeros((ne, tt), F32)
    for ix in idxs:
        chosen = chosen + jnp.where(erow == ix, 1.0, 0.0)
    s = lax.broadcasted_iota(jnp.int32, (tt, tt), 0)
    t = lax.broadcasted_iota(jnp.int32, (tt, tt), 1)
    before = jnp.dot(chosen.astype(BF16), jnp.where(s < t, 1.0, 0.0).astype(BF16), preferred_element_type=F32)
    count = jnp.broadcast_to(jnp.sum(chosen, axis=1, keepdims=True), (ne, LANES))
    carry = carry_ref[...]
    head = carry - MOE_CHUNK * jnp.floor(carry * (1.0 / MOE_CHUNK))
    nfull = jnp.floor((head + count) * (1.0 / MOE_CHUNK))
    nch = jnp.floor((head + count + (MOE_CHUNK - 1)) * (1.0 / MOE_CHUNK))
    er = lax.broadcasted_iota(jnp.int32, (ne, ne), 0)
    ec = lax.broadcasted_iota(jnp.int32, (ne, ne), 1)
    off = MOE_CHUNK * jnp.dot(jnp.where(ec < er, 1.0, 0.0).astype(BF16), nch.astype(BF16),
                              preferred_element_type=F32)
    slot = before + jnp.concatenate([off + head] * (tt // LANES), axis=1)
    krow = lax.broadcasted_iota(jnp.int32, (LANES, tt), 0)
    gate_t = jnp.zeros((LANES, tt), F32)
    pos_t = jnp.zeros((LANES, tt), F32)
    for k in range(TOP_K):
        pk = jnp.sum(jnp.where(erow == idxs[k], slot, 0.0), axis=0, keepdims=True)
        gate_t = jnp.where(krow == k, exps[k] / den, gate_t)
        pos_t = jnp.where(krow == k, pk, pos_t)
    gate_ref[u * tt:(u + 1) * tt, :] = gate_t.T
    pos_ref[u * tt:(u + 1) * tt, :] = pos_t.T.astype(jnp.int32)
    post_ref[u * SUBLANES:(u + 1) * SUBLANES, :] = pos_t[0:SUBLANES].astype(jnp.int32)
    start_ref[u] = (carry - head).astype(jnp.int32)
    nch_ref[u] = nch.astype(jnp.int32)
    nfull_ref[u] = nfull.astype(jnp.int32)
    off_ref[u] = off.astype(jnp.int32)
    carry_ref[...] = carry + count
    cnt_ref[...] = (carry + count).astype(jnp.int32)


def _for_each_chunk(tab_ref, n, fn):
    def one(c):
        fn(pl.multiple_of(c * MOE_CHUNK, MOE_CHUNK), pl.multiple_of(tab_ref[0, 0, c], MOE_CHUNK))

    def four(q, carry):
        for u in range(4):
            one(q * 4 + u)
        return carry

    def single(c, carry):
        one(c)
        return carry

    lax.fori_loop(0, lax.shift_right_logical(n, 2), four, 0)
    lax.fori_loop(n & ~3, n, single, 0)


def _pad_sizes():
    sizes, s = [], MOE_ROWS // 2
    while s >= MOE_CHUNK:
        sizes.append(s)
        s //= 2
    return sizes


def _scatter_kernel(nch_ref, nfull_ref, off_ref, total_ref, padrow_ref, padlen_ref, nused_ref, post_ref, tab_ref,
                    tabp_ref, h_ref, xout_ref, stage_ref, open_ref, zero_ref, sems, zsem, *, n_experts):
    rs, tt = stage_ref.shape[1], h_ref.shape[0]
    i, nt = pl.program_id(0), pl.num_programs(0)
    slot = i % 2
    base = i * n_experts
    stage = stage_ref.at[slot]

    @pl.when(i == 0)
    def _():
        open_ref[...] = jnp.zeros_like(open_ref)
        zero_ref[...] = jnp.zeros_like(zero_ref)
        stage_ref[...] = jnp.zeros_like(stage_ref)

    post = post_ref[...]
    main = _stage_main_rows(tt, n_experts)

    def select_rows(r0, r1):
        srow = lax.broadcasted_iota(jnp.int32, (r1 - r0, tt), 0) + r0
        hit = srow == post[0:1]
        for k in range(1, TOP_K):
            hit = hit | (srow == post[k:k + 1])
        sel = jnp.where(hit, 1.0, 0.0).astype(BF16)
        stage[r0:r1, :] = jnp.dot(sel, h_ref[...], preferred_element_type=F32)

    select_rows(0, main)
    if main < rs:
        @pl.when(total_ref[i] * MOE_CHUNK > main)
        def _():
            select_rows(main, rs)

    def add_open(e, carry):
        rows = pl.ds(pl.multiple_of(off_ref[base + e], MOE_CHUNK), MOE_CHUNK)
        stage[rows, :] = stage[rows, :] + open_ref[e]
        return carry

    def save_open(e, carry):
        nfull = nfull_ref[base + e]
        rows = pl.ds(pl.multiple_of(off_ref[base + e] + nfull * MOE_CHUNK, MOE_CHUNK), MOE_CHUNK)
        still_open = nch_ref[base + e] > nfull
        open_ref[e] = jnp.where(still_open, stage[rows, :], 0.0)
        return carry

    def chunk_copy(sl, srow0, xrow0):
        return pltpu.make_async_copy(stage_ref.at[sl, pl.ds(srow0, MOE_CHUNK)],
                                     xout_ref.at[pl.ds(xrow0, MOE_CHUNK)], sems.at[sl])

    def pad_copies(fn):
        def zeros_to(start, size):
            fn(pltpu.make_async_copy(zero_ref.at[pl.ds(0, size)], xout_ref.at[pl.ds(start, size)], zsem))

        def expert(e, carry):
            row, length = pl.multiple_of(padrow_ref[e], MOE_CHUNK), padlen_ref[e]
            for size in _pad_sizes():
                @pl.when((length & size) != 0)
                def _():
                    zeros_to(pl.multiple_of(row + (length & ~(2 * size - 1)), MOE_CHUNK), size)
            return carry

        def unused_half_block(c, carry):
            zeros_to(pl.multiple_of(c * (MOE_ROWS // 2), MOE_ROWS // 2), MOE_ROWS // 2)
            return carry

        lax.fori_loop(0, n_experts, expert, 0)
        lax.fori_loop(2 * nused_ref[0], xout_ref.shape[0] // (MOE_ROWS // 2), unused_half_block, 0)

    lax.fori_loop(0, n_experts, add_open, 0, unroll=8)

    @pl.when(i > 0)
    def _():
        _for_each_chunk(tabp_ref, total_ref[i - 1], lambda s0, x0: chunk_copy(1 - slot, s0, x0).wait())

    _for_each_chunk(tab_ref, total_ref[i], lambda s0, x0: chunk_copy(slot, s0, x0).start())
    lax.fori_loop(0, n_experts, save_open, 0, unroll=8)

    @pl.when(i == nt - 1)
    def _():
        pad_copies(lambda cp: cp.start())
        _for_each_chunk(tab_ref, total_ref[i], lambda s0, x0: chunk_copy(slot, s0, x0).wait())
        pad_copies(lambda cp: cp.wait())


def _stage_main_rows(tile_t, n_experts):
    return min(pl.cdiv(tile_t * TOP_K + n_experts * (MOE_CHUNK - 1), LANES) * LANES, _stage_rows(tile_t, n_experts))


def _stage_rows(tile_t, n_experts):
    rows = tile_t * TOP_K + 2 * n_experts * (MOE_CHUNK - 1) + MOE_CHUNK
    return pl.cdiv(rows, LANES) * LANES


def _chunk_table_spec(table, shift):
    nt, _, width = table.shape
    return pl.BlockSpec((1, 1, width), lambda i, *_: (jnp.clip(i + shift, 0, nt - 1), 0, 0), memory_space=pltpu.SMEM)


def _moe_scatter(nch, nfull, off, total, padrow, padlen, n_used, post, table, h2, n_rows, tile_t, n_experts):
    n, d = h2.shape
    tab = lambda shift: _chunk_table_spec(table, shift)
    return pl.pallas_call(
        functools.partial(_scatter_kernel, n_experts=n_experts),
        grid_spec=pltpu.PrefetchScalarGridSpec(
            num_scalar_prefetch=7,
            grid=(n // tile_t,),
            in_specs=[pl.BlockSpec((SUBLANES, tile_t), lambda i, *_: (i, 0)), tab(0), tab(-1),
                      pl.BlockSpec((tile_t, d), lambda i, *_: (i, 0))],
            out_specs=pl.BlockSpec(memory_space=pl.ANY),
            scratch_shapes=[pltpu.VMEM((2, _stage_rows(tile_t, n_experts), d), F32),
                            pltpu.VMEM((n_experts, MOE_CHUNK, d), F32),
                            pltpu.VMEM((MOE_ROWS // 2, d), F32),
                            pltpu.SemaphoreType.DMA((2,)), pltpu.SemaphoreType.DMA]),
        out_shape=jax.ShapeDtypeStruct((n_rows, d), F32),
        compiler_params=_cparams(("arbitrary",)),
        name="moe_scatter",
    )(nch, nfull, off, total, padrow, padlen, n_used, post, table, table, h2)


def _expert_kernel(be_ref, nu_ref, valid_ref, slot_ref, next_ref, x_ref, bg_ref, bl_ref, bd_ref, wu_hbm, wd_hbm,
                   y_ref, wu_buf, wd_buf, wg_s, wl_s, wd_s, sems):
    i = pl.program_id(0)
    used = i < nu_ref[0]
    e = be_ref[i]
    new_expert = (i == 0) | (e != be_ref[jnp.maximum(i - 1, 0)])
    slot = slot_ref[e]

    def weight_copies(expert, sl):
        return (pltpu.make_async_copy(wu_hbm.at[expert], wu_buf.at[sl], sems.at[0, sl]),
                pltpu.make_async_copy(wd_hbm.at[expert], wd_buf.at[sl], sems.at[1, sl]))

    @pl.when(used & (i == 0))
    def _():
        for cp in weight_copies(e, slot):
            cp.start()

    @pl.when(used & new_expert)
    def _():
        for cp in weight_copies(e, slot):
            cp.wait()
        nxt = next_ref[e]

        @pl.when(nxt >= 0)
        def _():
            for cp in weight_copies(nxt, 1 - slot):
                cp.start()

        r = lax.broadcasted_iota(jnp.int32, (2 * LANES, 2 * LANES), 0)
        c = lax.broadcasted_iota(jnp.int32, (2 * LANES, 2 * LANES), 1)
        pick = jnp.where(r == jnp.where(c < LANES, 2 * c, 2 * (c - LANES) + 1), 1.0, 0.0).astype(BF16)
        for cb in range(wg_s.shape[1] // LANES):
            grp = wu_buf[slot, :, cb * 2 * LANES:(cb + 1) * 2 * LANES].astype(BF16)
            out = slice(cb * LANES, (cb + 1) * LANES)
            both = jnp.dot(grp, pick, preferred_element_type=F32).astype(BF16)
            wg_s[:, out] = both[:, :LANES]
            wl_s[:, out] = both[:, LANES:]
        wd_s[...] = wd_buf[slot].astype(BF16)

    def ffn(rows):
        x = x_ref[rows, :].astype(BF16)
        glu = jnp.dot(x, wg_s[...], preferred_element_type=F32) + bg_ref[0]
        lin = jnp.dot(x, wl_s[...], preferred_element_type=F32) + bl_ref[0]
        glu = jnp.minimum(glu, SWIGLU_LIMIT)
        lin = jnp.clip(lin, -SWIGLU_LIMIT, SWIGLU_LIMIT)
        h = glu * jax.nn.sigmoid(SWIGLU_ALPHA * glu) * (lin + 1.0)
        y_ref[rows, :] = jnp.dot(h.astype(BF16), wd_s[...], preferred_element_type=F32) + bd_ref[0]

    half = MOE_ROWS // 2
    full = used & (valid_ref[i] > half)

    @pl.when(full)
    def _():
        ffn(slice(0, MOE_ROWS))

    @pl.when(used & jnp.logical_not(full))
    def _():
        ffn(slice(0, half))
        y_ref[half:, :] = jnp.zeros((MOE_ROWS - half, y_ref.shape[1]), y_ref.dtype)

    @pl.when(jnp.logical_not(used))
    def _():
        y_ref[...] = jnp.zeros_like(y_ref)


def _moe_experts(block_e, n_used, valid, buf_slot, next_expert, xbuf, w_up, bg, bl, w_down, bd):
    n_rows, d = xbuf.shape
    de = w_down.shape[1]
    nblk = n_rows // MOE_ROWS
    rows = pl.BlockSpec((MOE_ROWS, d), lambda i, be, nu, *_: (i, 0))
    rows_in = pl.BlockSpec((MOE_ROWS, d), lambda i, be, nu, *_: (jnp.minimum(i, jnp.maximum(nu[0] - 1, 0)), 0))
    bias = lambda a: pl.BlockSpec((1,) + a.shape[1:], lambda i, be, *_: (be[i], 0, 0))
    hbm = pl.BlockSpec(memory_space=pl.ANY)
    return pl.pallas_call(
        _expert_kernel,
        grid_spec=pltpu.PrefetchScalarGridSpec(
            num_scalar_prefetch=5,
            grid=(nblk,),
            in_specs=[rows_in, bias(bg), bias(bl), bias(bd), hbm, hbm],
            out_specs=rows,
            scratch_shapes=[pltpu.VMEM((2,) + w_up.shape[1:], F32), pltpu.VMEM((2,) + w_down.shape[1:], F32),
                            pltpu.VMEM((d, de), BF16), pltpu.VMEM((d, de), BF16), pltpu.VMEM((de, d), BF16),
                            pltpu.SemaphoreType.DMA((2, 2))]),
        out_shape=jax.ShapeDtypeStruct((n_rows, d), F32),
        compiler_params=_cparams(("arbitrary",)),
        name="moe_experts",
    )(block_e, n_used, valid, buf_slot, next_expert, xbuf, bg, bl, bd, w_up, w_down)


def _combine_kernel(total_ref, pos_ref, gate_ref, x1_ref, g_ref, tab_ref, tabn_ref, y_ref, o_ref, stage_ref, acc_ref,
                    sems, *, n_experts):
    rs, tt = stage_ref.shape[1], x1_ref.shape[0]
    i, nt = pl.program_id(0), pl.num_programs(0)
    slot = i % 2

    def chunk_copy(sl, srow0, yrow0):
        return pltpu.make_async_copy(y_ref.at[pl.ds(yrow0, MOE_CHUNK)],
                                     stage_ref.at[sl, pl.ds(srow0, MOE_CHUNK)], sems.at[sl])

    @pl.when(i == 0)
    def _():
        stage_ref[...] = jnp.zeros_like(stage_ref)
        _for_each_chunk(tab_ref, total_ref[i], lambda s0, y0: chunk_copy(slot, s0, y0).start())

    @pl.when(i + 1 < nt)
    def _():
        _for_each_chunk(tabn_ref, total_ref[i + 1], lambda s0, y0: chunk_copy(1 - slot, s0, y0).start())

    pos, gates = pos_ref[...], gate_ref[...]
    main = _stage_main_rows(tt, n_experts)

    def weighted_rows(r0, r1):
        scol = lax.broadcasted_iota(jnp.int32, (tt, r1 - r0), 1) + r0
        weights = jnp.zeros((tt, r1 - r0), F32)
        for k in range(TOP_K):
            weights = jnp.where(scol == pos[:, k:k + 1], gates[:, k:k + 1], weights)
        return jnp.dot(weights.astype(BF16), stage_ref[slot, r0:r1, :].astype(BF16), preferred_element_type=F32)

    _for_each_chunk(tab_ref, total_ref[i], lambda s0, y0: chunk_copy(slot, s0, y0).wait())
    acc_ref[...] = x1_ref[...] + weighted_rows(0, main)
    if main < rs:
        @pl.when(total_ref[i] * MOE_CHUNK > main)
        def _():
            acc_ref[...] += weighted_rows(main, rs)
    acc = acc_ref[...]
    o_ref[...] = (acc * lax.rsqrt(jnp.mean(acc * acc, axis=-1, keepdims=True) + NORM_EPS)) * g_ref[...]


def _moe_combine(total, pos, gates, x1, gain, table, ybuf, tile_t, n_experts):
    n, d = x1.shape
    return pl.pallas_call(
        functools.partial(_combine_kernel, n_experts=n_experts),
        grid_spec=pltpu.PrefetchScalarGridSpec(
            num_scalar_prefetch=1,
            grid=(n // tile_t,),
            in_specs=[pl.BlockSpec((tile_t, LANES), lambda i, *_: (i, 0)),
                      pl.BlockSpec((tile_t, LANES), lambda i, *_: (i, 0)),
                      pl.BlockSpec((tile_t, d), lambda i, *_: (i, 0)),
                      pl.BlockSpec((1, d), lambda i, *_: (0, 0)),
                      _chunk_table_spec(table, 0), _chunk_table_spec(table, 1),
                      pl.BlockSpec(memory_space=pl.ANY)],
            out_specs=pl.BlockSpec((tile_t, d), lambda i, *_: (i, 0)),
            scratch_shapes=[pltpu.VMEM((2, _stage_rows(tile_t, n_experts), d), F32),
                            pltpu.VMEM((tile_t, d), F32), pltpu.SemaphoreType.DMA((2,))]),
        out_shape=jax.ShapeDtypeStruct((n, d), F32),
        compiler_params=_cparams(("arbitrary",)),
        name="moe_combine",
    )(total, pos, gates, x1, gain, table, table, ybuf)


def kernel(x, norm1_g, w_in, hgrn_lb_logits, hgrn_norm_g, attn_norm_g, w_out, norm2_g, router_w, router_b,
           w_up, b_up, w_down, b_down, final_norm_g):
    b, s, d = x.shape
    n = b * s
    depth = w_in.shape[0]
    n_experts = router_w.shape[-1]
    d_mix = w_out.shape[1]
    d_hgrn = d_mix // 2
    d_attn = d_mix - d_hgrn
    assert w_in.shape[-1] == 4 * d_hgrn + 3 * d_attn and d_hgrn == d_attn
    assert all(s % win == 0 and win // dil == ATTN_BLOCK for win, dil in DILATED_PATTERNS)
    assert s % HGRN_ROWS == 0 and n % PROJ_ROWS == 0 and PROJ_ROWS % ROUTE_TOKENS == 0 and n_experts <= LANES
    assert depth == 1, "the final rmsnorm is fused into the single layer's MoE combine"

    lb_all = jnp.cumsum(jax.nn.softmax(hgrn_lb_logits.astype(F32), axis=0), axis=0)
    lane = jnp.arange(d_attn)
    head_mean = jnp.where((lane[:, None] // ATTN_HEAD_DIM) == (lane[None, :] // ATTN_HEAD_DIM),
                          1.0 / ATTN_HEAD_DIM, 0.0).astype(BF16)
    x2 = x.reshape(n, d)
    l = 0
    dils = tuple(dil for _, dil in DILATED_PATTERNS)
    n_pat = len(dils)
    hq, hf, hi, hg, *attn = _in_proj(
        x2, norm1_g[l].reshape(1, d), w_in[l].astype(BF16),
        (BF16, F32, BF16, BF16), 3, dils, tile_m=PROJ_ROWS)
    to3 = lambda t: t.reshape(b, s, -1)
    o_a = _hgrn2(to3(hq), to3(hf), to3(hi), to3(hg), lb_all[l].reshape(1, d_hgrn),
                 hgrn_norm_g[l].reshape(1, d_hgrn), tile_s=HGRN_ROWS)
    obs, lses = [], []
    for p, dil in enumerate(dils):
        aq, ak, av = (attn[a * n_pat + p].reshape(b, s // dil, dil * d_attn) for a in range(3))
        o_p, lse_p = _dilated_attention(aq, ak, av, dil, group=ATTN_BLOCKS)
        obs.append(o_p.reshape(n // dil, dil * d_attn))
        lses.append(lse_p.reshape(n // dil, dil * d_attn))
    rw = jnp.zeros((d, LANES), F32).at[:, :n_experts].set(router_w[l])
    rw_hi = rw.astype(BF16)
    rw_lo = (rw - rw_hi.astype(F32)).astype(BF16)
    rb = jnp.full((1, LANES), -jnp.inf, F32).at[0, :n_experts].set(router_b[l])
    w_o = w_out[l].astype(BF16)
    tile_t = ROUTE_TOKENS
    x1, h2, gates, pos, post, start, nch, nfull, off, counts = _out_proj(
        o_a.reshape(n, d_hgrn), obs, lses, dils, x2, attn_norm_g[l].reshape(1, d_attn), head_mean,
        w_o[:d_hgrn], w_o[d_hgrn:], norm2_g[l].reshape(1, d), rw_hi, rw_lo, rb, PROJ_ROWS, tile_t, n_experts)

    counts = counts[:n_experts, 0]
    padded = (counts + MOE_ROWS - 1) // MOE_ROWS * MOE_ROWS
    pad_end = jnp.cumsum(padded)
    pad_start = pad_end - padded
    n_rows = n * TOP_K + n_experts * MOE_ROWS
    nblk = n_rows // MOE_ROWS
    block_row = jnp.arange(nblk, dtype=jnp.int32) * MOE_ROWS
    block_e = jnp.minimum(jnp.sum(pad_end[None, :] <= block_row[:, None], axis=1), n_experts - 1).astype(jnp.int32)
    n_used = (pad_end[-1:] // MOE_ROWS).astype(jnp.int32)
    eid = jnp.arange(n_experts, dtype=jnp.int32)
    group_end = jnp.sum(jnp.where(eid[None, :] == block_e[:, None], (pad_start + counts)[None, :], 0), axis=1)
    valid = jnp.clip(group_end - block_row, 0, MOE_ROWS).astype(jnp.int32)
    has_rows = padded > 0
    buf_slot = ((jnp.cumsum(has_rows) - 1) % 2).astype(jnp.int32)
    following = jnp.min(jnp.where(has_rows[None, :] & (eid[None, :] > eid[:, None]), eid[None, :], n_experts), axis=1)
    next_expert = jnp.where(following < n_experts, following, -1).astype(jnp.int32)
    per_expert = lambda t: t[:, :n_experts, 0]
    seg = per_expert(start) + pad_start[None, :].astype(jnp.int32)
    nch, nfull, off = per_expert(nch), per_expert(nfull), per_expert(off)
    total = jnp.sum(nch, axis=1).astype(jnp.int32)
    first = off // MOE_CHUNK
    max_chunks = pl.cdiv(_stage_rows(tile_t, n_experts) // MOE_CHUNK, LANES) * LANES
    chunk = jnp.arange(max_chunks, dtype=jnp.int32)[None, :, None]
    owned = (first[:, None, :] <= chunk) & (chunk < (first + nch)[:, None, :])
    table = jnp.sum(jnp.where(owned, (seg - first * MOE_CHUNK)[:, None, :], 0), axis=-1) + chunk[:, :, 0] * MOE_CHUNK
    table = table.astype(jnp.int32)[:, None, :]
    filled = (counts + MOE_CHUNK - 1) // MOE_CHUNK * MOE_CHUNK
    padrow = (pad_start + filled).astype(jnp.int32)
    padlen = (padded - filled).astype(jnp.int32)

    xbuf = _moe_scatter(nch.reshape(-1), nfull.reshape(-1), off.reshape(-1), total, padrow, padlen, n_used, post,
                        table, h2, n_rows, tile_t, n_experts)
    ybuf = _moe_experts(block_e, n_used, valid, buf_slot, next_expert, xbuf, w_up[l], b_up[l][:, None, 0::2],
                        b_up[l][:, None, 1::2], w_down[l], b_down[l][:, None, :])
    out = _moe_combine(total, pos, gates, x1, final_norm_g.reshape(1, d), table, ybuf, tile_t, n_experts)
    return out.reshape(b, s, d)
```

```python
import functools

import jax
import jax.numpy as jnp
from jax import lax
from jax.experimental import pallas as pl
from jax.experimental.pallas import tpu as pltpu

F32 = jnp.float32
BF16 = jnp.bfloat16

NORM_EPS = 1e-6
HGRN_HEAD_DIM = 128
ATTN_HEAD_DIM = 64
DILATED_PATTERNS = ((128, 1), (512, 4), (2048, 16))
TOP_K = 4
SWIGLU_ALPHA = 1.702
SWIGLU_LIMIT = 7.0

LANES = 128
SUBLANES = 8
HGRN_CHUNK = 64
ATTN_BLOCK = 128
MOE_ROWS = 512
MOE_CHUNK = SUBLANES
V7X_VMEM_BYTES = 64 * 1024 * 1024
VMEM_LIMIT = V7X_VMEM_BYTES - 8 * 1024 * 1024

IN_PROJ_ROWS = 1024
PROJ_ROWS = 512
HGRN_ROWS = 1024
ATTN_BLOCKS = 4
ROUTE_TOKENS = 256

_NT = (((1,), (1,)), ((), ()))
_TN = (((0,), (0,)), ((), ()))


def _cparams(sem):
    return pltpu.CompilerParams(dimension_semantics=sem, vmem_limit_bytes=VMEM_LIMIT)


def _to_classes(res, o_refs, dils, slab_refs):
    tm, width = res.shape
    n_slab = width // LANES
    for sb in range(n_slab):
        slab_refs[0][sb] = res[:, sb * LANES:(sb + 1) * LANES]
    prev = 1
    for p, (o_ref, dil) in enumerate(zip(o_refs, dils)):
        if dil == 1:
            o_ref[...] = res.astype(o_ref.dtype)
            continue
        ratio, src, last = dil // prev, slab_refs[0] if prev == 1 else slab_refs[1], p == len(dils) - 1
        assert dil % prev == 0 and (prev == 1 or last), "one intermediate slab: at most two strided levels"
        rows = tm // dil
        for rp in range(prev):
            for r2 in range(ratio):
                r = r2 * prev + rp
                for sb in range(n_slab):
                    part = src[sb, pl.ds(rp * (tm // prev) + r2, rows, stride=ratio), :]
                    col = r * width + sb * LANES
                    o_ref[:, col:col + LANES] = part.astype(o_ref.dtype)
                    if not last:
                        slab_refs[1][sb, r * rows:(r + 1) * rows, :] = part
        prev = dil


def _from_classes(ref, slab_ref, mid_ref, dil, ratio):
    if dil == 1:
        return ref[...].astype(F32)
    rows = ref.shape[0]
    tm, width = rows * dil, ref.shape[1] // dil
    n_slab = width // LANES
    assert dil in (ratio, ratio * ratio)
    prev = dil // ratio
    for rp in range(prev):
        for r2 in range(ratio):
            r = r2 * prev + rp
            for sb in range(n_slab):
                col = r * width + sb * LANES
                part = ref[:, col:col + LANES].astype(F32)
                if prev == 1:
                    slab_ref[sb, pl.ds(r2, rows, stride=ratio), :] = part
                else:
                    mid_ref[sb, pl.ds(rp * (tm // prev) + r2, rows, stride=ratio), :] = part
    if prev > 1:
        for rp in range(prev):
            for sb in range(n_slab):
                slab_ref[sb, pl.ds(rp, tm // prev, stride=prev), :] = mid_ref[sb, rp * (tm // prev):(rp + 1) * (tm // prev), :]
    return jnp.concatenate([slab_ref[sb] for sb in range(n_slab)], axis=-1)


def _in_proj_kernel(x_ref, g_ref, w_ref, *refs, plain_dtypes, n_attn, dils):
    out_refs, slab_refs = refs[:-2], refs[-2:]
    x = x_ref[...]
    r = lax.rsqrt(jnp.mean(x * x, axis=-1, keepdims=True) + NORM_EPS)
    h = ((x * r) * g_ref[...]).astype(BF16)
    width = out_refs[0].shape[-1]
    n_plain = len(plain_dtypes)
    for j in list(range(n_plain, n_plain + n_attn)) + list(range(n_plain)):
        res = jnp.dot(h, w_ref[:, j * width:(j + 1) * width], preferred_element_type=F32)
        if j < n_plain:
            out_refs[j][...] = res.astype(out_refs[j].dtype)
        else:
            first = n_plain + (j - n_plain) * len(dils)
            _to_classes(res, out_refs[first:first + len(dils)], dils, slab_refs)


def _in_proj(x2, gain, w_bf16, plain_dtypes, n_attn, dils, tile_m):
    n, d = x2.shape
    width = w_bf16.shape[1] // (len(plain_dtypes) + n_attn)
    out_specs = [pl.BlockSpec((tile_m, width), lambda i: (i, 0)) for _ in plain_dtypes]
    out_shape = [jax.ShapeDtypeStruct((n, width), dt) for dt in plain_dtypes]
    for _ in range(n_attn):
        for dil in dils:
            out_specs.append(pl.BlockSpec((tile_m // dil, dil * width), lambda i: (i, 0)))
            out_shape.append(jax.ShapeDtypeStruct((n // dil, dil * width), BF16))
    return pl.pallas_call(
        functools.partial(_in_proj_kernel, plain_dtypes=plain_dtypes, n_attn=n_attn, dils=dils),
        grid=(n // tile_m,),
        in_specs=[pl.BlockSpec((tile_m, d), lambda i: (i, 0)),
                  pl.BlockSpec((1, d), lambda i: (0, 0)),
                  pl.BlockSpec(w_bf16.shape, lambda i: (0, 0), pipeline_mode=pl.Buffered(1))],
        out_specs=out_specs,
        out_shape=out_shape,
        scratch_shapes=[pltpu.VMEM((width // LANES, tile_m, LANES), F32) for _ in range(2)],
        compiler_params=_cparams(("arbitrary",)),
        name="in_proj",
    )(x2, gain, w_bf16)


def _hgrn_kernel(q_ref, f_ref, i_ref, g_ref, lb_ref, gain_ref, o_ref, state_ref, *, n_heads):
    hd = HGRN_HEAD_DIM
    width = n_heads * hd
    nblk = HGRN_CHUNK // SUBLANES

    @pl.when(pl.program_id(1) == 0)
    def _():
        state_ref[...] = jnp.zeros_like(state_ref)

    lb = lb_ref[...]
    gain = gain_ref[...]
    row8 = lax.broadcasted_iota(jnp.int32, (SUBLANES, width), 0)
    rowc = lax.broadcasted_iota(jnp.int32, (SUBLANES, HGRN_CHUNK), 0)
    colc = lax.broadcasted_iota(jnp.int32, (SUBLANES, HGRN_CHUNK), 1)
    r64 = lax.broadcasted_iota(jnp.int32, (HGRN_CHUNK, HGRN_CHUNK), 0)
    c64 = lax.broadcasted_iota(jnp.int32, (HGRN_CHUNK, HGRN_CHUNK), 1)
    same32 = (r64 // 32) == (c64 // 32)
    same16 = (r64 // 16) == (c64 // 16)

    def scan8(x):
        for s in (1, 2, 4):
            x = x + jnp.where(row8 >= s, pltpu.roll(x, s, axis=0), 0.0)
        return x

    def cat(blocks):
        return jnp.concatenate(blocks, axis=0)

    def chunk(ci, carry):
        r0 = pl.multiple_of(ci * HGRN_CHUNK, HGRN_CHUNK)
        rows = pl.ds(r0, HGRN_CHUNK)
        q = q_ref[0, rows, :].astype(F32)
        v = i_ref[0, rows, :]
        f = lb + (1.0 - lb) * jax.nn.sigmoid(f_ref[0, rows, :])
        logf = jnp.log2(f)
        kk = 1.0 - f

        qb = [q[SUBLANES * i:SUBLANES * (i + 1)] for i in range(nblk)]
        kb = [kk[SUBLANES * i:SUBLANES * (i + 1)] for i in range(nblk)]
        b8 = [scan8(logf[SUBLANES * i:SUBLANES * (i + 1)]) for i in range(nblk)]
        t8 = [jnp.broadcast_to(x[SUBLANES - 1:SUBLANES], x.shape) for x in b8]
        b16 = [b8[i] + t8[i - 1] if i % 2 else b8[i] for i in range(nblk)]
        t16 = [t8[i - i % 2] + t8[i - i % 2 + 1] for i in range(nblk)]
        b32 = [b16[i] + t16[i - 2] if (i // 2) % 2 else b16[i] for i in range(nblk)]
        t32 = [t16[i - i % 4] + t16[i - i % 4 + 2] for i in range(nblk)]
        b64 = [b32[i] + t32[0] if i >= 4 else b32[i] for i in range(nblk)]
        t64 = t32[0] + t32[4]

        zero = jnp.zeros_like(qb[0])

        def q_side(bl, span):
            return cat([qb[i] * jnp.exp2(bl[i]) if (i * SUBLANES // span) % 2 else zero for i in range(nblk)])

        def k_side(bl, tl, span):
            return cat([zero if (i * SUBLANES // span) % 2 else kb[i] * jnp.exp2(tl[i] - bl[i]) for i in range(nblk)])

        q64 = cat([qb[i] * jnp.exp2(b64[i]) for i in range(nblk)]).astype(BF16)
        k64 = cat([kb[i] * jnp.exp2(t64 - b64[i]) for i in range(nblk)]).astype(BF16)
        q32, k32 = q_side(b32, 32).astype(BF16), k_side(b32, t32, 32).astype(BF16)
        q16, k16 = q_side(b16, 16).astype(BF16), k_side(b16, t16, 16).astype(BF16)
        q8, k8 = q_side(b8, 8).astype(BF16), k_side(b8, t8, 8).astype(BF16)
        decay = jnp.exp2(t64[0:1])

        diag = [[jnp.zeros((SUBLANES, HGRN_CHUNK), F32) for _ in range(nblk)] for _ in range(n_heads)]
        for j in range(SUBLANES):
            for i in range(nblk):
                if j == 0:
                    p = qb[i] * kb[i]
                else:
                    p = qb[i] * pltpu.roll(kb[i], j, axis=0) * jnp.exp2(b8[i] - pltpu.roll(b8[i], j, axis=0))
                hit = (colc == rowc + (SUBLANES * i - j)) & (rowc >= j)
                for h in range(n_heads):
                    w = jnp.sum(p[:, h * hd:(h + 1) * hd], axis=-1, keepdims=True)
                    diag[h][i] = jnp.where(hit, w, diag[h][i])

        outs = []
        for h in range(n_heads):
            sl = slice(h * hd, (h + 1) * hd)
            a = cat(diag[h])
            a = a + lax.dot_general(q32[:, sl], k32[:, sl], _NT, preferred_element_type=F32)
            a = a + jnp.where(same32, lax.dot_general(q16[:, sl], k16[:, sl], _NT, preferred_element_type=F32), 0.0)
            a = a + jnp.where(same16, lax.dot_general(q8[:, sl], k8[:, sl], _NT, preferred_element_type=F32), 0.0)
            st = state_ref[h]
            o = lax.dot_general(q64[:, sl], st.astype(BF16), _NT, preferred_element_type=F32)
            o = o + jnp.dot(a.astype(BF16), v[:, sl], preferred_element_type=F32)
            state_ref[h] = st * decay[:, sl] + lax.dot_general(v[:, sl], k64[:, sl], _TN,
                                                              preferred_element_type=F32)
            o = o * lax.rsqrt(jnp.mean(o * o, axis=-1, keepdims=True) + NORM_EPS)
            outs.append(o)
        o = jnp.concatenate(outs, axis=-1) * gain
        o_ref[0, rows, :] = (o * jax.nn.silu(g_ref[0, rows, :].astype(F32))).astype(o_ref.dtype)
        return carry

    lax.fori_loop(0, q_ref.shape[1] // HGRN_CHUNK, chunk, 0, unroll=8)


def _hgrn2(hq, hf, hi, hg, lb, gain, tile_s):
    b, s, width = hq.shape
    n_heads = width // HGRN_HEAD_DIM
    blk = pl.BlockSpec((1, tile_s, width), lambda bi, si: (bi, si, 0))
    vec = pl.BlockSpec((1, width), lambda bi, si: (0, 0))
    return pl.pallas_call(
        functools.partial(_hgrn_kernel, n_heads=n_heads),
        grid=(b, s // tile_s),
        in_specs=[blk, blk, blk, blk, vec, vec],
        out_specs=blk,
        out_shape=jax.ShapeDtypeStruct((b, s, width), BF16),
        scratch_shapes=[pltpu.VMEM((n_heads, HGRN_HEAD_DIM, HGRN_HEAD_DIM), F32)],
        compiler_params=_cparams(("arbitrary", "arbitrary")),
        name="hgrn2",
    )(hq, hf, hi, hg, lb, gain)


def _attn_kernel(qc_ref, kc_ref, kp_ref, vc_ref, vp_ref, o_ref, lse_ref, *, group, scale):
    blk = ATTN_BLOCK
    n_pairs = qc_ref.shape[-1] // LANES
    lane = lax.broadcasted_iota(jnp.int32, (1, LANES), 1)
    low = lane < ATTN_HEAD_DIM
    qi = lax.broadcasted_iota(jnp.int32, (blk, 2 * blk), 0)
    ki = lax.broadcasted_iota(jnp.int32, (blk, 2 * blk), 1)
    band = ((ki >= blk) & (ki - blk <= qi)) | ((ki < blk) & (ki >= qi))
    first_key = jnp.where(pl.program_id(2) == 0, blk, 0)
    band_first = band & (ki >= first_key)

    for g in range(group):
        rows = slice(g * blk, (g + 1) * blk)
        prev = slice((g - 1) * blk, g * blk)
        mask = band_first if g == 0 else band
        for pr in range(n_pairs):
            sl = slice(pr * LANES, (pr + 1) * LANES)
            q = qc_ref[0, rows, sl]
            if g == 0:
                k2 = jnp.concatenate([kp_ref[0, :, sl], kc_ref[0, rows, sl]], axis=0)
                v2 = jnp.concatenate([vp_ref[0, :, sl], vc_ref[0, rows, sl]], axis=0)
            else:
                k2 = jnp.concatenate([kc_ref[0, prev, sl], kc_ref[0, rows, sl]], axis=0)
                v2 = jnp.concatenate([vc_ref[0, prev, sl], vc_ref[0, rows, sl]], axis=0)
            acc = None
            lse = None
            for half in (low, jnp.logical_not(low)):
                qh = jnp.where(half, q, jnp.zeros_like(q))
                vh = jnp.where(half, v2, jnp.zeros_like(v2))
                s = lax.dot_general(qh, k2, _NT, preferred_element_type=F32) * scale
                s = jnp.where(mask, s, -jnp.inf)
                m = jnp.max(s, axis=-1, keepdims=True)
                p = jnp.exp(s - m)
                den = jnp.sum(p, axis=-1, keepdims=True)
                oh = jnp.dot(p.astype(BF16), vh, preferred_element_type=F32) / den
                lh = m + jnp.log(den)
                acc = oh if acc is None else acc + oh
                lse = jnp.broadcast_to(lh, (blk, LANES)) if lse is None else jnp.where(low, lse, lh)
            o_ref[0, rows, sl] = acc.astype(o_ref.dtype)
            lse_ref[0, rows, sl] = lse


def _dilated_attention(aq, ak, av, dil, group):
    b, m, width = aq.shape
    width //= dil
    nb = m // ATTN_BLOCK
    blocks = group
    group = min(blocks, nb)
    classes = min(dil, blocks // group)
    tq = group * ATTN_BLOCK
    cur = pl.BlockSpec((1, tq, classes * width), lambda bi, r, n: (bi, n, r))
    prv = pl.BlockSpec((1, ATTN_BLOCK, classes * width), lambda bi, r, n: (bi, jnp.maximum(n * group - 1, 0), r))
    o, lse = pl.pallas_call(
        functools.partial(_attn_kernel, group=group, scale=ATTN_HEAD_DIM ** -0.5),
        grid=(b, dil // classes, nb // group),
        in_specs=[cur, cur, prv, cur, prv],
        out_specs=[cur, cur],
        out_shape=[jax.ShapeDtypeStruct((b, m, dil * width), BF16),
                   jax.ShapeDtypeStruct((b, m, dil * width), F32)],
        compiler_params=_cparams(("arbitrary", "arbitrary", "arbitrary")),
        name=f"dilated_attn_d{dil}",
    )(aq, ak, ak, av, av)
    return o, lse


def _split_bf16(v):
    hi = v.astype(BF16)
    return hi, (v - hi.astype(F32)).astype(BF16)


def _out_proj_kernel(oa_ref, *refs, dils, tile_t):
    n_pat = len(dils)
    o_refs, l_refs = refs[:n_pat], refs[n_pat:2 * n_pat]
    (x_ref, gb_ref, hm_ref, wa_ref, wb_ref, g2_ref, rwh_ref, rwl_ref, rb_ref,
     x1_ref, h2_ref) = refs[2 * n_pat:2 * n_pat + 11]
    route_refs = refs[2 * n_pat + 11:2 * n_pat + 19]
    carry_ref = refs[2 * n_pat + 19]
    slabs = refs[2 * n_pat + 20:]
    ratio = min(d for d in dils if d > 1)

    @pl.when(pl.program_id(0) == 0)
    def _():
        carry_ref[...] = jnp.zeros_like(carry_ref)

    outs = [_from_classes(o_refs[p], slabs[4 * p], slabs[4 * p + 1], dils[p], ratio) for p in range(n_pat)]
    lses = [_from_classes(l_refs[p], slabs[4 * p + 2], slabs[4 * p + 3], dils[p], ratio) for p in range(n_pat)]
    mx = functools.reduce(jnp.maximum, lses)
    es = [jnp.exp(l - mx) for l in lses]
    ob = sum(e * o for e, o in zip(es, outs)) / sum(es)
    sq_hi, sq_lo = _split_bf16(ob * ob)
    ms = (jnp.dot(sq_hi, hm_ref[...], preferred_element_type=F32)
          + jnp.dot(sq_lo, hm_ref[...], preferred_element_type=F32))
    obn = (ob * lax.rsqrt(ms + NORM_EPS) * gb_ref[...]).astype(BF16)
    y = jnp.dot(oa_ref[...], wa_ref[...], preferred_element_type=F32)
    y = y + jnp.dot(obn, wb_ref[...], preferred_element_type=F32)
    x1 = x_ref[...] + y
    x1_ref[...] = x1
    h2 = (x1 * lax.rsqrt(jnp.mean(x1 * x1, axis=-1, keepdims=True) + NORM_EPS)) * g2_ref[...]
    h_hi, h_lo = _split_bf16(h2)
    h2_ref[...] = h_hi
    lg = (jnp.dot(h_hi, rwh_ref[...], preferred_element_type=F32)
          + jnp.dot(h_hi, rwl_ref[...], preferred_element_type=F32)
          + jnp.dot(h_lo, rwh_ref[...], preferred_element_type=F32)) + rb_ref[...]
    for u in range(lg.shape[0] // tile_t):
        _route_tile(lg[u * tile_t:(u + 1) * tile_t].T, u, *route_refs, carry_ref)


def _out_proj(oa, obs, lses, dils, x2, gain_b, head_mean, w_a, w_b, gain2, rw_hi, rw_lo, rb, tile_m, tile_t,
              n_experts):
    n, d = x2.shape
    wm = oa.shape[1]
    nt, per_step = n // tile_t, tile_m // tile_t
    ne = pl.cdiv(n_experts, SUBLANES) * SUBLANES
    row = lambda w: pl.BlockSpec((tile_m, w), lambda i: (i, 0))
    cls = [pl.BlockSpec((tile_m // dil, dil * wm), lambda i: (i, 0)) for dil in dils]
    full = lambda a: pl.BlockSpec(a.shape, lambda i: (0, 0))
    tab = pl.BlockSpec((per_step, ne, LANES), lambda i: (i, 0, 0))
    tab_shape = jax.ShapeDtypeStruct((nt, ne, LANES), jnp.int32)
    consts = (gain_b, head_mean, w_a, w_b, gain2, rw_hi, rw_lo, rb)
    return pl.pallas_call(
        functools.partial(_out_proj_kernel, dils=dils, tile_t=tile_t),
        grid=(n // tile_m,),
        in_specs=[row(wm)] + cls + cls + [row(d)] + [full(a) for a in consts],
        out_specs=[row(d), row(d), row(LANES), row(LANES),
                   pl.BlockSpec((per_step * SUBLANES, tile_t), lambda i: (i, 0)), tab, tab, tab, tab,
                   pl.BlockSpec((ne, LANES), lambda i: (0, 0))],
        out_shape=[jax.ShapeDtypeStruct((n, d), F32), jax.ShapeDtypeStruct((n, d), BF16),
                   jax.ShapeDtypeStruct((n, LANES), F32), jax.ShapeDtypeStruct((n, LANES), jnp.int32),
                   jax.ShapeDtypeStruct((nt * SUBLANES, tile_t), jnp.int32), tab_shape, tab_shape, tab_shape,
                   tab_shape, jax.ShapeDtypeStruct((ne, LANES), jnp.int32)],
        scratch_shapes=[pltpu.VMEM((ne, LANES), F32)]
        + [pltpu.VMEM((wm // LANES, tile_m, LANES), F32) for _ in range(4 * len(dils))],
        compiler_params=_cparams(("arbitrary",)),
        name="out_proj",
    )(oa, *obs, *lses, x2, *consts)


def _route_tile(logits_t, u, gate_ref, pos_ref, post_ref, start_ref, nch_ref, nfull_ref, off_ref, cnt_ref,
                carry_ref):
    ne, tt = carry_ref.shape[0], logits_t.shape[1]

    logits = logits_t[0:ne, :]
    erow = lax.broadcasted_iota(jnp.int32, (ne, tt), 0).astype(F32)
    vals, idxs = [], []
    for _ in range(TOP_K):
        m = jnp.max(logits, axis=0, keepdims=True)
        ix = jnp.min(jnp.where(logits == m, erow, float(ne)), axis=0, keepdims=True)
        vals.append(m)
        idxs.append(ix)
        logits = jnp.where(erow == ix, -jnp.inf, logits)
    exps = [jnp.exp(v - vals[0]) for v in vals]
    den = exps[0] + exps[1] + exps[2] + exps[3]
    chosen = jnp.zeros((ne, tt), F32)
    for ix in idxs:
        chosen = chosen + jnp.where(erow == ix, 1.0, 0.0)
    s = lax.broadcasted_iota(jnp.int32, (tt, tt), 0)
    t = lax.broadcasted_iota(jnp.int32, (tt, tt), 1)
    before = jnp.dot(chosen.astype(BF16), jnp.where(s < t, 1.0, 0.0).astype(BF16), preferred_element_type=F32)
    count = jnp.broadcast_to(jnp.sum(chosen, axis=1, keepdims=True), (ne, LANES))
    carry = carry_ref[...]
    head = carry - MOE_CHUNK * jnp.floor(carry * (1.0 / MOE_CHUNK))
    nfull = jnp.floor((head + count) * (1.0 / MOE_CHUNK))
    nch = jnp.floor((head + count + (MOE_CHUNK - 1)) * (1.0 / MOE_CHUNK))
    er = lax.broadcasted_iota(jnp.int32, (ne, ne), 0)
    ec = lax.broadcasted_iota(jnp.int32, (ne, ne), 1)
    off = MOE_CHUNK * jnp.dot(jnp.where(ec < er, 1.0, 0.0).astype(BF16), nch.astype(BF16),
                              preferred_element_type=F32)
    slot = before + jnp.concatenate([off + head] * (tt // LANES), axis=1)
    krow = lax.broadcasted_iota(jnp.int32, (LANES, tt), 0)
    gate_t = jnp.zeros((LANES, tt), F32)
    pos_t = jnp.zeros((LANES, tt), F32)
    for k in range(TOP_K):
        pk = jnp.sum(jnp.where(erow == idxs[k], slot, 0.0), axis=0, keepdims=True)
        gate_t = jnp.where(krow == k, exps[k] / den, gate_t)
        pos_t = jnp.where(krow == k, pk, pos_t)
    gate_ref[u * tt:(u + 1) * tt, :] = gate_t.T
    pos_ref[u * tt:(u + 1) * tt, :] = pos_t.T.astype(jnp.int32)
    post_ref[u * SUBLANES:(u + 1) * SUBLANES, :] = pos_t[0:SUBLANES].astype(jnp.int32)
    start_ref[u] = (carry - head).astype(jnp.int32)
    nch_ref[u] = nch.astype(jnp.int32)
    nfull_ref[u] = nfull.astype(jnp.int32)
    off_ref[u] = off.astype(jnp.int32)
    carry_ref[...] = carry + count
    cnt_ref[...] = (carry + count).astype(jnp.int32)


def _for_each_chunk(tab_ref, n, fn):
    def one(c):
        fn(pl.multiple_of(c * MOE_CHUNK, MOE_CHUNK), pl.multiple_of(tab_ref[0, 0, c], MOE_CHUNK))

    def four(q, carry):
        for u in range(4):
            one(q * 4 + u)
        return carry

    def single(c, carry):
        one(c)
        return carry

    lax.fori_loop(0, lax.shift_right_logical(n, 2), four, 0)
    lax.fori_loop(n & ~3, n, single, 0)


def _pad_sizes():
    sizes, s = [], MOE_ROWS // 2
    while s >= MOE_CHUNK:
        sizes.append(s)
        s //= 2
    return sizes


def _scatter_kernel(nch_ref, nfull_ref, off_ref, total_ref, padrow_ref, padlen_ref, nused_ref, post_ref, tab_ref,
                    tabp_ref, h_ref, xout_ref, stage_ref, open_ref, zero_ref, sems, zsem, *, n_experts):
    rs, tt = stage_ref.shape[1], h_ref.shape[0]
    i, nt = pl.program_id(0), pl.num_programs(0)
    slot = i % 2
    base = i * n_experts
    stage = stage_ref.at[slot]

    @pl.when(i == 0)
    def _():
        open_ref[...] = jnp.zeros_like(open_ref)
        zero_ref[...] = jnp.zeros_like(zero_ref)
        stage_ref[...] = jnp.zeros_like(stage_ref)

    post = post_ref[...]
    main = _stage_main_rows(tt, n_experts)

    def select_rows(r0, r1):
        srow = lax.broadcasted_iota(jnp.int32, (r1 - r0, tt), 0) + r0
        hit = srow == post[0:1]
        for k in range(1, TOP_K):
            hit = hit | (srow == post[k:k + 1])
        sel = jnp.where(hit, 1.0, 0.0).astype(BF16)
        stage[r0:r1, :] = jnp.dot(sel, h_ref[...], preferred_element_type=F32)

    select_rows(0, main)
    if main < rs:
        @pl.when(total_ref[i] * MOE_CHUNK > main)
        def _():
            select_rows(main, rs)

    def add_open(e, carry):
        rows = pl.ds(pl.multiple_of(off_ref[base + e], MOE_CHUNK), MOE_CHUNK)
        stage[rows, :] = stage[rows, :] + open_ref[e]
        return carry

    def save_open(e, carry):
        nfull = nfull_ref[base + e]
        rows = pl.ds(pl.multiple_of(off_ref[base + e] + nfull * MOE_CHUNK, MOE_CHUNK), MOE_CHUNK)
        still_open = nch_ref[base + e] > nfull
        open_ref[e] = jnp.where(still_open, stage[rows, :], 0.0)
        return carry

    def chunk_copy(sl, srow0, xrow0):
        return pltpu.make_async_copy(stage_ref.at[sl, pl.ds(srow0, MOE_CHUNK)],
                                     xout_ref.at[pl.ds(xrow0, MOE_CHUNK)], sems.at[sl])

    def pad_copies(fn):
        def zeros_to(start, size):
            fn(pltpu.make_async_copy(zero_ref.at[pl.ds(0, size)], xout_ref.at[pl.ds(start, size)], zsem))

        def expert(e, carry):
            row, length = pl.multiple_of(padrow_ref[e], MOE_CHUNK), padlen_ref[e]
            for size in _pad_sizes():
                @pl.when((length & size) != 0)
                def _():
                    zeros_to(pl.multiple_of(row + (length & ~(2 * size - 1)), MOE_CHUNK), size)
            return carry

        def unused_half_block(c, carry):
            zeros_to(pl.multiple_of(c * (MOE_ROWS // 2), MOE_ROWS // 2), MOE_ROWS // 2)
            return carry

        lax.fori_loop(0, n_experts, expert, 0)
        lax.fori_loop(2 * nused_ref[0], xout_ref.shape[0] // (MOE_ROWS // 2), unused_half_block, 0)

    lax.fori_loop(0, n_experts, add_open, 0, unroll=8)

    @pl.when(i > 0)
    def _():
        _for_each_chunk(tabp_ref, total_ref[i - 1], lambda s0, x0: chunk_copy(1 - slot, s0, x0).wait())

    _for_each_chunk(tab_ref, total_ref[i], lambda s0, x0: chunk_copy(slot, s0, x0).start())
    lax.fori_loop(0, n_experts, save_open, 0, unroll=8)

    @pl.when(i == nt - 1)
    def _():
        pad_copies(lambda cp: cp.start())
        _for_each_chunk(tab_ref, total_ref[i], lambda s0, x0: chunk_copy(slot, s0, x0).wait())
        pad_copies(lambda cp: cp.wait())


def _stage_main_rows(tile_t, n_experts):
    return min(pl.cdiv(tile_t * TOP_K + n_experts * (MOE_CHUNK - 1), LANES) * LANES, _stage_rows(tile_t, n_experts))


def _stage_rows(tile_t, n_experts):
    rows = tile_t * TOP_K + 2 * n_experts * (MOE_CHUNK - 1) + MOE_CHUNK
    return pl.cdiv(rows, LANES) * LANES


def _chunk_table_spec(table, shift):
    nt, _, width = table.shape
    return pl.BlockSpec((1, 1, width), lambda i, *_: (jnp.clip(i + shift, 0, nt - 1), 0, 0), memory_space=pltpu.SMEM)


def _moe_scatter(nch, nfull, off, total, padrow, padlen, n_used, post, table, h2, n_rows, tile_t, n_experts):
    n, d = h2.shape
    tab = lambda shift: _chunk_table_spec(table, shift)
    return pl.pallas_call(
        functools.partial(_scatter_kernel, n_experts=n_experts),
        grid_spec=pltpu.PrefetchScalarGridSpec(
            num_scalar_prefetch=7,
            grid=(n // tile_t,),
            in_specs=[pl.BlockSpec((SUBLANES, tile_t), lambda i, *_: (i, 0)), tab(0), tab(-1),
                      pl.BlockSpec((tile_t, d), lambda i, *_: (i, 0))],
            out_specs=pl.BlockSpec(memory_space=pl.ANY),
            scratch_shapes=[pltpu.VMEM((2, _stage_rows(tile_t, n_experts), d), F32),
                            pltpu.VMEM((n_experts, MOE_CHUNK, d), F32),
                            pltpu.VMEM((MOE_ROWS // 2, d), F32),
                            pltpu.SemaphoreType.DMA((2,)), pltpu.SemaphoreType.DMA]),
        out_shape=jax.ShapeDtypeStruct((n_rows, d), F32),
        compiler_params=_cparams(("arbitrary",)),
        name="moe_scatter",
    )(nch, nfull, off, total, padrow, padlen, n_used, post, table, table, h2)


def _expert_kernel(be_ref, nu_ref, valid_ref, slot_ref, next_ref, x_ref, bg_ref, bl_ref, bd_ref, wu_hbm, wd_hbm,
                   y_ref, wu_buf, wd_buf, wg_s, wl_s, wd_s, sems):
    i = pl.program_id(0)
    used = i < nu_ref[0]
    e = be_ref[i]
    new_expert = (i == 0) | (e != be_ref[jnp.maximum(i - 1, 0)])
    slot = slot_ref[e]

    def weight_copies(expert, sl):
        return (pltpu.make_async_copy(wu_hbm.at[expert], wu_buf.at[sl], sems.at[0, sl]),
                pltpu.make_async_copy(wd_hbm.at[expert], wd_buf.at[sl], sems.at[1, sl]))

    @pl.when(used & (i == 0))
    def _():
        for cp in weight_copies(e, slot):
            cp.start()

    @pl.when(used & new_expert)
    def _():
        for cp in weight_copies(e, slot):
            cp.wait()
        nxt = next_ref[e]

        @pl.when(nxt >= 0)
        def _():
            for cp in weight_copies(nxt, 1 - slot):
                cp.start()

        r = lax.broadcasted_iota(jnp.int32, (2 * LANES, 2 * LANES), 0)
        c = lax.broadcasted_iota(jnp.int32, (2 * LANES, 2 * LANES), 1)
        pick = jnp.where(r == jnp.where(c < LANES, 2 * c, 2 * (c - LANES) + 1), 1.0, 0.0).astype(BF16)
        for cb in range(wg_s.shape[1] // LANES):
            grp = wu_buf[slot, :, cb * 2 * LANES:(cb + 1) * 2 * LANES].astype(BF16)
            out = slice(cb * LANES, (cb + 1) * LANES)
            both = jnp.dot(grp, pick, preferred_element_type=F32).astype(BF16)
            wg_s[:, out] = both[:, :LANES]
            wl_s[:, out] = both[:, LANES:]
        wd_s[...] = wd_buf[slot].astype(BF16)

    def ffn(rows):
        x = x_ref[rows, :].astype(BF16)
        glu = jnp.dot(x, wg_s[...], preferred_element_type=F32) + bg_ref[0]
        lin = jnp.dot(x, wl_s[...], preferred_element_type=F32) + bl_ref[0]
        glu = jnp.minimum(glu, SWIGLU_LIMIT)
        lin = jnp.clip(lin, -SWIGLU_LIMIT, SWIGLU_LIMIT)
        h = glu * jax.nn.sigmoid(SWIGLU_ALPHA * glu) * (lin + 1.0)
        y_ref[rows, :] = jnp.dot(h.astype(BF16), wd_s[...], preferred_element_type=F32) + bd_ref[0]

    half = MOE_ROWS // 2
    full = used & (valid_ref[i] > half)

    @pl.when(full)
    def _():
        ffn(slice(0, MOE_ROWS))

    @pl.when(used & jnp.logical_not(full))
    def _():
        ffn(slice(0, half))
        y_ref[half:, :] = jnp.zeros((MOE_ROWS - half, y_ref.shape[1]), y_ref.dtype)

    @pl.when(jnp.logical_not(used))
    def _():
        y_ref[...] = jnp.zeros_like(y_ref)


def _moe_experts(block_e, n_used, valid, buf_slot, next_expert, xbuf, w_up, bg, bl, w_down, bd):
    n_rows, d = xbuf.shape
    de = w_down.shape[1]
    nblk = n_rows // MOE_ROWS
    rows = pl.BlockSpec((MOE_ROWS, d), lambda i, be, nu, *_: (i, 0))
    rows_in = pl.BlockSpec((MOE_ROWS, d), lambda i, be, nu, *_: (jnp.minimum(i, jnp.maximum(nu[0] - 1, 0)), 0))
    bias = lambda a: pl.BlockSpec((1,) + a.shape[1:], lambda i, be, *_: (be[i], 0, 0))
    hbm = pl.BlockSpec(memory_space=pl.ANY)
    return pl.pallas_call(
        _expert_kernel,
        grid_spec=pltpu.PrefetchScalarGridSpec(
            num_scalar_prefetch=5,
            grid=(nblk,),
            in_specs=[rows_in, bias(bg), bias(bl), bias(bd), hbm, hbm],
            out_specs=rows,
            scratch_shapes=[pltpu.VMEM((2,) + w_up.shape[1:], F32), pltpu.VMEM((2,) + w_down.shape[1:], F32),
                            pltpu.VMEM((d, de), BF16), pltpu.VMEM((d, de), BF16), pltpu.VMEM((de, d), BF16),
                            pltpu.SemaphoreType.DMA((2, 2))]),
        out_shape=jax.ShapeDtypeStruct((n_rows, d), F32),
        compiler_params=_cparams(("arbitrary",)),
        name="moe_experts",
    )(block_e, n_used, valid, buf_slot, next_expert, xbuf, bg, bl, bd, w_up, w_down)


def _combine_kernel(total_ref, pos_ref, gate_ref, x1_ref, g_ref, tab_ref, tabn_ref, y_ref, o_ref, stage_ref, acc_ref,
                    sems, *, n_experts):
    rs, tt = stage_ref.shape[1], x1_ref.shape[0]
    i, nt = pl.program_id(0), pl.num_programs(0)
    slot = i % 2

    def chunk_copy(sl, srow0, yrow0):
        return pltpu.make_async_copy(y_ref.at[pl.ds(yrow0, MOE_CHUNK)],
                                     stage_ref.at[sl, pl.ds(srow0, MOE_CHUNK)], sems.at[sl])

    @pl.when(i == 0)
    def _():
        stage_ref[...] = jnp.zeros_like(stage_ref)
        _for_each_chunk(tab_ref, total_ref[i], lambda s0, y0: chunk_copy(slot, s0, y0).start())

    @pl.when(i + 1 < nt)
    def _():
        _for_each_chunk(tabn_ref, total_ref[i + 1], lambda s0, y0: chunk_copy(1 - slot, s0, y0).start())

    pos, gates = pos_ref[...], gate_ref[...]
    main = _stage_main_rows(tt, n_experts)

    def weighted_rows(r0, r1):
        scol = lax.broadcasted_iota(jnp.int32, (tt, r1 - r0), 1) + r0
        weights = jnp.zeros((tt, r1 - r0), F32)
        for k in range(TOP_K):
            weights = jnp.where(scol == pos[:, k:k + 1], gates[:, k:k + 1], weights)
        return jnp.dot(weights.astype(BF16), stage_ref[slot, r0:r1, :].astype(BF16), preferred_element_type=F32)

    _for_each_chunk(tab_ref, total_ref[i], lambda s0, y0: chunk_copy(slot, s0, y0).wait())
    acc_ref[...] = x1_ref[...] + weighted_rows(0, main)
    if main < rs:
        @pl.when(total_ref[i] * MOE_CHUNK > main)
        def _():
            acc_ref[...] += weighted_rows(main, rs)
    acc = acc_ref[...]
    o_ref[...] = (acc * lax.rsqrt(jnp.mean(acc * acc, axis=-1, keepdims=True) + NORM_EPS)) * g_ref[...]


def _moe_combine(total, pos, gates, x1, gain, table, ybuf, tile_t, n_experts):
    n, d = x1.shape
    return pl.pallas_call(
        functools.partial(_combine_kernel, n_experts=n_experts),
        grid_spec=pltpu.PrefetchScalarGridSpec(
            num_scalar_prefetch=1,
            grid=(n // tile_t,),
            in_specs=[pl.BlockSpec((tile_t, LANES), lambda i, *_: (i, 0)),
                      pl.BlockSpec((tile_t, LANES), lambda i, *_: (i, 0)),
                      pl.BlockSpec((tile_t, d), lambda i, *_: (i, 0)),
                      pl.BlockSpec((1, d), lambda i, *_: (0, 0)),
                      _chunk_table_spec(table, 0), _chunk_table_spec(table, 1),
                      pl.BlockSpec(memory_space=pl.ANY)],
            out_specs=pl.BlockSpec((tile_t, d), lambda i, *_: (i, 0)),
            scratch_shapes=[pltpu.VMEM((2, _stage_rows(tile_t, n_experts), d), F32),
                            pltpu.VMEM((tile_t, d), F32), pltpu.SemaphoreType.DMA((2,))]),
        out_shape=jax.ShapeDtypeStruct((n, d), F32),
        compiler_params=_cparams(("arbitrary",)),
        name="moe_combine",
    )(total, pos, gates, x1, gain, table, table, ybuf)


def kernel(x, norm1_g, w_in, hgrn_lb_logits, hgrn_norm_g, attn_norm_g, w_out, norm2_g, router_w, router_b,
           w_up, b_up, w_down, b_down, final_norm_g):
    b, s, d = x.shape
    n = b * s
    depth = w_in.shape[0]
    n_experts = router_w.shape[-1]
    d_mix = w_out.shape[1]
    d_hgrn = d_mix // 2
    d_attn = d_mix - d_hgrn
    assert w_in.shape[-1] == 4 * d_hgrn + 3 * d_attn and d_hgrn == d_attn
    assert all(s % win == 0 and win // dil == ATTN_BLOCK for win, dil in DILATED_PATTERNS)
    assert s % HGRN_ROWS == 0 and n % IN_PROJ_ROWS == 0 and n % PROJ_ROWS == 0 and PROJ_ROWS % ROUTE_TOKENS == 0
    assert n_experts <= LANES
    assert depth == 1, "the final rmsnorm is fused into the single layer's MoE combine"

    lb_all = jnp.cumsum(jax.nn.softmax(hgrn_lb_logits.astype(F32), axis=0), axis=0)
    lane = jnp.arange(d_attn)
    head_mean = jnp.where((lane[:, None] // ATTN_HEAD_DIM) == (lane[None, :] // ATTN_HEAD_DIM),
                          1.0 / ATTN_HEAD_DIM, 0.0).astype(BF16)
    x2 = x.reshape(n, d)
    l = 0
    dils = tuple(dil for _, dil in DILATED_PATTERNS)
    n_pat = len(dils)
    hq, hf, hi, hg, *attn = _in_proj(
        x2, norm1_g[l].reshape(1, d), w_in[l].astype(BF16),
        (BF16, F32, BF16, BF16), 3, dils, tile_m=IN_PROJ_ROWS)
    to3 = lambda t: t.reshape(b, s, -1)
    o_a = _hgrn2(to3(hq), to3(hf), to3(hi), to3(hg), lb_all[l].reshape(1, d_hgrn),
                 hgrn_norm_g[l].reshape(1, d_hgrn), tile_s=HGRN_ROWS)
    obs, lses = [], []
    for p, dil in enumerate(dils):
        aq, ak, av = (attn[a * n_pat + p].reshape(b, s // dil, dil * d_attn) for a in range(3))
        o_p, lse_p = _dilated_attention(aq, ak, av, dil, group=ATTN_BLOCKS)
        obs.append(o_p.reshape(n // dil, dil * d_attn))
        lses.append(lse_p.reshape(n // dil, dil * d_attn))
    rw = jnp.zeros((d, LANES), F32).at[:, :n_experts].set(router_w[l])
    rw_hi = rw.astype(BF16)
    rw_lo = (rw - rw_hi.astype(F32)).astype(BF16)
    rb = jnp.full((1, LANES), -jnp.inf, F32).at[0, :n_experts].set(router_b[l])
    w_o = w_out[l].astype(BF16)
    tile_t = ROUTE_TOKENS
    x1, h2, gates, pos, post, start, nch, nfull, off, counts = _out_proj(
        o_a.reshape(n, d_hgrn), obs, lses, dils, x2, attn_norm_g[l].reshape(1, d_attn), head_mean,
        w_o[:d_hgrn], w_o[d_hgrn:], norm2_g[l].reshape(1, d), rw_hi, rw_lo, rb, PROJ_ROWS, tile_t, n_experts)

    counts = counts[:n_experts, 0]
    padded = (counts + MOE_ROWS - 1) // MOE_ROWS * MOE_ROWS
    pad_end = jnp.cumsum(padded)
    pad_start = pad_end - padded
    n_rows = n * TOP_K + n_experts * MOE_ROWS
    nblk = n_rows // MOE_ROWS
    block_row = jnp.arange(nblk, dtype=jnp.int32) * MOE_ROWS
    block_e = jnp.minimum(jnp.sum(pad_end[None, :] <= block_row[:, None], axis=1), n_experts - 1).astype(jnp.int32)
    n_used = (pad_end[-1:] // MOE_ROWS).astype(jnp.int32)
    eid = jnp.arange(n_experts, dtype=jnp.int32)
    group_end = jnp.sum(jnp.where(eid[None, :] == block_e[:, None], (pad_start + counts)[None, :], 0), axis=1)
    valid = jnp.clip(group_end - block_row, 0, MOE_ROWS).astype(jnp.int32)
    has_rows = padded > 0
    buf_slot = ((jnp.cumsum(has_rows) - 1) % 2).astype(jnp.int32)
    following = jnp.min(jnp.where(has_rows[None, :] & (eid[None, :] > eid[:, None]), eid[None, :], n_experts), axis=1)
    next_expert = jnp.where(following < n_experts, following, -1).astype(jnp.int32)
    per_expert = lambda t: t[:, :n_experts, 0]
    seg = per_expert(start) + pad_start[None, :].astype(jnp.int32)
    nch, nfull, off = per_expert(nch), per_expert(nfull), per_expert(off)
    total = jnp.sum(nch, axis=1).astype(jnp.int32)
    first = off // MOE_CHUNK
    max_chunks = pl.cdiv(_stage_rows(tile_t, n_experts) // MOE_CHUNK, LANES) * LANES
    chunk = jnp.arange(max_chunks, dtype=jnp.int32)[None, :, None]
    owned = (first[:, None, :] <= chunk) & (chunk < (first + nch)[:, None, :])
    table = jnp.sum(jnp.where(owned, (seg - first * MOE_CHUNK)[:, None, :], 0), axis=-1) + chunk[:, :, 0] * MOE_CHUNK
    table = table.astype(jnp.int32)[:, None, :]
    filled = (counts + MOE_CHUNK - 1) // MOE_CHUNK * MOE_CHUNK
    padrow = (pad_start + filled).astype(jnp.int32)
    padlen = (padded - filled).astype(jnp.int32)

    xbuf = _moe_scatter(nch.reshape(-1), nfull.reshape(-1), off.reshape(-1), total, padrow, padlen, n_used, post,
                        table, h2, n_rows, tile_t, n_experts)
    ybuf = _moe_experts(block_e, n_used, valid, buf_slot, next_expert, xbuf, w_up[l], b_up[l][:, None, 0::2],
                        b_up[l][:, None, 1::2], w_down[l], b_down[l][:, None, :])
    out = _moe_combine(total, pos, gates, x1, final_norm_g.reshape(1, d), table, ybuf, tile_t, n_experts)
    return out.reshape(b, s, d)
```

```python
import functools

import jax
import jax.numpy as jnp
from jax import lax
from jax.experimental import pallas as pl
from jax.experimental.pallas import tpu as pltpu

F32 = jnp.float32
BF16 = jnp.bfloat16

NORM_EPS = 1e-6
HGRN_HEAD_DIM = 128
ATTN_HEAD_DIM = 64
DILATED_PATTERNS = ((128, 1), (512, 4), (2048, 16))
TOP_K = 4
SWIGLU_ALPHA = 1.702
SWIGLU_LIMIT = 7.0

LANES = 128
SUBLANES = 8
HGRN_CHUNK = 64
ATTN_BLOCK = 128
MOE_ROWS = 512
MOE_CHUNK = SUBLANES
V7X_VMEM_BYTES = 64 * 1024 * 1024
VMEM_LIMIT = V7X_VMEM_BYTES - 8 * 1024 * 1024

IN_PROJ_ROWS = 1024
PROJ_ROWS = 512
HGRN_ROWS = 1024
ATTN_BLOCKS = 8
ROUTE_TOKENS = 256

_NT = (((1,), (1,)), ((), ()))
_TN = (((0,), (0,)), ((), ()))


def _cparams(sem):
    return pltpu.CompilerParams(dimension_semantics=sem, vmem_limit_bytes=VMEM_LIMIT)


def _to_classes(res, o_refs, dils, slab_refs):
    tm, width = res.shape
    n_slab = width // LANES
    for sb in range(n_slab):
        slab_refs[0][sb] = res[:, sb * LANES:(sb + 1) * LANES]
    prev = 1
    for p, (o_ref, dil) in enumerate(zip(o_refs, dils)):
        if dil == 1:
            o_ref[...] = res.astype(o_ref.dtype)
            continue
        ratio, src, last = dil // prev, slab_refs[0] if prev == 1 else slab_refs[1], p == len(dils) - 1
        assert dil % prev == 0 and (prev == 1 or last), "one intermediate slab: at most two strided levels"
        rows = tm // dil
        for rp in range(prev):
            for r2 in range(ratio):
                r = r2 * prev + rp
                for sb in range(n_slab):
                    part = src[sb, pl.ds(rp * (tm // prev) + r2, rows, stride=ratio), :]
                    col = r * width + sb * LANES
                    o_ref[:, col:col + LANES] = part.astype(o_ref.dtype)
                    if not last:
                        slab_refs[1][sb, r * rows:(r + 1) * rows, :] = part
        prev = dil


def _from_classes(ref, slab_ref, mid_ref, dil, ratio):
    if dil == 1:
        return ref[...].astype(F32)
    rows = ref.shape[0]
    tm, width = rows * dil, ref.shape[1] // dil
    n_slab = width // LANES
    assert dil in (ratio, ratio * ratio)
    prev = dil // ratio
    for rp in range(prev):
        for r2 in range(ratio):
            r = r2 * prev + rp
            for sb in range(n_slab):
                col = r * width + sb * LANES
                part = ref[:, col:col + LANES].astype(F32)
                if prev == 1:
                    slab_ref[sb, pl.ds(r2, rows, stride=ratio), :] = part
                else:
                    mid_ref[sb, pl.ds(rp * (tm // prev) + r2, rows, stride=ratio), :] = part
    if prev > 1:
        seg = tm // prev
        for rp in range(prev):
            for sb in range(n_slab):
                slab_ref[sb, pl.ds(rp, seg, stride=prev), :] = mid_ref[sb, rp * seg:(rp + 1) * seg, :]
    return jnp.concatenate([slab_ref[sb] for sb in range(n_slab)], axis=-1)


def _in_proj_kernel(x_ref, g_ref, w_ref, *refs, plain_dtypes, n_attn, dils):
    out_refs, slab_refs = refs[:-2], refs[-2:]
    x = x_ref[...]
    r = lax.rsqrt(jnp.mean(x * x, axis=-1, keepdims=True) + NORM_EPS)
    h = ((x * r) * g_ref[...]).astype(BF16)
    width = out_refs[0].shape[-1]
    n_plain = len(plain_dtypes)
    for j in list(range(n_plain, n_plain + n_attn)) + list(range(n_plain)):
        res = jnp.dot(h, w_ref[:, j * width:(j + 1) * width], preferred_element_type=F32)
        if j < n_plain:
            out_refs[j][...] = res.astype(out_refs[j].dtype)
        else:
            first = n_plain + (j - n_plain) * len(dils)
            _to_classes(res, out_refs[first:first + len(dils)], dils, slab_refs)


def _in_proj(x2, gain, w_bf16, plain_dtypes, n_attn, dils, tile_m):
    n, d = x2.shape
    width = w_bf16.shape[1] // (len(plain_dtypes) + n_attn)
    out_specs = [pl.BlockSpec((tile_m, width), lambda i: (i, 0)) for _ in plain_dtypes]
    out_shape = [jax.ShapeDtypeStruct((n, width), dt) for dt in plain_dtypes]
    for _ in range(n_attn):
        for dil in dils:
            out_specs.append(pl.BlockSpec((tile_m // dil, dil * width), lambda i: (i, 0)))
            out_shape.append(jax.ShapeDtypeStruct((n // dil, dil * width), BF16))
    return pl.pallas_call(
        functools.partial(_in_proj_kernel, plain_dtypes=plain_dtypes, n_attn=n_attn, dils=dils),
        grid=(n // tile_m,),
        in_specs=[pl.BlockSpec((tile_m, d), lambda i: (i, 0)),
                  pl.BlockSpec((1, d), lambda i: (0, 0)),
                  pl.BlockSpec(w_bf16.shape, lambda i: (0, 0), pipeline_mode=pl.Buffered(1))],
        out_specs=out_specs,
        out_shape=out_shape,
        scratch_shapes=[pltpu.VMEM((width // LANES, tile_m, LANES), F32) for _ in range(2)],
        compiler_params=_cparams(("arbitrary",)),
        name="in_proj",
    )(x2, gain, w_bf16)


def _hgrn_kernel(q_ref, f_ref, i_ref, g_ref, lb_ref, gain_ref, o_ref, state_ref, *, n_heads):
    hd = HGRN_HEAD_DIM
    width = n_heads * hd
    nblk = HGRN_CHUNK // SUBLANES

    @pl.when(pl.program_id(1) == 0)
    def _():
        state_ref[...] = jnp.zeros_like(state_ref)

    lb = lb_ref[...]
    gain = gain_ref[...]
    row8 = lax.broadcasted_iota(jnp.int32, (SUBLANES, width), 0)
    rowc = lax.broadcasted_iota(jnp.int32, (SUBLANES, HGRN_CHUNK), 0)
    colc = lax.broadcasted_iota(jnp.int32, (SUBLANES, HGRN_CHUNK), 1)
    r64 = lax.broadcasted_iota(jnp.int32, (HGRN_CHUNK, HGRN_CHUNK), 0)
    c64 = lax.broadcasted_iota(jnp.int32, (HGRN_CHUNK, HGRN_CHUNK), 1)
    same32 = (r64 // 32) == (c64 // 32)
    same16 = (r64 // 16) == (c64 // 16)

    def scan8(x):
        for s in (1, 2, 4):
            x = x + jnp.where(row8 >= s, pltpu.roll(x, s, axis=0), 0.0)
        return x

    def cat(blocks):
        return jnp.concatenate(blocks, axis=0)

    def chunk(ci, carry):
        r0 = pl.multiple_of(ci * HGRN_CHUNK, HGRN_CHUNK)
        rows = pl.ds(r0, HGRN_CHUNK)
        q = q_ref[0, rows, :].astype(F32)
        v = i_ref[0, rows, :]
        f = lb + (1.0 - lb) * jax.nn.sigmoid(f_ref[0, rows, :])
        logf = jnp.log2(f)
        kk = 1.0 - f

        qb = [q[SUBLANES * i:SUBLANES * (i + 1)] for i in range(nblk)]
        kb = [kk[SUBLANES * i:SUBLANES * (i + 1)] for i in range(nblk)]
        b8 = [scan8(logf[SUBLANES * i:SUBLANES * (i + 1)]) for i in range(nblk)]
        t8 = [jnp.broadcast_to(x[SUBLANES - 1:SUBLANES], x.shape) for x in b8]
        b16 = [b8[i] + t8[i - 1] if i % 2 else b8[i] for i in range(nblk)]
        t16 = [t8[i - i % 2] + t8[i - i % 2 + 1] for i in range(nblk)]
        b32 = [b16[i] + t16[i - 2] if (i // 2) % 2 else b16[i] for i in range(nblk)]
        t32 = [t16[i - i % 4] + t16[i - i % 4 + 2] for i in range(nblk)]
        b64 = [b32[i] + t32[0] if i >= 4 else b32[i] for i in range(nblk)]
        t64 = t32[0] + t32[4]

        zero = jnp.zeros_like(qb[0])

        def q_side(bl, span):
            return cat([qb[i] * jnp.exp2(bl[i]) if (i * SUBLANES // span) % 2 else zero for i in range(nblk)])

        def k_side(bl, tl, span):
            return cat([zero if (i * SUBLANES // span) % 2 else kb[i] * jnp.exp2(tl[i] - bl[i]) for i in range(nblk)])

        q64 = cat([qb[i] * jnp.exp2(b64[i]) for i in range(nblk)]).astype(BF16)
        k64 = cat([kb[i] * jnp.exp2(t64 - b64[i]) for i in range(nblk)]).astype(BF16)
        q32, k32 = q_side(b32, 32).astype(BF16), k_side(b32, t32, 32).astype(BF16)
        q16, k16 = q_side(b16, 16).astype(BF16), k_side(b16, t16, 16).astype(BF16)
        q8, k8 = q_side(b8, 8).astype(BF16), k_side(b8, t8, 8).astype(BF16)
        decay = jnp.exp2(t64[0:1])

        diag = [[jnp.zeros((SUBLANES, HGRN_CHUNK), F32) for _ in range(nblk)] for _ in range(n_heads)]
        for j in range(SUBLANES):
            for i in range(nblk):
                if j == 0:
                    p = qb[i] * kb[i]
                else:
                    p = qb[i] * pltpu.roll(kb[i], j, axis=0) * jnp.exp2(b8[i] - pltpu.roll(b8[i], j, axis=0))
                hit = (colc == rowc + (SUBLANES * i - j)) & (rowc >= j)
                for h in range(n_heads):
                    w = jnp.sum(p[:, h * hd:(h + 1) * hd], axis=-1, keepdims=True)
                    diag[h][i] = jnp.where(hit, w, diag[h][i])

        outs = []
        for h in range(n_heads):
            sl = slice(h * hd, (h + 1) * hd)
            a = cat(diag[h])
            a = a + lax.dot_general(q32[:, sl], k32[:, sl], _NT, preferred_element_type=F32)
            a = a + jnp.where(same32, lax.dot_general(q16[:, sl], k16[:, sl], _NT, preferred_element_type=F32), 0.0)
            a = a + jnp.where(same16, lax.dot_general(q8[:, sl], k8[:, sl], _NT, preferred_element_type=F32), 0.0)
            st = state_ref[h]
            o = lax.dot_general(q64[:, sl], st.astype(BF16), _NT, preferred_element_type=F32)
            o = o + jnp.dot(a.astype(BF16), v[:, sl], preferred_element_type=F32)
            state_ref[h] = st * decay[:, sl] + lax.dot_general(v[:, sl], k64[:, sl], _TN,
                                                              preferred_element_type=F32)
            o = o * lax.rsqrt(jnp.mean(o * o, axis=-1, keepdims=True) + NORM_EPS)
            outs.append(o)
        o = jnp.concatenate(outs, axis=-1) * gain
        o_ref[0, rows, :] = (o * jax.nn.silu(g_ref[0, rows, :].astype(F32))).astype(o_ref.dtype)
        return carry

    lax.fori_loop(0, q_ref.shape[1] // HGRN_CHUNK, chunk, 0, unroll=8)


def _hgrn2(hq, hf, hi, hg, lb, gain, tile_s):
    b, s, width = hq.shape
    n_heads = width // HGRN_HEAD_DIM
    blk = pl.BlockSpec((1, tile_s, width), lambda bi, si: (bi, si, 0))
    vec = pl.BlockSpec((1, width), lambda bi, si: (0, 0))
    return pl.pallas_call(
        functools.partial(_hgrn_kernel, n_heads=n_heads),
        grid=(b, s // tile_s),
        in_specs=[blk, blk, blk, blk, vec, vec],
        out_specs=blk,
        out_shape=jax.ShapeDtypeStruct((b, s, width), BF16),
        scratch_shapes=[pltpu.VMEM((n_heads, HGRN_HEAD_DIM, HGRN_HEAD_DIM), F32)],
        compiler_params=_cparams(("arbitrary", "arbitrary")),
        name="hgrn2",
    )(hq, hf, hi, hg, lb, gain)


def _attn_kernel(qc_ref, kc_ref, kp_ref, vc_ref, vp_ref, o_ref, lse_ref, *, group, scale):
    blk = ATTN_BLOCK
    n_pairs = qc_ref.shape[-1] // LANES
    lane = lax.broadcasted_iota(jnp.int32, (1, LANES), 1)
    low = lane < ATTN_HEAD_DIM
    qi = lax.broadcasted_iota(jnp.int32, (blk, 2 * blk), 0)
    ki = lax.broadcasted_iota(jnp.int32, (blk, 2 * blk), 1)
    band = ((ki >= blk) & (ki - blk <= qi)) | ((ki < blk) & (ki >= qi))
    first_key = jnp.where(pl.program_id(2) == 0, blk, 0)
    band_first = band & (ki >= first_key)

    for g in range(group):
        rows = slice(g * blk, (g + 1) * blk)
        prev = slice((g - 1) * blk, g * blk)
        mask = band_first if g == 0 else band
        for pr in range(n_pairs):
            sl = slice(pr * LANES, (pr + 1) * LANES)
            q = qc_ref[0, rows, sl]
            if g == 0:
                k2 = jnp.concatenate([kp_ref[0, :, sl], kc_ref[0, rows, sl]], axis=0)
                v2 = jnp.concatenate([vp_ref[0, :, sl], vc_ref[0, rows, sl]], axis=0)
            else:
                k2 = jnp.concatenate([kc_ref[0, prev, sl], kc_ref[0, rows, sl]], axis=0)
                v2 = jnp.concatenate([vc_ref[0, prev, sl], vc_ref[0, rows, sl]], axis=0)
            acc = None
            lse = None
            for half in (low, jnp.logical_not(low)):
                qh = jnp.where(half, q, jnp.zeros_like(q))
                vh = jnp.where(half, v2, jnp.zeros_like(v2))
                s = lax.dot_general(qh, k2, _NT, preferred_element_type=F32) * scale
                s = jnp.where(mask, s, -jnp.inf)
                m = jnp.max(s, axis=-1, keepdims=True)
                p = jnp.exp(s - m)
                den = jnp.sum(p, axis=-1, keepdims=True)
                oh = jnp.dot(p.astype(BF16), vh, preferred_element_type=F32) / den
                lh = m + jnp.log(den)
                acc = oh if acc is None else acc + oh
                lse = jnp.broadcast_to(lh, (blk, LANES)) if lse is None else jnp.where(low, lse, lh)
            o_ref[0, rows, sl] = acc.astype(o_ref.dtype)
            lse_ref[0, rows, sl] = lse


def _dilated_attention(aq, ak, av, dil, group):
    b, m, width = aq.shape
    width //= dil
    nb = m // ATTN_BLOCK
    blocks = group
    group = min(blocks, nb)
    classes = min(dil, blocks // group)
    tq = group * ATTN_BLOCK
    cur = pl.BlockSpec((1, tq, classes * width), lambda bi, r, n: (bi, n, r))
    prv = pl.BlockSpec((1, ATTN_BLOCK, classes * width), lambda bi, r, n: (bi, jnp.maximum(n * group - 1, 0), r))
    o, lse = pl.pallas_call(
        functools.partial(_attn_kernel, group=group, scale=ATTN_HEAD_DIM ** -0.5),
        grid=(b, dil // classes, nb // group),
        in_specs=[cur, cur, prv, cur, prv],
        out_specs=[cur, cur],
        out_shape=[jax.ShapeDtypeStruct((b, m, dil * width), BF16),
                   jax.ShapeDtypeStruct((b, m, dil * width), F32)],
        compiler_params=_cparams(("arbitrary", "arbitrary", "arbitrary")),
        name=f"dilated_attn_d{dil}",
    )(aq, ak, ak, av, av)
    return o, lse


def _split_bf16(v):
    hi = v.astype(BF16)
    return hi, (v - hi.astype(F32)).astype(BF16)


def _out_proj_kernel(oa_ref, *refs, dils, tile_t):
    n_pat = len(dils)
    o_refs, l_refs = refs[:n_pat], refs[n_pat:2 * n_pat]
    (x_ref, gb_ref, hm_ref, wa_ref, wb_ref, g2_ref, rwh_ref, rwl_ref, rb_ref,
     x1_ref, h2_ref) = refs[2 * n_pat:2 * n_pat + 11]
    route_refs = refs[2 * n_pat + 11:2 * n_pat + 19]
    carry_ref = refs[2 * n_pat + 19]
    slabs = refs[2 * n_pat + 20:]
    ratio = min(d for d in dils if d > 1)

    @pl.when(pl.program_id(0) == 0)
    def _():
        carry_ref[...] = jnp.zeros_like(carry_ref)

    outs = [_from_classes(o_refs[p], slabs[4 * p], slabs[4 * p + 1], dils[p], ratio) for p in range(n_pat)]
    lses = [_from_classes(l_refs[p], slabs[4 * p + 2], slabs[4 * p + 3], dils[p], ratio) for p in range(n_pat)]
    mx = functools.reduce(jnp.maximum, lses)
    es = [jnp.exp(l - mx) for l in lses]
    ob = sum(e * o for e, o in zip(es, outs)) / sum(es)
    sq_hi, sq_lo = _split_bf16(ob * ob)
    ms = (jnp.dot(sq_hi, hm_ref[...], preferred_element_type=F32)
          + jnp.dot(sq_lo, hm_ref[...], preferred_element_type=F32))
    obn = (ob * lax.rsqrt(ms + NORM_EPS) * gb_ref[...]).astype(BF16)
    y = jnp.dot(oa_ref[...], wa_ref[...], preferred_element_type=F32)
    y = y + jnp.dot(obn, wb_ref[...], preferred_element_type=F32)
    x1 = x_ref[...] + y
    x1_ref[...] = x1
    h2 = (x1 * lax.rsqrt(jnp.mean(x1 * x1, axis=-1, keepdims=True) + NORM_EPS)) * g2_ref[...]
    h_hi, h_lo = _split_bf16(h2)
    h2_ref[...] = h_hi
    lg = (jnp.dot(h_hi, rwh_ref[...], preferred_element_type=F32)
          + jnp.dot(h_hi, rwl_ref[...], preferred_element_type=F32)
          + jnp.dot(h_lo, rwh_ref[...], preferred_element_type=F32)) + rb_ref[...]
    for u in range(lg.shape[0] // tile_t):
        _route_tile(lg[u * tile_t:(u + 1) * tile_t].T, u, *route_refs, carry_ref)


def _out_proj(oa, obs, lses, dils, x2, gain_b, head_mean, w_a, w_b, gain2, rw_hi, rw_lo, rb, tile_m, tile_t,
              n_experts):
    n, d = x2.shape
    wm = oa.shape[1]
    nt, per_step = n // tile_t, tile_m // tile_t
    ne = pl.cdiv(n_experts, SUBLANES) * SUBLANES
    row = lambda w: pl.BlockSpec((tile_m, w), lambda i: (i, 0))
    cls = [pl.BlockSpec((tile_m // dil, dil * wm), lambda i: (i, 0)) for dil in dils]
    full = lambda a: pl.BlockSpec(a.shape, lambda i: (0, 0))
    tab = pl.BlockSpec((per_step, ne, LANES), lambda i: (i, 0, 0))
    tab_shape = jax.ShapeDtypeStruct((nt, ne, LANES), jnp.int32)
    consts = (gain_b, head_mean, w_a, w_b, gain2, rw_hi, rw_lo, rb)
    return pl.pallas_call(
        functools.partial(_out_proj_kernel, dils=dils, tile_t=tile_t),
        grid=(n // tile_m,),
        in_specs=[row(wm)] + cls + cls + [row(d)] + [full(a) for a in consts],
        out_specs=[row(d), row(d), row(LANES), row(LANES),
                   pl.BlockSpec((per_step * SUBLANES, tile_t), lambda i: (i, 0)), tab, tab, tab, tab,
                   pl.BlockSpec((ne, LANES), lambda i: (0, 0))],
        out_shape=[jax.ShapeDtypeStruct((n, d), F32), jax.ShapeDtypeStruct((n, d), BF16),
                   jax.ShapeDtypeStruct((n, LANES), F32), jax.ShapeDtypeStruct((n, LANES), jnp.int32),
                   jax.ShapeDtypeStruct((nt * SUBLANES, tile_t), jnp.int32), tab_shape, tab_shape, tab_shape,
                   tab_shape, jax.ShapeDtypeStruct((ne, LANES), jnp.int32)],
        scratch_shapes=[pltpu.VMEM((ne, LANES), F32)]
        + [pltpu.VMEM((wm // LANES, tile_m, LANES), F32) for _ in range(4 * len(dils))],
        compiler_params=_cparams(("arbitrary",)),
        name="out_proj",
    )(oa, *obs, *lses, x2, *consts)


def _route_tile(logits_t, u, gate_ref, pos_ref, post_ref, start_ref, nch_ref, nfull_ref, off_ref, cnt_ref,
                carry_ref):
    ne, tt = carry_ref.shape[0], logits_t.shape[1]

    logits = logits_t[0:ne, :]
    erow = lax.broadcasted_iota(jnp.int32, (ne, tt), 0).astype(F32)
    vals, idxs = [], []
    for _ in range(TOP_K):
        m = jnp.max(logits, axis=0, keepdims=True)
        ix = jnp.min(jnp.where(logits == m, erow, float(ne)), axis=0, keepdims=True)
        vals.append(m)
        idxs.append(ix)
        logits = jnp.where(erow == ix, -jnp.inf, logits)
    exps = [jnp.exp(v - vals[0]) for v in vals]
    den = exps[0] + exps[1] + exps[2] + exps[3]
    chosen = jnp.zeros((ne, tt), F32)
    for ix in idxs:
        chosen = chosen + jnp.where(erow == ix, 1.0, 0.0)
    s = lax.broadcasted_iota(jnp.int32, (tt, tt), 0)
    t = lax.broadcasted_iota(jnp.int32, (tt, tt), 1)
    before = jnp.dot(chosen.astype(BF16), jnp.where(s < t, 1.0, 0.0).astype(BF16), preferred_element_type=F32)
    count = jnp.broadcast_to(jnp.sum(chosen, axis=1, keepdims=True), (ne, LANES))
    carry = carry_ref[...]
    head = carry - MOE_CHUNK * jnp.floor(carry * (1.0 / MOE_CHUNK))
    nfull = jnp.floor((head + count) * (1.0 / MOE_CHUNK))
    nch = jnp.floor((head + count + (MOE_CHUNK - 1)) * (1.0 / MOE_CHUNK))
    er = lax.broadcasted_iota(jnp.int32, (ne, ne), 0)
    ec = lax.broadcasted_iota(jnp.int32, (ne, ne), 1)
    off = MOE_CHUNK * jnp.dot(jnp.where(ec < er, 1.0, 0.0).astype(BF16), nch.astype(BF16),
                              preferred_element_type=F32)
    slot = before + jnp.concatenate([off + head] * (tt // LANES), axis=1)
    krow = lax.broadcasted_iota(jnp.int32, (LANES, tt), 0)
    gate_t = jnp.zeros((LANES, tt), F32)
    pos_t = jnp.zeros((LANES, tt), F32)
    for k in range(TOP_K):
        pk = jnp.sum(jnp.where(erow == idxs[k], slot, 0.0), axis=0, keepdims=True)
        gate_t = jnp.where(krow == k, exps[k] / den, gate_t)
        pos_t = jnp.where(krow == k, pk, pos_t)
    gate_ref[u * tt:(u + 1) * tt, :] = gate_t.T
    pos_ref[u * tt:(u + 1) * tt, :] = pos_t.T.astype(jnp.int32)
    post_ref[u * SUBLANES:(u + 1) * SUBLANES, :] = pos_t[0:SUBLANES].astype(jnp.int32)
    start_ref[u] = (carry - head).astype(jnp.int32)
    nch_ref[u] = nch.astype(jnp.int32)
    nfull_ref[u] = nfull.astype(jnp.int32)
    off_ref[u] = off.astype(jnp.int32)
    carry_ref[...] = carry + count
    cnt_ref[...] = (carry + count).astype(jnp.int32)


def _for_each_chunk(tab_ref, n, fn):
    def one(c):
        fn(pl.multiple_of(c * MOE_CHUNK, MOE_CHUNK), pl.multiple_of(tab_ref[0, 0, c], MOE_CHUNK))

    def four(q, carry):
        for u in range(4):
            one(q * 4 + u)
        return carry

    def single(c, carry):
        one(c)
        return carry

    lax.fori_loop(0, lax.shift_right_logical(n, 2), four, 0)
    lax.fori_loop(n & ~3, n, single, 0)


def _pad_sizes():
    sizes, s = [], MOE_ROWS // 2
    while s >= MOE_CHUNK:
        sizes.append(s)
        s //= 2
    return sizes


def _scatter_kernel(nch_ref, nfull_ref, off_ref, total_ref, padrow_ref, padlen_ref, nused_ref, post_ref, tab_ref,
                    tabp_ref, h_ref, xout_ref, stage_ref, open_ref, zero_ref, sems, zsem, *, n_experts):
    rs, tt = stage_ref.shape[1], h_ref.shape[0]
    i, nt = pl.program_id(0), pl.num_programs(0)
    slot = i % 2
    base = i * n_experts
    stage = stage_ref.at[slot]

    @pl.when(i == 0)
    def _():
        open_ref[...] = jnp.zeros_like(open_ref)
        zero_ref[...] = jnp.zeros_like(zero_ref)
        stage_ref[...] = jnp.zeros_like(stage_ref)

    post = post_ref[...]
    main = _stage_main_rows(tt, n_experts)

    def select_rows(r0, r1):
        srow = (lax.broadcasted_iota(jnp.int32, (r1 - r0, tt), 0) + r0).astype(jnp.int16)
        post16 = post.astype(jnp.int16)
        hit = srow == post16[0:1]
        for k in range(1, TOP_K):
            hit = hit | (srow == post16[k:k + 1])
        sel = jnp.where(hit, jnp.ones((), BF16), jnp.zeros((), BF16))
        stage[r0:r1, :] = jnp.dot(sel, h_ref[...], preferred_element_type=F32)

    select_rows(0, main)
    if main < rs:
        @pl.when(total_ref[i] * MOE_CHUNK > main)
        def _():
            select_rows(main, rs)

    def add_open(e, carry):
        rows = pl.ds(pl.multiple_of(off_ref[base + e], MOE_CHUNK), MOE_CHUNK)
        stage[rows, :] = stage[rows, :] + open_ref[e]
        return carry

    def save_open(e, carry):
        nfull = nfull_ref[base + e]
        rows = pl.ds(pl.multiple_of(off_ref[base + e] + nfull * MOE_CHUNK, MOE_CHUNK), MOE_CHUNK)
        still_open = nch_ref[base + e] > nfull
        open_ref[e] = jnp.where(still_open, stage[rows, :], 0.0)
        return carry

    def chunk_copy(sl, srow0, xrow0):
        return pltpu.make_async_copy(stage_ref.at[sl, pl.ds(srow0, MOE_CHUNK)],
                                     xout_ref.at[pl.ds(xrow0, MOE_CHUNK)], sems.at[sl])

    def pad_copies(fn):
        def zeros_to(start, size):
            fn(pltpu.make_async_copy(zero_ref.at[pl.ds(0, size)], xout_ref.at[pl.ds(start, size)], zsem))

        def expert(e, carry):
            row, length = pl.multiple_of(padrow_ref[e], MOE_CHUNK), padlen_ref[e]
            for size in _pad_sizes():
                @pl.when((length & size) != 0)
                def _():
                    zeros_to(pl.multiple_of(row + (length & ~(2 * size - 1)), MOE_CHUNK), size)
            return carry

        def unused_half_block(c, carry):
            zeros_to(pl.multiple_of(c * (MOE_ROWS // 2), MOE_ROWS // 2), MOE_ROWS // 2)
            return carry

        lax.fori_loop(0, n_experts, expert, 0)
        lax.fori_loop(2 * nused_ref[0], xout_ref.shape[0] // (MOE_ROWS // 2), unused_half_block, 0)

    lax.fori_loop(0, n_experts, add_open, 0, unroll=8)

    @pl.when(i > 0)
    def _():
        _for_each_chunk(tabp_ref, total_ref[i - 1], lambda s0, x0: chunk_copy(1 - slot, s0, x0).wait())

    _for_each_chunk(tab_ref, total_ref[i], lambda s0, x0: chunk_copy(slot, s0, x0).start())
    lax.fori_loop(0, n_experts, save_open, 0, unroll=8)

    @pl.when(i == nt - 1)
    def _():
        pad_copies(lambda cp: cp.start())
        _for_each_chunk(tab_ref, total_ref[i], lambda s0, x0: chunk_copy(slot, s0, x0).wait())
        pad_copies(lambda cp: cp.wait())


def _stage_main_rows(tile_t, n_experts):
    return min(pl.cdiv(tile_t * TOP_K + n_experts * (MOE_CHUNK - 1), LANES) * LANES, _stage_rows(tile_t, n_experts))


def _stage_rows(tile_t, n_experts):
    rows = pl.cdiv(tile_t * TOP_K + 2 * n_experts * (MOE_CHUNK - 1) + MOE_CHUNK, LANES) * LANES
    assert rows < 2 ** 15
    return rows


def _chunk_table_spec(table, shift):
    nt, _, width = table.shape
    return pl.BlockSpec((1, 1, width), lambda i, *_: (jnp.clip(i + shift, 0, nt - 1), 0, 0), memory_space=pltpu.SMEM)


def _moe_scatter(nch, nfull, off, total, padrow, padlen, n_used, post, table, h2, n_rows, tile_t, n_experts):
    n, d = h2.shape
    tab = lambda shift: _chunk_table_spec(table, shift)
    return pl.pallas_call(
        functools.partial(_scatter_kernel, n_experts=n_experts),
        grid_spec=pltpu.PrefetchScalarGridSpec(
            num_scalar_prefetch=7,
            grid=(n // tile_t,),
            in_specs=[pl.BlockSpec((SUBLANES, tile_t), lambda i, *_: (i, 0)), tab(0), tab(-1),
                      pl.BlockSpec((tile_t, d), lambda i, *_: (i, 0))],
            out_specs=pl.BlockSpec(memory_space=pl.ANY),
            scratch_shapes=[pltpu.VMEM((2, _stage_rows(tile_t, n_experts), d), F32),
                            pltpu.VMEM((n_experts, MOE_CHUNK, d), F32),
                            pltpu.VMEM((MOE_ROWS // 2, d), F32),
                            pltpu.SemaphoreType.DMA((2,)), pltpu.SemaphoreType.DMA]),
        out_shape=jax.ShapeDtypeStruct((n_rows, d), F32),
        compiler_params=_cparams(("arbitrary",)),
        name="moe_scatter",
    )(nch, nfull, off, total, padrow, padlen, n_used, post, table, table, h2)


def _expert_kernel(be_ref, nu_ref, valid_ref, slot_ref, next_ref, x_ref, bg_ref, bl_ref, bd_ref, wu_hbm, wd_hbm,
                   y_ref, wu_buf, wd_buf, wg_s, wl_s, wd_s, sems):
    i = pl.program_id(0)
    used = i < nu_ref[0]
    e = be_ref[i]
    new_expert = (i == 0) | (e != be_ref[jnp.maximum(i - 1, 0)])
    slot = slot_ref[e]

    def weight_copies(expert, sl):
        return (pltpu.make_async_copy(wu_hbm.at[expert], wu_buf.at[sl], sems.at[0, sl]),
                pltpu.make_async_copy(wd_hbm.at[expert], wd_buf.at[sl], sems.at[1, sl]))

    @pl.when(used & (i == 0))
    def _():
        for cp in weight_copies(e, slot):
            cp.start()

    @pl.when(used & new_expert)
    def _():
        for cp in weight_copies(e, slot):
            cp.wait()
        nxt = next_ref[e]

        @pl.when(nxt >= 0)
        def _():
            for cp in weight_copies(nxt, 1 - slot):
                cp.start()

        r = lax.broadcasted_iota(jnp.int32, (2 * LANES, 2 * LANES), 0)
        c = lax.broadcasted_iota(jnp.int32, (2 * LANES, 2 * LANES), 1)
        pick = jnp.where(r == jnp.where(c < LANES, 2 * c, 2 * (c - LANES) + 1), 1.0, 0.0).astype(BF16)
        for cb in range(wg_s.shape[1] // LANES):
            grp = wu_buf[slot, :, cb * 2 * LANES:(cb + 1) * 2 * LANES].astype(BF16)
            out = slice(cb * LANES, (cb + 1) * LANES)
            both = jnp.dot(grp, pick, preferred_element_type=F32).astype(BF16)
            wg_s[:, out] = both[:, :LANES]
            wl_s[:, out] = both[:, LANES:]
        wd_s[...] = wd_buf[slot].astype(BF16)

    def ffn(rows):
        x = x_ref[rows, :].astype(BF16)
        glu = jnp.dot(x, wg_s[...], preferred_element_type=F32) + bg_ref[0]
        lin = jnp.dot(x, wl_s[...], preferred_element_type=F32) + bl_ref[0]
        glu = jnp.minimum(glu, SWIGLU_LIMIT)
        lin = jnp.clip(lin, -SWIGLU_LIMIT, SWIGLU_LIMIT)
        h = glu * jax.nn.sigmoid(SWIGLU_ALPHA * glu) * (lin + 1.0)
        y_ref[rows, :] = jnp.dot(h.astype(BF16), wd_s[...], preferred_element_type=F32) + bd_ref[0]

    half = MOE_ROWS // 2
    full = used & (valid_ref[i] > half)

    @pl.when(full)
    def _():
        ffn(slice(0, MOE_ROWS))

    @pl.when(used & jnp.logical_not(full))
    def _():
        ffn(slice(0, half))
        y_ref[half:, :] = jnp.zeros((MOE_ROWS - half, y_ref.shape[1]), y_ref.dtype)

    @pl.when(jnp.logical_not(used))
    def _():
        y_ref[...] = jnp.zeros_like(y_ref)


def _moe_experts(block_e, n_used, valid, buf_slot, next_expert, xbuf, w_up, bg, bl, w_down, bd):
    n_rows, d = xbuf.shape
    de = w_down.shape[1]
    nblk = n_rows // MOE_ROWS
    rows = pl.BlockSpec((MOE_ROWS, d), lambda i, be, nu, *_: (i, 0))
    rows_in = pl.BlockSpec((MOE_ROWS, d), lambda i, be, nu, *_: (jnp.minimum(i, jnp.maximum(nu[0] - 1, 0)), 0))
    bias = lambda a: pl.BlockSpec((1,) + a.shape[1:], lambda i, be, *_: (be[i], 0, 0))
    hbm = pl.BlockSpec(memory_space=pl.ANY)
    return pl.pallas_call(
        _expert_kernel,
        grid_spec=pltpu.PrefetchScalarGridSpec(
            num_scalar_prefetch=5,
            grid=(nblk,),
            in_specs=[rows_in, bias(bg), bias(bl), bias(bd), hbm, hbm],
            out_specs=rows,
            scratch_shapes=[pltpu.VMEM((2,) + w_up.shape[1:], F32), pltpu.VMEM((2,) + w_down.shape[1:], F32),
                            pltpu.VMEM((d, de), BF16), pltpu.VMEM((d, de), BF16), pltpu.VMEM((de, d), BF16),
                            pltpu.SemaphoreType.DMA((2, 2))]),
        out_shape=jax.ShapeDtypeStruct((n_rows, d), F32),
        compiler_params=_cparams(("arbitrary",)),
        name="moe_experts",
    )(block_e, n_used, valid, buf_slot, next_expert, xbuf, bg, bl, bd, w_up, w_down)


def _combine_kernel(total_ref, pos_ref, gate_ref, x1_ref, g_ref, tab_ref, tabn_ref, y_ref, o_ref, stage_ref, acc_ref,
                    sems, *, n_experts):
    rs, tt = stage_ref.shape[1], x1_ref.shape[0]
    i, nt = pl.program_id(0), pl.num_programs(0)
    slot = i % 2

    def chunk_copy(sl, srow0, yrow0):
        return pltpu.make_async_copy(y_ref.at[pl.ds(yrow0, MOE_CHUNK)],
                                     stage_ref.at[sl, pl.ds(srow0, MOE_CHUNK)], sems.at[sl])

    @pl.when(i == 0)
    def _():
        stage_ref[...] = jnp.zeros_like(stage_ref)
        _for_each_chunk(tab_ref, total_ref[i], lambda s0, y0: chunk_copy(slot, s0, y0).start())

    @pl.when(i + 1 < nt)
    def _():
        _for_each_chunk(tabn_ref, total_ref[i + 1], lambda s0, y0: chunk_copy(1 - slot, s0, y0).start())

    pos, gates = pos_ref[...], gate_ref[...]
    main = _stage_main_rows(tt, n_experts)

    def weighted_rows(r0, r1):
        scol = (lax.broadcasted_iota(jnp.int32, (tt, r1 - r0), 1) + r0).astype(jnp.int16)
        pos16, gates16 = pos.astype(jnp.int16), gates.astype(BF16)
        weights = jnp.zeros((tt, r1 - r0), BF16)
        for k in range(TOP_K):
            weights = jnp.where(scol == pos16[:, k:k + 1], gates16[:, k:k + 1], weights)
        return jnp.dot(weights, stage_ref[slot, r0:r1, :].astype(BF16), preferred_element_type=F32)

    _for_each_chunk(tab_ref, total_ref[i], lambda s0, y0: chunk_copy(slot, s0, y0).wait())
    acc_ref[...] = x1_ref[...] + weighted_rows(0, main)
    if main < rs:
        @pl.when(total_ref[i] * MOE_CHUNK > main)
        def _():
            acc_ref[...] += weighted_rows(main, rs)
    acc = acc_ref[...]
    o_ref[...] = (acc * lax.rsqrt(jnp.mean(acc * acc, axis=-1, keepdims=True) + NORM_EPS)) * g_ref[...]


def _moe_combine(total, pos, gates, x1, gain, table, ybuf, tile_t, n_experts):
    n, d = x1.shape
    return pl.pallas_call(
        functools.partial(_combine_kernel, n_experts=n_experts),
        grid_spec=pltpu.PrefetchScalarGridSpec(
            num_scalar_prefetch=1,
            grid=(n // tile_t,),
            in_specs=[pl.BlockSpec((tile_t, LANES), lambda i, *_: (i, 0)),
                      pl.BlockSpec((tile_t, LANES), lambda i, *_: (i, 0)),
                      pl.BlockSpec((tile_t, d), lambda i, *_: (i, 0)),
                      pl.BlockSpec((1, d), lambda i, *_: (0, 0)),
                      _chunk_table_spec(table, 0), _chunk_table_spec(table, 1),
                      pl.BlockSpec(memory_space=pl.ANY)],
            out_specs=pl.BlockSpec((tile_t, d), lambda i, *_: (i, 0)),
            scratch_shapes=[pltpu.VMEM((2, _stage_rows(tile_t, n_experts), d), F32),
                            pltpu.VMEM((tile_t, d), F32), pltpu.SemaphoreType.DMA((2,))]),
        out_shape=jax.ShapeDtypeStruct((n, d), F32),
        compiler_params=_cparams(("arbitrary",)),
        name="moe_combine",
    )(total, pos, gates, x1, gain, table, table, ybuf)


def kernel(x, norm1_g, w_in, hgrn_lb_logits, hgrn_norm_g, attn_norm_g, w_out, norm2_g, router_w, router_b,
           w_up, b_up, w_down, b_down, final_norm_g):
    b, s, d = x.shape
    n = b * s
    depth = w_in.shape[0]
    n_experts = router_w.shape[-1]
    d_mix = w_out.shape[1]
    d_hgrn = d_mix // 2
    d_attn = d_mix - d_hgrn
    assert w_in.shape[-1] == 4 * d_hgrn + 3 * d_attn and d_hgrn == d_attn
    assert all(s % win == 0 and win // dil == ATTN_BLOCK for win, dil in DILATED_PATTERNS)
    assert s % HGRN_ROWS == 0 and n % IN_PROJ_ROWS == 0 and n % PROJ_ROWS == 0 and PROJ_ROWS % ROUTE_TOKENS == 0
    assert n_experts <= LANES
    assert depth == 1, "the final rmsnorm is fused into the single layer's MoE combine"

    lb_all = jnp.cumsum(jax.nn.softmax(hgrn_lb_logits.astype(F32), axis=0), axis=0)
    lane = jnp.arange(d_attn)
    head_mean = jnp.where((lane[:, None] // ATTN_HEAD_DIM) == (lane[None, :] // ATTN_HEAD_DIM),
                          1.0 / ATTN_HEAD_DIM, 0.0).astype(BF16)
    x2 = x.reshape(n, d)
    l = 0
    dils = tuple(dil for _, dil in DILATED_PATTERNS)
    n_pat = len(dils)
    hq, hf, hi, hg, *attn = _in_proj(
        x2, norm1_g[l].reshape(1, d), w_in[l].astype(BF16),
        (BF16, F32, BF16, BF16), 3, dils, tile_m=IN_PROJ_ROWS)
    to3 = lambda t: t.reshape(b, s, -1)
    o_a = _hgrn2(to3(hq), to3(hf), to3(hi), to3(hg), lb_all[l].reshape(1, d_hgrn),
                 hgrn_norm_g[l].reshape(1, d_hgrn), tile_s=HGRN_ROWS)
    obs, lses = [], []
    for p, dil in enumerate(dils):
        aq, ak, av = (attn[a * n_pat + p].reshape(b, s // dil, dil * d_attn) for a in range(3))
        o_p, lse_p = _dilated_attention(aq, ak, av, dil, group=ATTN_BLOCKS)
        obs.append(o_p.reshape(n // dil, dil * d_attn))
        lses.append(lse_p.reshape(n // dil, dil * d_attn))
    rw = jnp.zeros((d, LANES), F32).at[:, :n_experts].set(router_w[l])
    rw_hi = rw.astype(BF16)
    rw_lo = (rw - rw_hi.astype(F32)).astype(BF16)
    rb = jnp.full((1, LANES), -jnp.inf, F32).at[0, :n_experts].set(router_b[l])
    w_o = w_out[l].astype(BF16)
    tile_t = ROUTE_TOKENS
    x1, h2, gates, pos, post, start, nch, nfull, off, counts = _out_proj(
        o_a.reshape(n, d_hgrn), obs, lses, dils, x2, attn_norm_g[l].reshape(1, d_attn), head_mean,
        w_o[:d_hgrn], w_o[d_hgrn:], norm2_g[l].reshape(1, d), rw_hi, rw_lo, rb, PROJ_ROWS, tile_t, n_experts)

    counts = counts[:n_experts, 0]
    padded = (counts + MOE_ROWS - 1) // MOE_ROWS * MOE_ROWS
    pad_end = jnp.cumsum(padded)
    pad_start = pad_end - padded
    n_rows = n * TOP_K + n_experts * MOE_ROWS
    nblk = n_rows // MOE_ROWS
    block_row = jnp.arange(nblk, dtype=jnp.int32) * MOE_ROWS
    block_e = jnp.minimum(jnp.sum(pad_end[None, :] <= block_row[:, None], axis=1), n_experts - 1).astype(jnp.int32)
    n_used = (pad_end[-1:] // MOE_ROWS).astype(jnp.int32)
    eid = jnp.arange(n_experts, dtype=jnp.int32)
    group_end = jnp.sum(jnp.where(eid[None, :] == block_e[:, None], (pad_start + counts)[None, :], 0), axis=1)
    valid = jnp.clip(group_end - block_row, 0, MOE_ROWS).astype(jnp.int32)
    has_rows = padded > 0
    buf_slot = ((jnp.cumsum(has_rows) - 1) % 2).astype(jnp.int32)
    following = jnp.min(jnp.where(has_rows[None, :] & (eid[None, :] > eid[:, None]), eid[None, :], n_experts), axis=1)
    next_expert = jnp.where(following < n_experts, following, -1).astype(jnp.int32)
    per_expert = lambda t: t[:, :n_experts, 0]
    seg = per_expert(start) + pad_start[None, :].astype(jnp.int32)
    nch, nfull, off = per_expert(nch), per_expert(nfull), per_expert(off)
    total = jnp.sum(nch, axis=1).astype(jnp.int32)
    first = off // MOE_CHUNK
    max_chunks = pl.cdiv(_stage_rows(tile_t, n_experts) // MOE_CHUNK, LANES) * LANES
    chunk = jnp.arange(max_chunks, dtype=jnp.int32)[None, :, None]
    owned = (first[:, None, :] <= chunk) & (chunk < (first + nch)[:, None, :])
    table = jnp.sum(jnp.where(owned, (seg - first * MOE_CHUNK)[:, None, :], 0), axis=-1) + chunk[:, :, 0] * MOE_CHUNK
    table = table.astype(jnp.int32)[:, None, :]
    filled = (counts + MOE_CHUNK - 1) // MOE_CHUNK * MOE_CHUNK
    padrow = (pad_start + filled).astype(jnp.int32)
    padlen = (padded - filled).astype(jnp.int32)

    xbuf = _moe_scatter(nch.reshape(-1), nfull.reshape(-1), off.reshape(-1), total, padrow, padlen, n_used, post,
                        table, h2, n_rows, tile_t, n_experts)
    ybuf = _moe_experts(block_e, n_used, valid, buf_slot, next_expert, xbuf, w_up[l], b_up[l][:, None, 0::2],
                        b_up[l][:, None, 1::2], w_down[l], b_down[l][:, None, :])
    out = _moe_combine(total, pos, gates, x1, final_norm_g.reshape(1, d), table, ybuf, tile_t, n_experts)
    return out.reshape(b, s, d)
```

```python
import functools

import jax
import jax.numpy as jnp
from jax import lax
from jax.experimental import pallas as pl
from jax.experimental.pallas import tpu as pltpu

F32 = jnp.float32
BF16 = jnp.bfloat16

NORM_EPS = 1e-6
HGRN_HEAD_DIM = 128
ATTN_HEAD_DIM = 64
DILATED_PATTERNS = ((128, 1), (512, 4), (2048, 16))
TOP_K = 4
SWIGLU_ALPHA = 1.702
SWIGLU_LIMIT = 7.0

LANES = 128
SUBLANES = 8
HGRN_CHUNK = 64
ATTN_BLOCK = 128
MOE_ROWS = 512
MOE_CHUNK = SUBLANES
V7X_VMEM_BYTES = 64 * 1024 * 1024
VMEM_LIMIT = V7X_VMEM_BYTES - 8 * 1024 * 1024

IN_PROJ_ROWS = 1024
PROJ_ROWS = 512
HGRN_ROWS = 1024
ATTN_BLOCKS = 8
ROUTE_TOKENS = 256

_NT = (((1,), (1,)), ((), ()))
_TN = (((0,), (0,)), ((), ()))


def _cparams(sem):
    return pltpu.CompilerParams(dimension_semantics=sem, vmem_limit_bytes=VMEM_LIMIT)


def _to_classes(res, o_refs, dils, slab_refs):
    tm, width = res.shape
    n_slab = width // LANES
    for sb in range(n_slab):
        slab_refs[0][sb] = res[:, sb * LANES:(sb + 1) * LANES]
    prev = 1
    for p, (o_ref, dil) in enumerate(zip(o_refs, dils)):
        if dil == 1:
            o_ref[...] = res.astype(o_ref.dtype)
            continue
        ratio, src, last = dil // prev, slab_refs[0] if prev == 1 else slab_refs[1], p == len(dils) - 1
        assert dil % prev == 0 and (prev == 1 or last), "one intermediate slab: at most two strided levels"
        rows = tm // dil
        for rp in range(prev):
            for r2 in range(ratio):
                r = r2 * prev + rp
                for sb in range(n_slab):
                    part = src[sb, pl.ds(rp * (tm // prev) + r2, rows, stride=ratio), :]
                    col = r * width + sb * LANES
                    o_ref[:, col:col + LANES] = part.astype(o_ref.dtype)
                    if not last:
                        slab_refs[1][sb, r * rows:(r + 1) * rows, :] = part
        prev = dil


def _from_classes(ref, slab_ref, mid_ref, dil, ratio):
    if dil == 1:
        return ref[...].astype(F32)
    rows = ref.shape[0]
    tm, width = rows * dil, ref.shape[1] // dil
    n_slab = width // LANES
    assert dil in (ratio, ratio * ratio)
    prev = dil // ratio
    for rp in range(prev):
        for r2 in range(ratio):
            r = r2 * prev + rp
            for sb in range(n_slab):
                col = r * width + sb * LANES
                part = ref[:, col:col + LANES].astype(F32)
                if prev == 1:
                    slab_ref[sb, pl.ds(r2, rows, stride=ratio), :] = part
                else:
                    mid_ref[sb, pl.ds(rp * (tm // prev) + r2, rows, stride=ratio), :] = part
    if prev > 1:
        seg = tm // prev
        for rp in range(prev):
            for sb in range(n_slab):
                slab_ref[sb, pl.ds(rp, seg, stride=prev), :] = mid_ref[sb, rp * seg:(rp + 1) * seg, :]
    return jnp.concatenate([slab_ref[sb] for sb in range(n_slab)], axis=-1)


def _in_proj_kernel(x_ref, g_ref, w_ref, *refs, plain_dtypes, n_attn, dils):
    out_refs, slab_refs = refs[:-2], refs[-2:]
    x = x_ref[...]
    r = lax.rsqrt(jnp.mean(x * x, axis=-1, keepdims=True) + NORM_EPS)
    h = ((x * r) * g_ref[...]).astype(BF16)
    width = out_refs[0].shape[-1]
    n_plain = len(plain_dtypes)
    for j in list(range(n_plain, n_plain + n_attn)) + list(range(n_plain)):
        res = jnp.dot(h, w_ref[:, j * width:(j + 1) * width], preferred_element_type=F32)
        if j < n_plain:
            out_refs[j][...] = res.astype(out_refs[j].dtype)
        else:
            first = n_plain + (j - n_plain) * len(dils)
            _to_classes(res, out_refs[first:first + len(dils)], dils, slab_refs)


def _in_proj(x2, gain, w_bf16, plain_dtypes, n_attn, dils, tile_m):
    n, d = x2.shape
    width = w_bf16.shape[1] // (len(plain_dtypes) + n_attn)
    out_specs = [pl.BlockSpec((tile_m, width), lambda i: (i, 0)) for _ in plain_dtypes]
    out_shape = [jax.ShapeDtypeStruct((n, width), dt) for dt in plain_dtypes]
    for _ in range(n_attn):
        for dil in dils:
            out_specs.append(pl.BlockSpec((tile_m // dil, dil * width), lambda i: (i, 0)))
            out_shape.append(jax.ShapeDtypeStruct((n // dil, dil * width), BF16))
    return pl.pallas_call(
        functools.partial(_in_proj_kernel, plain_dtypes=plain_dtypes, n_attn=n_attn, dils=dils),
        grid=(n // tile_m,),
        in_specs=[pl.BlockSpec((tile_m, d), lambda i: (i, 0)),
                  pl.BlockSpec((1, d), lambda i: (0, 0)),
                  pl.BlockSpec(w_bf16.shape, lambda i: (0, 0), pipeline_mode=pl.Buffered(1))],
        out_specs=out_specs,
        out_shape=out_shape,
        scratch_shapes=[pltpu.VMEM((width // LANES, tile_m, LANES), F32) for _ in range(2)],
        compiler_params=_cparams(("arbitrary",)),
        name="in_proj",
    )(x2, gain, w_bf16)


def _hgrn_kernel(q_ref, f_ref, i_ref, g_ref, lb_ref, gain_ref, o_ref, state_ref, *, n_heads):
    hd = HGRN_HEAD_DIM
    width = n_heads * hd
    nblk = HGRN_CHUNK // SUBLANES

    @pl.when(pl.program_id(1) == 0)
    def _():
        state_ref[...] = jnp.zeros_like(state_ref)

    lb = lb_ref[...]
    gain = gain_ref[...]
    row8 = lax.broadcasted_iota(jnp.int32, (SUBLANES, width), 0)
    rowc = lax.broadcasted_iota(jnp.int32, (SUBLANES, HGRN_CHUNK), 0)
    colc = lax.broadcasted_iota(jnp.int32, (SUBLANES, HGRN_CHUNK), 1)
    r64 = lax.broadcasted_iota(jnp.int32, (HGRN_CHUNK, HGRN_CHUNK), 0)
    c64 = lax.broadcasted_iota(jnp.int32, (HGRN_CHUNK, HGRN_CHUNK), 1)
    same32 = (r64 // 32) == (c64 // 32)
    same16 = (r64 // 16) == (c64 // 16)

    def scan8(x):
        for s in (1, 2, 4):
            x = x + jnp.where(row8 >= s, pltpu.roll(x, s, axis=0), 0.0)
        return x

    def cat(blocks):
        return jnp.concatenate(blocks, axis=0)

    def chunk(ci, carry):
        r0 = pl.multiple_of(ci * HGRN_CHUNK, HGRN_CHUNK)
        rows = pl.ds(r0, HGRN_CHUNK)
        q = q_ref[0, rows, :].astype(F32)
        v = i_ref[0, rows, :]
        f = lb + (1.0 - lb) * jax.nn.sigmoid(f_ref[0, rows, :])
        logf = jnp.log2(f)
        kk = 1.0 - f

        qb = [q[SUBLANES * i:SUBLANES * (i + 1)] for i in range(nblk)]
        kb = [kk[SUBLANES * i:SUBLANES * (i + 1)] for i in range(nblk)]
        b8 = [scan8(logf[SUBLANES * i:SUBLANES * (i + 1)]) for i in range(nblk)]
        t8 = [jnp.broadcast_to(x[SUBLANES - 1:SUBLANES], x.shape) for x in b8]
        b16 = [b8[i] + t8[i - 1] if i % 2 else b8[i] for i in range(nblk)]
        t16 = [t8[i - i % 2] + t8[i - i % 2 + 1] for i in range(nblk)]
        b32 = [b16[i] + t16[i - 2] if (i // 2) % 2 else b16[i] for i in range(nblk)]
        t32 = [t16[i - i % 4] + t16[i - i % 4 + 2] for i in range(nblk)]
        b64 = [b32[i] + t32[0] if i >= 4 else b32[i] for i in range(nblk)]
        t64 = t32[0] + t32[4]

        zero = jnp.zeros_like(qb[0])

        def q_side(bl, span):
            return cat([qb[i] * jnp.exp2(bl[i]) if (i * SUBLANES // span) % 2 else zero for i in range(nblk)])

        def k_side(bl, tl, span):
            return cat([zero if (i * SUBLANES // span) % 2 else kb[i] * jnp.exp2(tl[i] - bl[i]) for i in range(nblk)])

        q64 = cat([qb[i] * jnp.exp2(b64[i]) for i in range(nblk)]).astype(BF16)
        k64 = cat([kb[i] * jnp.exp2(t64 - b64[i]) for i in range(nblk)]).astype(BF16)
        q32, k32 = q_side(b32, 32).astype(BF16), k_side(b32, t32, 32).astype(BF16)
        q16, k16 = q_side(b16, 16).astype(BF16), k_side(b16, t16, 16).astype(BF16)
        q8, k8 = q_side(b8, 8).astype(BF16), k_side(b8, t8, 8).astype(BF16)
        decay = jnp.exp2(t64[0:1])

        diag = [[jnp.zeros((SUBLANES, HGRN_CHUNK), F32) for _ in range(nblk)] for _ in range(n_heads)]
        for j in range(SUBLANES):
            for i in range(nblk):
                if j == 0:
                    p = qb[i] * kb[i]
                else:
                    p = qb[i] * pltpu.roll(kb[i], j, axis=0) * jnp.exp2(b8[i] - pltpu.roll(b8[i], j, axis=0))
                hit = (colc == rowc + (SUBLANES * i - j)) & (rowc >= j)
                for h in range(n_heads):
                    w = jnp.sum(p[:, h * hd:(h + 1) * hd], axis=-1, keepdims=True)
                    diag[h][i] = jnp.where(hit, w, diag[h][i])

        outs = []
        for h in range(n_heads):
            sl = slice(h * hd, (h + 1) * hd)
            a = cat(diag[h])
            a = a + lax.dot_general(q32[:, sl], k32[:, sl], _NT, preferred_element_type=F32)
            a = a + jnp.where(same32, lax.dot_general(q16[:, sl], k16[:, sl], _NT, preferred_element_type=F32), 0.0)
            a = a + jnp.where(same16, lax.dot_general(q8[:, sl], k8[:, sl], _NT, preferred_element_type=F32), 0.0)
            st = state_ref[h]
            o = lax.dot_general(q64[:, sl], st.astype(BF16), _NT, preferred_element_type=F32)
            o = o + jnp.dot(a.astype(BF16), v[:, sl], preferred_element_type=F32)
            state_ref[h] = st * decay[:, sl] + lax.dot_general(v[:, sl], k64[:, sl], _TN,
                                                              preferred_element_type=F32)
            o = o * lax.rsqrt(jnp.mean(o * o, axis=-1, keepdims=True) + NORM_EPS)
            outs.append(o)
        o = jnp.concatenate(outs, axis=-1) * gain
        o_ref[0, rows, :] = (o * jax.nn.silu(g_ref[0, rows, :].astype(F32))).astype(o_ref.dtype)
        return carry

    lax.fori_loop(0, q_ref.shape[1] // HGRN_CHUNK, chunk, 0, unroll=8)


def _hgrn2(hq, hf, hi, hg, lb, gain, tile_s):
    b, s, width = hq.shape
    n_heads = width // HGRN_HEAD_DIM
    blk = pl.BlockSpec((1, tile_s, width), lambda bi, si: (bi, si, 0))
    vec = pl.BlockSpec((1, width), lambda bi, si: (0, 0))
    return pl.pallas_call(
        functools.partial(_hgrn_kernel, n_heads=n_heads),
        grid=(b, s // tile_s),
        in_specs=[blk, blk, blk, blk, vec, vec],
        out_specs=blk,
        out_shape=jax.ShapeDtypeStruct((b, s, width), BF16),
        scratch_shapes=[pltpu.VMEM((n_heads, HGRN_HEAD_DIM, HGRN_HEAD_DIM), F32)],
        compiler_params=_cparams(("arbitrary", "arbitrary")),
        name="hgrn2",
    )(hq, hf, hi, hg, lb, gain)


def _attn_kernel(qc_ref, kc_ref, kp_ref, vc_ref, vp_ref, o_ref, lse_ref, *, group, scale):
    blk = ATTN_BLOCK
    n_pairs = qc_ref.shape[-1] // LANES
    lane = lax.broadcasted_iota(jnp.int32, (1, LANES), 1)
    low = lane < ATTN_HEAD_DIM
    qi = lax.broadcasted_iota(jnp.int32, (blk, 2 * blk), 0)
    ki = lax.broadcasted_iota(jnp.int32, (blk, 2 * blk), 1)
    band = ((ki >= blk) & (ki - blk <= qi)) | ((ki < blk) & (ki >= qi))
    first_key = jnp.where(pl.program_id(2) == 0, blk, 0)
    band_first = band & (ki >= first_key)

    for g in range(group):
        rows = slice(g * blk, (g + 1) * blk)
        prev = slice((g - 1) * blk, g * blk)
        mask = band_first if g == 0 else band
        for pr in range(n_pairs):
            sl = slice(pr * LANES, (pr + 1) * LANES)
            q = qc_ref[0, rows, sl]
            if g == 0:
                k2 = jnp.concatenate([kp_ref[0, :, sl], kc_ref[0, rows, sl]], axis=0)
                v2 = jnp.concatenate([vp_ref[0, :, sl], vc_ref[0, rows, sl]], axis=0)
            else:
                k2 = jnp.concatenate([kc_ref[0, prev, sl], kc_ref[0, rows, sl]], axis=0)
                v2 = jnp.concatenate([vc_ref[0, prev, sl], vc_ref[0, rows, sl]], axis=0)
            acc = None
            lse = None
            for half in (low, jnp.logical_not(low)):
                qh = jnp.where(half, q, jnp.zeros_like(q))
                vh = jnp.where(half, v2, jnp.zeros_like(v2))
                s = lax.dot_general(qh, k2, _NT, preferred_element_type=F32) * scale
                s = jnp.where(mask, s, -jnp.inf)
                m = jnp.max(s, axis=-1, keepdims=True)
                p = jnp.exp(s - m)
                den = jnp.sum(p, axis=-1, keepdims=True)
                oh = jnp.dot(p.astype(BF16), vh, preferred_element_type=F32) / den
                lh = m + jnp.log(den)
                acc = oh if acc is None else acc + oh
                lse = jnp.broadcast_to(lh, (blk, LANES)) if lse is None else jnp.where(low, lse, lh)
            o_ref[0, rows, sl] = acc.astype(o_ref.dtype)
            lse_ref[0, rows, sl] = lse


def _dilated_attention(aq, ak, av, dil, group):
    b, m, width = aq.shape
    width //= dil
    nb = m // ATTN_BLOCK
    blocks = group
    group = min(blocks, nb)
    classes = min(dil, blocks // group)
    tq = group * ATTN_BLOCK
    cur = pl.BlockSpec((1, tq, classes * width), lambda bi, r, n: (bi, n, r))
    prv = pl.BlockSpec((1, ATTN_BLOCK, classes * width), lambda bi, r, n: (bi, jnp.maximum(n * group - 1, 0), r))
    o, lse = pl.pallas_call(
        functools.partial(_attn_kernel, group=group, scale=ATTN_HEAD_DIM ** -0.5),
        grid=(b, dil // classes, nb // group),
        in_specs=[cur, cur, prv, cur, prv],
        out_specs=[cur, cur],
        out_shape=[jax.ShapeDtypeStruct((b, m, dil * width), BF16),
                   jax.ShapeDtypeStruct((b, m, dil * width), F32)],
        compiler_params=_cparams(("arbitrary", "arbitrary", "arbitrary")),
        name=f"dilated_attn_d{dil}",
    )(aq, ak, ak, av, av)
    return o, lse


def _split_bf16(v):
    hi = v.astype(BF16)
    return hi, (v - hi.astype(F32)).astype(BF16)


def _out_proj_kernel(oa_ref, *refs, dils, tile_t):
    n_pat = len(dils)
    o_refs, l_refs = refs[:n_pat], refs[n_pat:2 * n_pat]
    (x_ref, gb_ref, hm_ref, wa_ref, wb_ref, g2_ref, rw_ref, rb_ref,
     x1_ref, h2_ref) = refs[2 * n_pat:2 * n_pat + 10]
    route_refs = refs[2 * n_pat + 10:2 * n_pat + 18]
    carry_ref = refs[2 * n_pat + 18]
    slabs = refs[2 * n_pat + 19:]
    ratio = min(d for d in dils if d > 1)

    @pl.when(pl.program_id(0) == 0)
    def _():
        carry_ref[...] = jnp.zeros_like(carry_ref)

    outs = [_from_classes(o_refs[p], slabs[4 * p], slabs[4 * p + 1], dils[p], ratio) for p in range(n_pat)]
    lses = [_from_classes(l_refs[p], slabs[4 * p + 2], slabs[4 * p + 3], dils[p], ratio) for p in range(n_pat)]
    mx = functools.reduce(jnp.maximum, lses)
    es = [jnp.exp(l - mx) for l in lses]
    ob = sum(e * o for e, o in zip(es, outs)) / sum(es)
    sq_hi, sq_lo = _split_bf16(ob * ob)
    ms = (jnp.dot(sq_hi, hm_ref[...], preferred_element_type=F32)
          + jnp.dot(sq_lo, hm_ref[...], preferred_element_type=F32))
    obn = (ob * lax.rsqrt(ms + NORM_EPS) * gb_ref[...]).astype(BF16)
    y = jnp.dot(oa_ref[...], wa_ref[...], preferred_element_type=F32)
    y = y + jnp.dot(obn, wb_ref[...], preferred_element_type=F32)
    x1 = x_ref[...] + y
    x1_ref[...] = x1
    h2 = (x1 * lax.rsqrt(jnp.mean(x1 * x1, axis=-1, keepdims=True) + NORM_EPS)) * g2_ref[...]
    h_hi, h_lo = _split_bf16(h2)
    h2_ref[...] = h_hi
    hi_parts = jnp.dot(h_hi, rw_ref[...], preferred_element_type=F32)
    lg = (hi_parts[:, :LANES] + hi_parts[:, LANES:]
          + jnp.dot(h_lo, rw_ref[:, :LANES], preferred_element_type=F32)) + rb_ref[...]
    for u in range(lg.shape[0] // tile_t):
        _route_tile(lg[u * tile_t:(u + 1) * tile_t].T, u, *route_refs, carry_ref)


def _out_proj(oa, obs, lses, dils, x2, gain_b, head_mean, w_a, w_b, gain2, rw_split, rb, tile_m, tile_t,
              n_experts):
    n, d = x2.shape
    wm = oa.shape[1]
    nt, per_step = n // tile_t, tile_m // tile_t
    ne = pl.cdiv(n_experts, SUBLANES) * SUBLANES
    row = lambda w: pl.BlockSpec((tile_m, w), lambda i: (i, 0))
    cls = [pl.BlockSpec((tile_m // dil, dil * wm), lambda i: (i, 0)) for dil in dils]
    full = lambda a: pl.BlockSpec(a.shape, lambda i: (0, 0))
    tab = pl.BlockSpec((per_step, ne, LANES), lambda i: (i, 0, 0))
    tab_shape = jax.ShapeDtypeStruct((nt, ne, LANES), jnp.int32)
    consts = (gain_b, head_mean, w_a, w_b, gain2, rw_split, rb)
    return pl.pallas_call(
        functools.partial(_out_proj_kernel, dils=dils, tile_t=tile_t),
        grid=(n // tile_m,),
        in_specs=[row(wm)] + cls + cls + [row(d)] + [full(a) for a in consts],
        out_specs=[row(d), row(d), row(LANES), row(LANES),
                   pl.BlockSpec((per_step * SUBLANES, tile_t), lambda i: (i, 0)), tab, tab, tab, tab,
                   pl.BlockSpec((ne, LANES), lambda i: (0, 0))],
        out_shape=[jax.ShapeDtypeStruct((n, d), F32), jax.ShapeDtypeStruct((n, d), BF16),
                   jax.ShapeDtypeStruct((n, LANES), F32), jax.ShapeDtypeStruct((n, LANES), jnp.int32),
                   jax.ShapeDtypeStruct((nt * SUBLANES, tile_t), jnp.int32), tab_shape, tab_shape, tab_shape,
                   tab_shape, jax.ShapeDtypeStruct((ne, LANES), jnp.int32)],
        scratch_shapes=[pltpu.VMEM((ne, LANES), F32)]
        + [pltpu.VMEM((wm // LANES, tile_m, LANES), F32) for _ in range(4 * len(dils))],
        compiler_params=_cparams(("arbitrary",)),
        name="out_proj",
    )(oa, *obs, *lses, x2, *consts)


def _route_tile(logits_t, u, gate_ref, pos_ref, post_ref, start_ref, nch_ref, nfull_ref, off_ref, cnt_ref,
                carry_ref):
    ne, tt = carry_ref.shape[0], logits_t.shape[1]

    logits = logits_t[0:ne, :]
    erow = lax.broadcasted_iota(jnp.int32, (ne, tt), 0).astype(F32)
    vals, idxs = [], []
    for _ in range(TOP_K):
        m = jnp.max(logits, axis=0, keepdims=True)
        ix = jnp.min(jnp.where(logits == m, erow, float(ne)), axis=0, keepdims=True)
        vals.append(m)
        idxs.append(ix)
        logits = jnp.where(erow == ix, -jnp.inf, logits)
    exps = [jnp.exp(v - vals[0]) for v in vals]
    den = exps[0] + exps[1] + exps[2] + exps[3]
    chosen = jnp.zeros((ne, tt), F32)
    for ix in idxs:
        chosen = chosen + jnp.where(erow == ix, 1.0, 0.0)
    s = lax.broadcasted_iota(jnp.int32, (tt, tt), 0)
    t = lax.broadcasted_iota(jnp.int32, (tt, tt), 1)
    before = jnp.dot(chosen.astype(BF16), jnp.where(s < t, 1.0, 0.0).astype(BF16), preferred_element_type=F32)
    count = jnp.broadcast_to(jnp.sum(chosen, axis=1, keepdims=True), (ne, LANES))
    carry = carry_ref[...]
    head = carry - MOE_CHUNK * jnp.floor(carry * (1.0 / MOE_CHUNK))
    nfull = jnp.floor((head + count) * (1.0 / MOE_CHUNK))
    nch = jnp.floor((head + count + (MOE_CHUNK - 1)) * (1.0 / MOE_CHUNK))
    er = lax.broadcasted_iota(jnp.int32, (ne, ne), 0)
    ec = lax.broadcasted_iota(jnp.int32, (ne, ne), 1)
    off = MOE_CHUNK * jnp.dot(jnp.where(ec < er, 1.0, 0.0).astype(BF16), nch.astype(BF16),
                              preferred_element_type=F32)
    slot = before + jnp.concatenate([off + head] * (tt // LANES), axis=1)
    krow = lax.broadcasted_iota(jnp.int32, (LANES, tt), 0)
    gate_t = jnp.zeros((LANES, tt), F32)
    pos_t = jnp.zeros((LANES, tt), F32)
    for k in range(TOP_K):
        pk = jnp.sum(jnp.where(erow == idxs[k], slot, 0.0), axis=0, keepdims=True)
        gate_t = jnp.where(krow == k, exps[k] / den, gate_t)
        pos_t = jnp.where(krow == k, pk, pos_t)
    gate_ref[u * tt:(u + 1) * tt, :] = gate_t.T
    pos_ref[u * tt:(u + 1) * tt, :] = pos_t.T.astype(jnp.int32)
    post_ref[u * SUBLANES:(u + 1) * SUBLANES, :] = pos_t[0:SUBLANES].astype(jnp.int32)
    start_ref[u] = (carry - head).astype(jnp.int32)
    nch_ref[u] = nch.astype(jnp.int32)
    nfull_ref[u] = nfull.astype(jnp.int32)
    off_ref[u] = off.astype(jnp.int32)
    carry_ref[...] = carry + count
    cnt_ref[...] = (carry + count).astype(jnp.int32)


def _for_each_chunk(tab_ref, n, fn):
    def one(c):
        fn(pl.multiple_of(c * MOE_CHUNK, MOE_CHUNK), pl.multiple_of(tab_ref[0, 0, c], MOE_CHUNK))

    def four(q, carry):
        for u in range(4):
            one(q * 4 + u)
        return carry

    def single(c, carry):
        one(c)
        return carry

    lax.fori_loop(0, lax.shift_right_logical(n, 2), four, 0)
    lax.fori_loop(n & ~3, n, single, 0)


def _pad_sizes():
    sizes, s = [], MOE_ROWS // 2
    while s >= MOE_CHUNK:
        sizes.append(s)
        s //= 2
    return sizes


def _scatter_kernel(nch_ref, nfull_ref, off_ref, total_ref, padrow_ref, padlen_ref, nused_ref, post_ref, tab_ref,
                    tabp_ref, h_ref, xout_ref, stage_ref, open_ref, zero_ref, sems, zsem, *, n_experts):
    rs, tt = stage_ref.shape[1], h_ref.shape[0]
    i, nt = pl.program_id(0), pl.num_programs(0)
    slot = i % 2
    base = i * n_experts
    stage = stage_ref.at[slot]

    @pl.when(i == 0)
    def _():
        open_ref[...] = jnp.zeros_like(open_ref)
        zero_ref[...] = jnp.zeros_like(zero_ref)
        stage_ref[...] = jnp.zeros_like(stage_ref)

    post = post_ref[...]
    main = _stage_main_rows(tt, n_experts)

    def select_rows(r0, r1):
        srow = (lax.broadcasted_iota(jnp.int32, (r1 - r0, tt), 0) + r0).astype(jnp.int16)
        post16 = post.astype(jnp.int16)
        hit = srow == post16[0:1]
        for k in range(1, TOP_K):
            hit = hit | (srow == post16[k:k + 1])
        sel = jnp.where(hit, jnp.ones((), BF16), jnp.zeros((), BF16))
        stage[r0:r1, :] = jnp.dot(sel, h_ref[...], preferred_element_type=F32)

    select_rows(0, main)
    if main < rs:
        @pl.when(total_ref[i] * MOE_CHUNK > main)
        def _():
            select_rows(main, rs)

    def add_open(e, carry):
        rows = pl.ds(pl.multiple_of(off_ref[base + e], MOE_CHUNK), MOE_CHUNK)
        stage[rows, :] = stage[rows, :] + open_ref[e]
        return carry

    def save_open(e, carry):
        nfull = nfull_ref[base + e]
        rows = pl.ds(pl.multiple_of(off_ref[base + e] + nfull * MOE_CHUNK, MOE_CHUNK), MOE_CHUNK)
        still_open = nch_ref[base + e] > nfull
        open_ref[e] = jnp.where(still_open, stage[rows, :], 0.0)
        return carry

    def chunk_copy(sl, srow0, xrow0):
        return pltpu.make_async_copy(stage_ref.at[sl, pl.ds(srow0, MOE_CHUNK)],
                                     xout_ref.at[pl.ds(xrow0, MOE_CHUNK)], sems.at[sl])

    def pad_copies(fn):
        def zeros_to(start, size):
            fn(pltpu.make_async_copy(zero_ref.at[pl.ds(0, size)], xout_ref.at[pl.ds(start, size)], zsem))

        def expert(e, carry):
            row, length = pl.multiple_of(padrow_ref[e], MOE_CHUNK), padlen_ref[e]
            for size in _pad_sizes():
                @pl.when((length & size) != 0)
                def _():
                    zeros_to(pl.multiple_of(row + (length & ~(2 * size - 1)), MOE_CHUNK), size)
            return carry

        def unused_half_block(c, carry):
            zeros_to(pl.multiple_of(c * (MOE_ROWS // 2), MOE_ROWS // 2), MOE_ROWS // 2)
            return carry

        lax.fori_loop(0, n_experts, expert, 0)
        lax.fori_loop(2 * nused_ref[0], xout_ref.shape[0] // (MOE_ROWS // 2), unused_half_block, 0)

    lax.fori_loop(0, n_experts, add_open, 0, unroll=8)

    @pl.when(i > 0)
    def _():
        _for_each_chunk(tabp_ref, total_ref[i - 1], lambda s0, x0: chunk_copy(1 - slot, s0, x0).wait())

    _for_each_chunk(tab_ref, total_ref[i], lambda s0, x0: chunk_copy(slot, s0, x0).start())
    lax.fori_loop(0, n_experts, save_open, 0, unroll=8)

    @pl.when(i == nt - 1)
    def _():
        pad_copies(lambda cp: cp.start())
        _for_each_chunk(tab_ref, total_ref[i], lambda s0, x0: chunk_copy(slot, s0, x0).wait())
        pad_copies(lambda cp: cp.wait())


def _stage_main_rows(tile_t, n_experts):
    return min(pl.cdiv(tile_t * TOP_K + n_experts * (MOE_CHUNK - 1), LANES) * LANES, _stage_rows(tile_t, n_experts))


def _stage_rows(tile_t, n_experts):
    rows = pl.cdiv(tile_t * TOP_K + 2 * n_experts * (MOE_CHUNK - 1) + MOE_CHUNK, LANES) * LANES
    assert rows < 2 ** 15
    return rows


def _chunk_table_spec(table, shift):
    nt, _, width = table.shape
    return pl.BlockSpec((1, 1, width), lambda i, *_: (jnp.clip(i + shift, 0, nt - 1), 0, 0), memory_space=pltpu.SMEM)


def _moe_scatter(nch, nfull, off, total, padrow, padlen, n_used, post, table, h2, n_rows, tile_t, n_experts):
    n, d = h2.shape
    tab = lambda shift: _chunk_table_spec(table, shift)
    return pl.pallas_call(
        functools.partial(_scatter_kernel, n_experts=n_experts),
        grid_spec=pltpu.PrefetchScalarGridSpec(
            num_scalar_prefetch=7,
            grid=(n // tile_t,),
            in_specs=[pl.BlockSpec((SUBLANES, tile_t), lambda i, *_: (i, 0)), tab(0), tab(-1),
                      pl.BlockSpec((tile_t, d), lambda i, *_: (i, 0))],
            out_specs=pl.BlockSpec(memory_space=pl.ANY),
            scratch_shapes=[pltpu.VMEM((2, _stage_rows(tile_t, n_experts), d), F32),
                            pltpu.VMEM((n_experts, MOE_CHUNK, d), F32),
                            pltpu.VMEM((MOE_ROWS // 2, d), F32),
                            pltpu.SemaphoreType.DMA((2,)), pltpu.SemaphoreType.DMA]),
        out_shape=jax.ShapeDtypeStruct((n_rows, d), F32),
        compiler_params=_cparams(("arbitrary",)),
        name="moe_scatter",
    )(nch, nfull, off, total, padrow, padlen, n_used, post, table, table, h2)


def _expert_kernel(be_ref, nu_ref, valid_ref, slot_ref, next_ref, x_ref, bg_ref, bl_ref, bd_ref, wu_hbm, wd_hbm,
                   y_ref, wu_buf, wd_buf, wg_s, wl_s, wd_s, sems):
    i = pl.program_id(0)
    used = i < nu_ref[0]
    e = be_ref[i]
    new_expert = (i == 0) | (e != be_ref[jnp.maximum(i - 1, 0)])
    slot = slot_ref[e]

    def weight_copies(expert, sl):
        return (pltpu.make_async_copy(wu_hbm.at[expert], wu_buf.at[sl], sems.at[0, sl]),
                pltpu.make_async_copy(wd_hbm.at[expert], wd_buf.at[sl], sems.at[1, sl]))

    @pl.when(used & (i == 0))
    def _():
        for cp in weight_copies(e, slot):
            cp.start()

    @pl.when(used & new_expert)
    def _():
        for cp in weight_copies(e, slot):
            cp.wait()
        nxt = next_ref[e]

        @pl.when(nxt >= 0)
        def _():
            for cp in weight_copies(nxt, 1 - slot):
                cp.start()

        r = lax.broadcasted_iota(jnp.int32, (2 * LANES, 2 * LANES), 0)
        c = lax.broadcasted_iota(jnp.int32, (2 * LANES, 2 * LANES), 1)
        pick = jnp.where(r == jnp.where(c < LANES, 2 * c, 2 * (c - LANES) + 1), 1.0, 0.0).astype(BF16)
        for cb in range(wg_s.shape[1] // LANES):
            grp = wu_buf[slot, :, cb * 2 * LANES:(cb + 1) * 2 * LANES].astype(BF16)
            out = slice(cb * LANES, (cb + 1) * LANES)
            both = jnp.dot(grp, pick, preferred_element_type=F32).astype(BF16)
            wg_s[:, out] = both[:, :LANES]
            wl_s[:, out] = both[:, LANES:]
        wd_s[...] = wd_buf[slot].astype(BF16)

    def ffn(rows):
        x = x_ref[rows, :].astype(BF16)
        glu = jnp.dot(x, wg_s[...], preferred_element_type=F32) + bg_ref[0]
        lin = jnp.dot(x, wl_s[...], preferred_element_type=F32) + bl_ref[0]
        glu = jnp.minimum(glu, SWIGLU_LIMIT)
        lin = jnp.clip(lin, -SWIGLU_LIMIT, SWIGLU_LIMIT)
        h = glu * jax.nn.sigmoid(SWIGLU_ALPHA * glu) * (lin + 1.0)
        y_ref[rows, :] = jnp.dot(h.astype(BF16), wd_s[...], preferred_element_type=F32) + bd_ref[0]

    half = MOE_ROWS // 2
    full = used & (valid_ref[i] > half)

    @pl.when(full)
    def _():
        ffn(slice(0, MOE_ROWS))

    @pl.when(used & jnp.logical_not(full))
    def _():
        ffn(slice(0, half))
        y_ref[half:, :] = jnp.zeros((MOE_ROWS - half, y_ref.shape[1]), y_ref.dtype)

    @pl.when(jnp.logical_not(used))
    def _():
        y_ref[...] = jnp.zeros_like(y_ref)


def _moe_experts(block_e, n_used, valid, buf_slot, next_expert, xbuf, w_up, bg, bl, w_down, bd):
    n_rows, d = xbuf.shape
    de = w_down.shape[1]
    nblk = n_rows // MOE_ROWS
    rows = pl.BlockSpec((MOE_ROWS, d), lambda i, be, nu, *_: (i, 0))
    rows_in = pl.BlockSpec((MOE_ROWS, d), lambda i, be, nu, *_: (jnp.minimum(i, jnp.maximum(nu[0] - 1, 0)), 0))
    bias = lambda a: pl.BlockSpec((1,) + a.shape[1:], lambda i, be, *_: (be[i], 0, 0))
    hbm = pl.BlockSpec(memory_space=pl.ANY)
    return pl.pallas_call(
        _expert_kernel,
        grid_spec=pltpu.PrefetchScalarGridSpec(
            num_scalar_prefetch=5,
            grid=(nblk,),
            in_specs=[rows_in, bias(bg), bias(bl), bias(bd), hbm, hbm],
            out_specs=rows,
            scratch_shapes=[pltpu.VMEM((2,) + w_up.shape[1:], F32), pltpu.VMEM((2,) + w_down.shape[1:], F32),
                            pltpu.VMEM((d, de), BF16), pltpu.VMEM((d, de), BF16), pltpu.VMEM((de, d), BF16),
                            pltpu.SemaphoreType.DMA((2, 2))]),
        out_shape=jax.ShapeDtypeStruct((n_rows, d), F32),
        compiler_params=_cparams(("arbitrary",)),
        name="moe_experts",
    )(block_e, n_used, valid, buf_slot, next_expert, xbuf, bg, bl, bd, w_up, w_down)


def _combine_kernel(total_ref, pos_ref, gate_ref, x1_ref, g_ref, tab_ref, tabn_ref, y_ref, o_ref, stage_ref, acc_ref,
                    sems, *, n_experts):
    rs, tt = stage_ref.shape[1], x1_ref.shape[0]
    i, nt = pl.program_id(0), pl.num_programs(0)
    slot = i % 2

    def chunk_copy(sl, srow0, yrow0):
        return pltpu.make_async_copy(y_ref.at[pl.ds(yrow0, MOE_CHUNK)],
                                     stage_ref.at[sl, pl.ds(srow0, MOE_CHUNK)], sems.at[sl])

    @pl.when(i == 0)
    def _():
        stage_ref[...] = jnp.zeros_like(stage_ref)
        _for_each_chunk(tab_ref, total_ref[i], lambda s0, y0: chunk_copy(slot, s0, y0).start())

    @pl.when(i + 1 < nt)
    def _():
        _for_each_chunk(tabn_ref, total_ref[i + 1], lambda s0, y0: chunk_copy(1 - slot, s0, y0).start())

    pos, gates = pos_ref[...], gate_ref[...]
    main = _stage_main_rows(tt, n_experts)

    def weighted_rows(r0, r1):
        scol = (lax.broadcasted_iota(jnp.int32, (tt, r1 - r0), 1) + r0).astype(jnp.int16)
        pos16, gates16 = pos.astype(jnp.int16), gates.astype(BF16)
        weights = jnp.zeros((tt, r1 - r0), BF16)
        for k in range(TOP_K):
            weights = jnp.where(scol == pos16[:, k:k + 1], gates16[:, k:k + 1], weights)
        return jnp.dot(weights, stage_ref[slot, r0:r1, :].astype(BF16), preferred_element_type=F32)

    _for_each_chunk(tab_ref, total_ref[i], lambda s0, y0: chunk_copy(slot, s0, y0).wait())
    acc_ref[...] = x1_ref[...] + weighted_rows(0, main)
    if main < rs:
        @pl.when(total_ref[i] * MOE_CHUNK > main)
        def _():
            acc_ref[...] += weighted_rows(main, rs)
    acc = acc_ref[...]
    o_ref[...] = (acc * lax.rsqrt(jnp.mean(acc * acc, axis=-1, keepdims=True) + NORM_EPS)) * g_ref[...]


def _moe_combine(total, pos, gates, x1, gain, table, ybuf, tile_t, n_experts):
    n, d = x1.shape
    return pl.pallas_call(
        functools.partial(_combine_kernel, n_experts=n_experts),
        grid_spec=pltpu.PrefetchScalarGridSpec(
            num_scalar_prefetch=1,
            grid=(n // tile_t,),
            in_specs=[pl.BlockSpec((tile_t, LANES), lambda i, *_: (i, 0)),
                      pl.BlockSpec((tile_t, LANES), lambda i, *_: (i, 0)),
                      pl.BlockSpec((tile_t, d), lambda i, *_: (i, 0)),
                      pl.BlockSpec((1, d), lambda i, *_: (0, 0)),
                      _chunk_table_spec(table, 0), _chunk_table_spec(table, 1),
                      pl.BlockSpec(memory_space=pl.ANY)],
            out_specs=pl.BlockSpec((tile_t, d), lambda i, *_: (i, 0)),
            scratch_shapes=[pltpu.VMEM((2, _stage_rows(tile_t, n_experts), d), F32),
                            pltpu.VMEM((tile_t, d), F32), pltpu.SemaphoreType.DMA((2,))]),
        out_shape=jax.ShapeDtypeStruct((n, d), F32),
        compiler_params=_cparams(("arbitrary",)),
        name="moe_combine",
    )(total, pos, gates, x1, gain, table, table, ybuf)


def kernel(x, norm1_g, w_in, hgrn_lb_logits, hgrn_norm_g, attn_norm_g, w_out, norm2_g, router_w, router_b,
           w_up, b_up, w_down, b_down, final_norm_g):
    b, s, d = x.shape
    n = b * s
    depth = w_in.shape[0]
    n_experts = router_w.shape[-1]
    d_mix = w_out.shape[1]
    d_hgrn = d_mix // 2
    d_attn = d_mix - d_hgrn
    assert w_in.shape[-1] == 4 * d_hgrn + 3 * d_attn and d_hgrn == d_attn
    assert all(s % win == 0 and win // dil == ATTN_BLOCK for win, dil in DILATED_PATTERNS)
    assert s % HGRN_ROWS == 0 and n % IN_PROJ_ROWS == 0 and n % PROJ_ROWS == 0 and PROJ_ROWS % ROUTE_TOKENS == 0
    assert n_experts <= LANES
    assert depth == 1, "the final rmsnorm is fused into the single layer's MoE combine"

    lb_all = jnp.cumsum(jax.nn.softmax(hgrn_lb_logits.astype(F32), axis=0), axis=0)
    lane = jnp.arange(d_attn)
    head_mean = jnp.where((lane[:, None] // ATTN_HEAD_DIM) == (lane[None, :] // ATTN_HEAD_DIM),
                          1.0 / ATTN_HEAD_DIM, 0.0).astype(BF16)
    x2 = x.reshape(n, d)
    l = 0
    dils = tuple(dil for _, dil in DILATED_PATTERNS)
    n_pat = len(dils)
    hq, hf, hi, hg, *attn = _in_proj(
        x2, norm1_g[l].reshape(1, d), w_in[l].astype(BF16),
        (BF16, F32, BF16, BF16), 3, dils, tile_m=IN_PROJ_ROWS)
    to3 = lambda t: t.reshape(b, s, -1)
    o_a = _hgrn2(to3(hq), to3(hf), to3(hi), to3(hg), lb_all[l].reshape(1, d_hgrn),
                 hgrn_norm_g[l].reshape(1, d_hgrn), tile_s=HGRN_ROWS)
    obs, lses = [], []
    for p, dil in enumerate(dils):
        aq, ak, av = (attn[a * n_pat + p].reshape(b, s // dil, dil * d_attn) for a in range(3))
        o_p, lse_p = _dilated_attention(aq, ak, av, dil, group=ATTN_BLOCKS)
        obs.append(o_p.reshape(n // dil, dil * d_attn))
        lses.append(lse_p.reshape(n // dil, dil * d_attn))
    rw = jnp.zeros((d, LANES), F32).at[:, :n_experts].set(router_w[l])
    rw_hi = rw.astype(BF16)
    rw_split = jnp.concatenate([rw_hi, (rw - rw_hi.astype(F32)).astype(BF16)], axis=1)
    rb =jnp.full((1, LANES), -jnp.inf, F32).at[0, :n_experts].set(router_b[l])
    w_o = w_out[l].astype(BF16)
    tile_t = ROUTE_TOKENS
    x1, h2, gates, pos, post, start, nch, nfull, off, counts = _out_proj(
        o_a.reshape(n, d_hgrn), obs, lses, dils, x2, attn_norm_g[l].reshape(1, d_attn), head_mean,
        w_o[:d_hgrn], w_o[d_hgrn:], norm2_g[l].reshape(1, d), rw_split, rb, PROJ_ROWS, tile_t, n_experts)

    counts = counts[:n_experts, 0]
    padded = (counts + MOE_ROWS - 1) // MOE_ROWS * MOE_ROWS
    pad_end = jnp.cumsum(padded)
    pad_start = pad_end - padded
    n_rows = n * TOP_K + n_experts * MOE_ROWS
    nblk = n_rows // MOE_ROWS
    block_row = jnp.arange(nblk, dtype=jnp.int32) * MOE_ROWS
    block_e = jnp.minimum(jnp.sum(pad_end[None, :] <= block_row[:, None], axis=1), n_experts - 1).astype(jnp.int32)
    n_used = (pad_end[-1:] // MOE_ROWS).astype(jnp.int32)
    eid = jnp.arange(n_experts, dtype=jnp.int32)
    group_end = jnp.sum(jnp.where(eid[None, :] == block_e[:, None], (pad_start + counts)[None, :], 0), axis=1)
    valid = jnp.clip(group_end - block_row, 0, MOE_ROWS).astype(jnp.int32)
    has_rows = padded > 0
    buf_slot = ((jnp.cumsum(has_rows) - 1) % 2).astype(jnp.int32)
    following = jnp.min(jnp.where(has_rows[None, :] & (eid[None, :] > eid[:, None]), eid[None, :], n_experts), axis=1)
    next_expert = jnp.where(following < n_experts, following, -1).astype(jnp.int32)
    per_expert = lambda t: t[:, :n_experts, 0]
    seg = per_expert(start) + pad_start[None, :].astype(jnp.int32)
    nch, nfull, off = per_expert(nch), per_expert(nfull), per_expert(off)
    total = jnp.sum(nch, axis=1).astype(jnp.int32)
    first = off // MOE_CHUNK
    max_chunks = pl.cdiv(_stage_rows(tile_t, n_experts) // MOE_CHUNK, LANES) * LANES
    chunk = jnp.arange(max_chunks, dtype=jnp.int32)[None, :, None]
    owned = (first[:, None, :] <= chunk) & (chunk < (first + nch)[:, None, :])
    table = jnp.sum(jnp.where(owned, (seg - first * MOE_CHUNK)[:, None, :], 0), axis=-1) + chunk[:, :, 0] * MOE_CHUNK
    table = table.astype(jnp.int32)[:, None, :]
    filled = (counts + MOE_CHUNK - 1) // MOE_CHUNK * MOE_CHUNK
    padrow = (pad_start + filled).astype(jnp.int32)
    padlen = (padded - filled).astype(jnp.int32)

    xbuf = _moe_scatter(nch.reshape(-1), nfull.reshape(-1), off.reshape(-1), total, padrow, padlen, n_used, post,
                        table, h2, n_rows, tile_t, n_experts)
    ybuf = _moe_experts(block_e, n_used, valid, buf_slot, next_expert, xbuf, w_up[l], b_up[l][:, None, 0::2],
                        b_up[l][:, None, 1::2], w_down[l], b_down[l][:, None, :])
    out = _moe_combine(total, pos, gates, x1, final_norm_g.reshape(1, d), table, ybuf, tile_t, n_experts)
    return out.reshape(b, s, d)
```
